```python
import math
import jax, jax.numpy as jnp
from jax import lax
import numpy as np

D_MODEL = 1024
BATCH = 32
SEQ = 256
DEPTH = 1
DEC_BATCH = 8
DEC_SEQ = 2048
PAST_LEN = 512

GRID_W = 64
POS_BASE = 10000.0
D_MLSTM = 1024
MLSTM_HEADS = 4
MLSTM_HEAD_DIM = D_MLSTM // MLSTM_HEADS
CHUNK = 128
D_CONV = 1024
CONV_WIDTH = 31
GATE_OFF = 4 * D_MLSTM
N_GATES = 4 * MLSTM_HEADS
N_IN = 4 * D_MLSTM + N_GATES + 2 * D_CONV + 2 * D_MODEL
SPLITS = [D_MLSTM, 2 * D_MLSTM, 3 * D_MLSTM, 4 * D_MLSTM, GATE_OFF + N_GATES, GATE_OFF + N_GATES + D_CONV, GATE_OFF + N_GATES + 2 * D_CONV, GATE_OFF + N_GATES + 2 * D_CONV + D_MODEL]
PEER_HEADS = 8
PEER_NKEYS = 128
PEER_EXPERTS = PEER_NKEYS ** 2
PEER_DKEY = 256
PEER_TOPK = 16
PEER_BLOCK = 128
ALPHA = (2.0 * DEPTH) ** 0.25
BETA = (8.0 * DEPTH) ** -0.25
LN_EPS = 1e-6

kernel_name = 'hybrid_mlstm_conformer_peer_diffusion_step'


def layer_norm(x, w, b):
    xf = x.astype(jnp.float32)
    mu = xf.mean(-1, keepdims=True)
    var = jnp.square(xf - mu).mean(-1, keepdims=True)
    return ((xf - mu) * lax.rsqrt(var + LN_EPS)).astype(x.dtype) * w + b


def grid_pos_embed(n_tokens, dtype):
    rows = n_tokens // GRID_W
    quarter = D_MODEL // 4
    freqs = jnp.exp(-math.log(POS_BASE) * jnp.arange(quarter, dtype=jnp.float32) / quarter)
    r = jnp.arange(rows, dtype=jnp.float32)[:, None] * freqs
    cl = jnp.arange(GRID_W, dtype=jnp.float32)[:, None] * freqs
    er = jnp.concatenate([jnp.sin(r), jnp.cos(r)], -1)
    ec = jnp.concatenate([jnp.sin(cl), jnp.cos(cl)], -1)
    emb = jnp.concatenate([jnp.broadcast_to(er[:, None, :], (rows, GRID_W, D_MODEL // 2)),
                           jnp.broadcast_to(ec[None, :, :], (rows, GRID_W, D_MODEL // 2))], -1)
    return emb.reshape(rows * GRID_W, D_MODEL).astype(dtype)


def mlstm_scan(q, k, v, ig, lf, C0, n0, m0):
    B, S, H, Dh = q.shape
    nc = S // CHUNK

    def to_chunks(a):
        a = a.reshape((B, nc, CHUNK) + a.shape[2:])
        return jnp.moveaxis(jnp.moveaxis(a, 1, 0), 2, 3)

    mask = jnp.tril(jnp.ones((CHUNK, CHUNK), dtype=bool))

    def step(carry, xs):
        C, n, m = carry
        qc, kc, vc, ic, fc = xs
        b = jnp.cumsum(fc, axis=-1)
        dmat = b[..., :, None] - b[..., None, :] + ic[..., None, :]
        dmat = jnp.where(mask, dmat, -jnp.inf)
        inter = b + m[..., None]
        m_t = jnp.maximum(inter, dmat.max(-1))
        w_ts = jnp.exp(dmat - m_t[..., None])
        a_t = jnp.exp(inter - m_t)
        s = jnp.einsum('bhtd,bhsd->bhts', qc, kc) * w_ts
        num = jnp.einsum('bhts,bhse->bhte', s, vc) + a_t[..., None] * jnp.einsum('bhed,bhtd->bhte', C, qc)
        den = s.sum(-1) + a_t * jnp.einsum('bhd,bhtd->bht', n, qc)
        h = num / jnp.maximum(jnp.abs(den), jnp.exp(-m_t))[..., None]
        b_last = b[..., -1]
        dec = b_last[..., None] - b + ic
        m_new = jnp.maximum(b_last + m, dec.max(-1))
        w_s = jnp.exp(dec - m_new[..., None])
        a_c = jnp.exp(b_last + m - m_new)
        C_new = a_c[..., None, None] * C + jnp.einsum('bhs,bhse,bhsd->bhed', w_s, vc, kc)
        n_new = a_c[..., None] * n + jnp.einsum('bhs,bhsd->bhd', w_s, kc)
        return (C_new, n_new, m_new), h

    xs = (to_chunks(q), to_chunks(k), to_chunks(v), to_chunks(ig), to_chunks(lf))
    (C, n, m), h = lax.scan(step, (C0, n0, m0), xs)
    h = jnp.moveaxis(jnp.moveaxis(h, 3, 2), 0, 1).reshape(B, S, H, Dh)
    return h, (C, n, m)


def mixer(u, w_in, b_in, norm_w, w_a, dw_w, dw_b, cln_w, cln_b, w_cout, w_out, C0, n0, m0):
    B, S, _ = u.shape
    f32 = jnp.float32
    proj = u @ w_in + b_in
    q, k, v, o, gates, glu_val, glu_gate, g_a, g_b = jnp.split(proj, SPLITS, axis=-1)
    hs = (B, S, MLSTM_HEADS, MLSTM_HEAD_DIM)
    q = q.reshape(hs).astype(f32)
    k = k.reshape(hs).astype(f32) * (MLSTM_HEAD_DIM ** -0.5)
    v = v.reshape(hs).astype(f32)
    gates = gates.reshape(B, S, 4, MLSTM_HEADS).astype(f32)
    lf_f = jax.nn.log_sigmoid(gates[:, :, 1])
    lf_b = jax.nn.log_sigmoid(gates[:, :, 3])
    C0 = C0.astype(f32)
    n0 = n0.astype(f32)
    m0 = m0.astype(f32)
    h_f, (Cf, nf, mf) = mlstm_scan(q, k, v, gates[:, :, 0], lf_f, C0[:, 0], n0[:, 0], m0[:, 0])
    rev = lambda a: a[:, ::-1]
    h_b, (Cb, nb, mb) = mlstm_scan(rev(q), rev(k), rev(v), rev(gates[:, :, 2]), rev(lf_b), C0[:, 1], n0[:, 1], m0[:, 1])
    h = h_f + rev(h_b)
    mu = h.mean(-1, keepdims=True)
    var = jnp.square(h - mu).mean(-1, keepdims=True)
    h = ((h - mu) * lax.rsqrt(var + LN_EPS)).reshape(B, S, D_MLSTM).astype(u.dtype) * norm_w
    h = jax.nn.sigmoid(o) * h
    branch_a = h @ w_a
    xc = glu_val * jax.nn.sigmoid(glu_gate)
    xc = lax.conv_general_dilated(xc, dw_w[:, None, :], (1,), [(CONV_WIDTH // 2, CONV_WIDTH // 2)],
                                  dimension_numbers=('NWC', 'WIO', 'NWC'), feature_group_count=D_CONV) + dw_b
    xc = jax.nn.silu(layer_norm(xc, cln_w, cln_b))
    branch_b = xc @ w_cout
    mix = (jax.nn.sigmoid(g_a) * branch_a + jax.nn.sigmoid(g_b) * branch_b) @ w_out
    state = (jnp.stack([Cf, Cb], axis=1), jnp.stack([nf, nb], axis=1), jnp.stack([mf, mb], axis=1))
    return mix, state


def peer(u, w_query, subkeys, table_u, table_v):
    B, S, D = u.shape
    xt = u.reshape(-1, PEER_BLOCK, D)

    def block_fn(xb):
        t = xb.shape[0]
        qry = (xb @ w_query).reshape(t, PEER_HEADS, 2, PEER_DKEY // 2)
        sc = jnp.einsum('thpd,hpkd->thpk', qry, subkeys).astype(jnp.float32)
        s, idx = lax.top_k(sc, PEER_TOPK)
        cand = (s[:, :, 0, :, None] + s[:, :, 1, None, :]).reshape(t, PEER_HEADS, PEER_TOPK * PEER_TOPK)
        cand_idx = (idx[:, :, 0, :, None] * PEER_NKEYS + idx[:, :, 1, None, :]).reshape(t, PEER_HEADS, PEER_TOPK * PEER_TOPK)
        cs, ci = lax.top_k(cand, PEER_TOPK)
        eidx = jnp.take_along_axis(cand_idx, ci, axis=-1)
        g = jax.nn.softmax(cs, axis=-1)
        ue = jnp.take(table_u, eidx, axis=0)
        act = jax.nn.gelu(jnp.einsum('thkd,td->thk', ue, xb))
        ve = jnp.take(table_v, eidx, axis=0)
        return jnp.einsum('thk,thkd->td', (g * act).astype(ve.dtype), ve)

    return lax.map(block_fn, xt).reshape(B, S, D)


def block(x, mod, mixer_params, peer_params, ln_params, state0):
    shift1, scale1, gate1, shift2, scale2, gate2 = jnp.split(mod, 6, axis=-1)
    ln1_w, ln1_b, ln2_w, ln2_b = ln_params
    u = x * (1 + scale1) + shift1
    y, state = mixer(u, *mixer_params, *state0)
    x = layer_norm(ALPHA * x + gate1 * y, ln1_w, ln1_b)
    u = x * (1 + scale2) + shift2
    x = layer_norm(ALPHA * x + gate2 * peer(u, *peer_params), ln2_w, ln2_b)
    return x, state


def setup_inputs(seed: int = 0) -> dict:
    key = jax.random.key(seed)
    ks = jax.random.split(key, 28)
    f32 = jnp.float32
    nrm = lambda k, shape, s: jax.random.normal(k, shape, f32) * s
    L = DEPTH
    H = MLSTM_HEADS
    DH = MLSTM_HEAD_DIM
    forget_bias = jnp.linspace(3.0, 6.0, H)
    b_in = nrm(ks[8], (L, N_IN), 0.02)
    b_in = b_in.at[:, GATE_OFF + H:GATE_OFF + 2 * H].add(forget_bias)
    b_in = b_in.at[:, GATE_OFF + 3 * H:GATE_OFF + 4 * H].add(forget_bias)
    return {
        'x_prompt': nrm(ks[0], (BATCH, SEQ, D_MODEL), 1.0),
        'x_sample': nrm(ks[1], (DEC_BATCH, DEC_SEQ, D_MODEL), 1.0),
        'state_C': nrm(ks[2], (DEC_BATCH, L, 2, H, DH, DH), 0.3),
        'state_n': nrm(ks[3], (DEC_BATCH, L, 2, H, DH), 0.3),
        'state_m': nrm(ks[4], (DEC_BATCH, L, 2, H), 0.5),
        'c': nrm(ks[5], (DEC_BATCH, D_MODEL), 1.0),
        'c_ctx': nrm(ks[6], (D_MODEL,), 1.0),
        'w_in': nrm(ks[7], (L, D_MODEL, N_IN), D_MODEL ** -0.5),
        'b_in': b_in,
        'mlstm_norm_w': 1.0 + nrm(ks[9], (L, D_MLSTM), 0.02),
        'w_a': nrm(ks[10], (L, D_MLSTM, D_MODEL), BETA * D_MLSTM ** -0.5),
        'conv_dw_w': nrm(ks[11], (L, CONV_WIDTH, D_CONV), CONV_WIDTH ** -0.5),
        'conv_dw_b': nrm(ks[12], (L, D_CONV), 0.02),
        'conv_ln_w': 1.0 + nrm(ks[13], (L, D_CONV), 0.02),
        'conv_ln_b': nrm(ks[14], (L, D_CONV), 0.02),
        'w_conv_out': nrm(ks[15], (L, D_CONV, D_MODEL), BETA * D_CONV ** -0.5),
        'w_out': nrm(ks[16], (L, D_MODEL, D_MODEL), BETA * D_MODEL ** -0.5),
        'w_mod': nrm(ks[17], (L, D_MODEL, 6 * D_MODEL), 0.5 * D_MODEL ** -0.5),
        'b_mod': nrm(ks[18], (L, 6 * D_MODEL), 0.02),
        'ln1_w': 1.0 + nrm(ks[19], (L, D_MODEL), 0.02),
        'ln1_b': nrm(ks[20], (L, D_MODEL), 0.02),
        'ln2_w': 1.0 + nrm(ks[21], (L, D_MODEL), 0.02),
        'ln2_b': nrm(ks[22], (L, D_MODEL), 0.02),
        'peer_w_query': nrm(ks[23], (L, D_MODEL, PEER_HEADS * PEER_DKEY), D_MODEL ** -0.5),
        'peer_subkeys': nrm(ks[24], (L, PEER_HEADS, 2, PEER_NKEYS, PEER_DKEY // 2), (PEER_DKEY // 2) ** -0.5),
        'peer_u': nrm(ks[25], (L, PEER_EXPERTS, D_MODEL), D_MODEL ** -0.5),
        'peer_v': nrm(ks[26], (L, PEER_EXPERTS, D_MODEL), BETA),
    }


def reference(x_prompt, x_sample, state_C, state_n, state_m, c, c_ctx, w_in, b_in, mlstm_norm_w, w_a,
              conv_dw_w, conv_dw_b, conv_ln_w, conv_ln_b, w_conv_out, w_out, w_mod, b_mod,
              ln1_w, ln1_b, ln2_w, ln2_b, peer_w_query, peer_subkeys, peer_u, peer_v):
    f32 = jnp.float32
    bsz = x_prompt.shape[0]
    H = MLSTM_HEADS
    DH = MLSTM_HEAD_DIM
    ctx = x_prompt
    lat = x_sample + grid_pos_embed(x_sample.shape[1], x_sample.dtype)
    new_C, new_n, new_m = [], [], []
    for l in range(DEPTH):
        mixer_p = (w_in[l], b_in[l], mlstm_norm_w[l], w_a[l], conv_dw_w[l], conv_dw_b[l],
                   conv_ln_w[l], conv_ln_b[l], w_conv_out[l], w_out[l])
        peer_p = (peer_w_query[l], peer_subkeys[l], peer_u[l], peer_v[l])
        ln_p = (ln1_w[l], ln1_b[l], ln2_w[l], ln2_b[l])
        mod_ctx = (jax.nn.silu(c_ctx) @ w_mod[l] + b_mod[l])[None, None, :]
        mod_lat = (jax.nn.silu(c) @ w_mod[l] + b_mod[l])[:, None, :]
        zero_state = (jnp.zeros((bsz, 2, H, DH, DH), f32), jnp.zeros((bsz, 2, H, DH), f32),
                      jnp.zeros((bsz, 2, H), f32))
        ctx, (cC, cn, cm) = block(ctx, mod_ctx, mixer_p, peer_p, ln_p, zero_state)
        new_C.append(cC)
        new_n.append(cn)
        new_m.append(cm)
        lat, _ = block(lat, mod_lat, mixer_p, peer_p, ln_p, (state_C[:, l], state_n[:, l], state_m[:, l]))
    return (ctx, lat, jnp.stack(new_C, axis=1), jnp.stack(new_n, axis=1), jnp.stack(new_m, axis=1))
```

```python
import functools
import math

import jax
import jax.numpy as jnp
from jax import lax
from jax.experimental import pallas as pl
from jax.experimental.pallas import tpu as pltpu

F32 = jnp.float32
BF16 = jnp.bfloat16

D_MODEL = 1024
N_HEADS = 4
HEAD_DIM = 256
CHUNK = 128
CONV_WIDTH = 31
CONV_HALO = 16
N_GATES = 16
GATE_PAD = 128
GRID_W = 64
POS_BASE = 10000.0
LN_EPS = 1e-6
PEER_HEADS = 8
PEER_NKEYS = 128
PEER_TOPK = 16
V7X_VMEM_LIMIT = 56 * 1024 * 1024


def _params(n_axes):
    return pltpu.CompilerParams(dimension_semantics=("arbitrary",) * n_axes,
                                vmem_limit_bytes=V7X_VMEM_LIMIT)


def _sigmoid(x):
    return 1.0 / (1.0 + jnp.exp(-x))


def _log_sigmoid(x):
    return jnp.minimum(x, 0.0) - jnp.log(1.0 + jnp.exp(-jnp.abs(x)))


def _dot(a, b):
    return jnp.dot(a, b, preferred_element_type=F32)


def _dot_nt(a, b):
    return lax.dot_general(a, b, (((1,), (1,)), ((), ())), preferred_element_type=F32)


def _split2(x):
    hi = x.astype(BF16)
    lo = (x - hi.astype(F32)).astype(BF16)
    return hi, lo


def _split3(x):
    a = x.astype(BF16)
    r = x - a.astype(F32)
    b = r.astype(BF16)
    c = (r - b.astype(F32)).astype(BF16)
    return a, b, c


def _layer_norm(x, w, b):
    mu = jnp.mean(x, axis=-1, keepdims=True)
    xc = x - mu
    var = jnp.mean(xc * xc, axis=-1, keepdims=True)
    return xc * lax.rsqrt(var + LN_EPS) * w + b


def _mod_kernel(c_ref, w_ref, b_ref, o_ref):
    c = c_ref[...]
    s = c * _sigmoid(c)
    sh, sl = _split2(s)
    wh, wl = _split2(w_ref[...])
    o_ref[...] = _dot(sh, wh) + _dot(sl, wh) + _dot(sh, wl) + b_ref[...]


def _modulation(cvec, w_mod, b_mod):
    rows = cvec.shape[0]
    tn = 1536
    n = w_mod.shape[1]
    return pl.pallas_call(
        _mod_kernel,
        grid=(n // tn,),
        in_specs=[pl.BlockSpec((rows, D_MODEL), lambda j: (0, 0)),
                  pl.BlockSpec((D_MODEL, tn), lambda j: (0, j)),
                  pl.BlockSpec((1, tn), lambda j: (0, j))],
        out_specs=pl.BlockSpec((rows, tn), lambda j: (0, j)),
        out_shape=jax.ShapeDtypeStruct((rows, n), F32),
        compiler_params=_params(1),
        name="modulation",
    )(cvec, w_mod, b_mod.reshape(1, n))


_COL_K = 1


def _inproj_kernel(has_pos, *refs):
    if has_pos:
        x_ref, pos_ref, mod_ref, w_ref, b_ref, wgh_ref, wgl_ref, bg_ref, proj_ref, gates_ref, u_scr = refs
    else:
        x_ref, mod_ref, w_ref, b_ref, wgh_ref, wgl_ref, bg_ref, proj_ref, gates_ref, u_scr = refs
    j = pl.program_id(1)

    @pl.when(j == 0)
    def _():
        x = x_ref[...]
        if has_pos:
            x = x + pos_ref[...]
        mod = mod_ref[...]
        u = x * (1.0 + mod[:, D_MODEL:2 * D_MODEL]) + mod[:, 0:D_MODEL]
        uh, ul = _split2(u)
        u_scr[...] = uh
        wgh = wgh_ref[...]
        gates_ref[...] = _dot(uh, wgh) + _dot(ul, wgh) + _dot(uh, wgl_ref[...]) + bg_ref[...]

    acc = _dot(u_scr[...], w_ref[...]) + b_ref[...]
    is_sig = jnp.logical_or(j == 3, j >= 5)
    is_k = j == _COL_K

    @pl.when(is_sig)
    def _():
        proj_ref[...] = _sigmoid(acc).astype(BF16)

    @pl.when(is_k)
    def _():
        proj_ref[...] = (acc * (HEAD_DIM ** -0.5)).astype(BF16)

    @pl.when(jnp.logical_not(jnp.logical_or(is_sig, is_k)))
    def _():
        proj_ref[...] = acc.astype(BF16)


def _inproj(x, pos, mod, row_fn, w_main, b_main, wg_hi, wg_lo, b_gate, tm):
    t = x.shape[0]
    n_col = w_main.shape[1] // D_MODEL
    has_pos = pos is not None
    in_specs = [pl.BlockSpec((tm, D_MODEL), lambda i, j: (i, 0))]
    args = [x]
    if has_pos:
        pos_blocks = pos.shape[0] // tm
        in_specs.append(pl.BlockSpec((tm, D_MODEL), lambda i, j: (i % pos_blocks, 0)))
        args.append(pos)
    in_specs += [
        pl.BlockSpec((None, 1, mod.shape[2]), lambda i, j: (row_fn(i), 0, 0)),
        pl.BlockSpec((D_MODEL, D_MODEL), lambda i, j: (0, j)),
        pl.BlockSpec((1, D_MODEL), lambda i, j: (0, j)),
        pl.BlockSpec((D_MODEL, GATE_PAD), lambda i, j: (0, 0)),
        pl.BlockSpec((D_MODEL, GATE_PAD), lambda i, j: (0, 0)),
        pl.BlockSpec((1, GATE_PAD), lambda i, j: (0, 0)),
    ]
    args += [mod, w_main, b_main, wg_hi, wg_lo, b_gate]
    return pl.pallas_call(
        functools.partial(_inproj_kernel, has_pos),
        grid=(t // tm, n_col),
        in_specs=in_specs,
        out_specs=[pl.BlockSpec((tm, D_MODEL), lambda i, j: (i, j)),
                   pl.BlockSpec((tm, GATE_PAD), lambda i, j: (i, 0))],
        out_shape=[jax.ShapeDtypeStruct((t, n_col * D_MODEL), BF16),
                   jax.ShapeDtypeStruct((t, GATE_PAD), F32)],
        scratch_shapes=[pltpu.VMEM((tm, D_MODEL), BF16)],
        compiler_params=_params(2),
        name="inproj",
    )(*args)


def _mlstm_kernel(has_state, want_state, nc, *refs):
    refs = list(refs)
    qf, kf, vf, qb, kb, vb, gf, gb = refs[:8]
    pos = 8
    if has_state:
        c0_ref, n0_ref, m0_ref = refs[pos:pos + 3]
        pos += 3
    hf_ref, hb_ref = refs[pos:pos + 2]
    pos += 2
    if want_state:
        co_ref, no_ref, mo_ref = refs[pos:pos + 3]
        pos += 3
    c_scr, n_scr, m_scr = refs[pos:pos + 3]
    step = pl.program_id(1)

    @pl.when(step == 0)
    def _():
        if has_state:
            c_scr[...] = c0_ref[...]
            n_scr[...] = n0_ref[...]
            m_scr[...] = m0_ref[...]
        else:
            c_scr[...] = jnp.zeros_like(c_scr)
            n_scr[...] = jnp.zeros_like(n_scr)
            m_scr[...] = jnp.zeros_like(m_scr)

    row = lax.broadcasted_iota(jnp.int32, (CHUNK, CHUNK), 0)
    col = lax.broadcasted_iota(jnp.int32, (CHUNK, CHUNK), 1)

    for d, (q_ref, k_ref, v_ref, g_ref, h_ref) in enumerate(((qf, kf, vf, gf, hf_ref), (qb, kb, vb, gb, hb_ref))):
        valid = (col <= row) if d == 0 else (col >= row)
        tri = jnp.where(valid, 1.0, 0.0).astype(BF16)
        tri_t = jnp.where((row <= col) if d == 0 else (row >= col), 1.0, 0.0).astype(BF16)
        g = g_ref[...]
        lf = _log_sigmoid(g)
        g_t = g.T
        lf_t = lf.T
        l1, l2, l3 = _split3(lf)
        b_col_all = _dot(tri, l1) + _dot(tri, l2) + _dot(tri, l3)
        t1, t2, t3 = _split3(lf_t)
        b_row_all = _dot(t1, tri_t) + _dot(t2, tri_t) + _dot(t3, tri_t)
        last = CHUNK - 1 if d == 0 else 0
        for h in range(N_HEADS):
            u = d * N_HEADS + h
            ci = d * 2 * N_HEADS + h
            cf = ci + N_HEADS
            b_col = b_col_all[:, cf:cf + 1]
            b_row = b_row_all[cf:cf + 1, :]
            i_col = g[:, ci:ci + 1]
            i_row = g_t[ci:ci + 1, :]
            m = m_scr[u:u + 1, 0:1]
            hs = slice(h * HEAD_DIM, (h + 1) * HEAD_DIM)
            q = q_ref[:, hs]
            k = k_ref[:, hs]
            v = v_ref[:, hs]
            c_state = c_scr[u]
            n_state = n_scr[u:u + 1, :]

            dmat = jnp.where(valid, b_col - b_row + i_row, -jnp.inf)
            inter = b_col + m
            m_t = jnp.maximum(inter, jnp.max(dmat, axis=-1, keepdims=True))
            w_ts = jnp.exp(dmat - m_t)
            a_t = jnp.exp(inter - m_t)
            s = _dot_nt(q, k) * w_ts
            num = _dot(s.astype(BF16), v) + a_t * _dot_nt(q, c_state.astype(BF16))
            qf32 = q.astype(F32)
            den = jnp.sum(s, axis=-1, keepdims=True) + a_t * jnp.sum(qf32 * n_state, axis=-1, keepdims=True)
            h_ref[:, hs] = num / jnp.maximum(jnp.abs(den), jnp.exp(-m_t))

            b_last = b_row[:, last:last + 1]
            dec_col = b_last - b_col + i_col
            dec_row = b_last - b_row + i_row
            m_new = jnp.maximum(b_last + m, jnp.max(dec_row, axis=-1, keepdims=True))
            w_col = jnp.exp(dec_col - m_new)
            a_c = jnp.exp(b_last + m - m_new)
            vw_t = (v.astype(F32) * w_col).T.astype(BF16)
            c_scr[u] = a_c * c_state + _dot(vw_t, k)
            n_scr[u:u + 1, :] = a_c * n_state + jnp.sum(k.astype(F32) * w_col, axis=0, keepdims=True)
            m_scr[u:u + 1, :] = jnp.broadcast_to(m_new, (1, GATE_PAD))

    if want_state:
        @pl.when(step == nc - 1)
        def _():
            co_ref[...] = c_scr[...]
            no_ref[...] = n_scr[...]
            mo_ref[...] = m_scr[...]


def _mlstm(proj, gates, n_seq, seq_len, state0, want_state):
    t = proj.shape[0]
    nc = seq_len // CHUNK
    units = 2 * N_HEADS
    has_state = state0 is not None

    def fwd(c):
        return lambda s, k: (s * nc + k, c)

    def bwd(c):
        return lambda s, k: (s * nc + nc - 1 - k, c)

    tile = (CHUNK, D_MODEL)
    in_specs = [pl.BlockSpec(tile, fwd(0)), pl.BlockSpec(tile, fwd(1)), pl.BlockSpec(tile, fwd(2)),
                pl.BlockSpec(tile, bwd(0)), pl.BlockSpec(tile, bwd(1)), pl.BlockSpec(tile, bwd(2)),
                pl.BlockSpec((CHUNK, GATE_PAD), fwd(0)), pl.BlockSpec((CHUNK, GATE_PAD), bwd(0))]
    args = [proj] * 6 + [gates, gates]
    state_specs = [pl.BlockSpec((None, units, HEAD_DIM, HEAD_DIM), lambda s, k: (s, 0, 0, 0)),
                   pl.BlockSpec((None, units, HEAD_DIM), lambda s, k: (s, 0, 0)),
                   pl.BlockSpec((None, units, GATE_PAD), lambda s, k: (s, 0, 0))]
    state_shapes = [jax.ShapeDtypeStruct((n_seq, units, HEAD_DIM, HEAD_DIM), F32),
                    jax.ShapeDtypeStruct((n_seq, units, HEAD_DIM), F32),
                    jax.ShapeDtypeStruct((n_seq, units, GATE_PAD), F32)]
    if has_state:
        in_specs += state_specs
        args += list(state0)
    out_specs = [pl.BlockSpec(tile, fwd(0)), pl.BlockSpec(tile, bwd(0))]
    out_shape = [jax.ShapeDtypeStruct((t, D_MODEL), F32), jax.ShapeDtypeStruct((t, D_MODEL), F32)]
    if want_state:
        out_specs += state_specs
        out_shape += state_shapes
    return pl.pallas_call(
        functools.partial(_mlstm_kernel, has_state, want_state, nc),
        grid=(n_seq, nc),
        in_specs=in_specs,
        out_specs=out_specs,
        out_shape=out_shape,
        scratch_shapes=[pltpu.VMEM((units, HEAD_DIM, HEAD_DIM), F32),
                        pltpu.VMEM((units, HEAD_DIM), F32),
                        pltpu.VMEM((units, GATE_PAD), F32)],
        compiler_params=_params(2),
        name="mlstm",
    )(*args)


_CONV_ROWS = 64
_LANES = 128


def _tail_kernel(has_pos, tm, tiles_per_seq, alpha, *refs):
    refs = list(refs)
    hf_ref, hb_ref, so_ref, val_ref, sg_ref, sga_ref, sgb_ref, vp_ref, gp_ref, vn_ref, gn_ref, x_ref = refs[:12]
    pos = 12
    if has_pos:
        pos_ref = refs[pos]
        pos += 1
    (mod_ref, normw_ref, wa_ref, wc_ref, wo_ref, dww_ref, dwb_ref, clnw_ref, clnb_ref, ln1w_ref, ln1b_ref,
     x1_ref, u2_ref, xpad, conv_scr) = refs[pos:]
    i = pl.program_id(0)

    hsum = hf_ref[...] + hb_ref[...]
    parts = []
    for h in range(N_HEADS):
        hh = hsum[:, h * HEAD_DIM:(h + 1) * HEAD_DIM]
        mu = jnp.mean(hh, axis=-1, keepdims=True)
        hc = hh - mu
        var = jnp.mean(hc * hc, axis=-1, keepdims=True)
        parts.append(hc * lax.rsqrt(var + LN_EPS))
    hn = jnp.concatenate(parts, axis=-1) * normw_ref[...]
    hg = (so_ref[...].astype(F32) * hn).astype(BF16)
    branch_a = _dot(hg, wa_ref[...])

    first = (i % tiles_per_seq) == 0
    last = (i % tiles_per_seq) == tiles_per_seq - 1
    keep_prev = jnp.where(first, 0.0, 1.0)
    keep_next = jnp.where(last, 0.0, 1.0)
    xpad[0:CONV_HALO, :] = vp_ref[...].astype(F32) * gp_ref[...].astype(F32) * keep_prev
    xpad[CONV_HALO:CONV_HALO + tm, :] = val_ref[...].astype(F32) * sg_ref[...].astype(F32)
    xpad[CONV_HALO + tm:2 * CONV_HALO + tm, :] = vn_ref[...].astype(F32) * gn_ref[...].astype(F32) * keep_next
    tap0 = CONV_HALO - CONV_WIDTH // 2

    def col_body(c, carry):
        cs = pl.ds(pl.multiple_of(c * _LANES, _LANES), _LANES)
        for r in range(tm // _CONV_ROWS):
            acc = jnp.broadcast_to(dwb_ref[:, cs], (_CONV_ROWS, _LANES))
            for k in range(CONV_WIDTH):
                start = tap0 + k + r * _CONV_ROWS
                acc = acc + xpad[pl.ds(start, _CONV_ROWS), cs] * dww_ref[k:k + 1, cs]
            conv_scr[pl.ds(r * _CONV_ROWS, _CONV_ROWS), cs] = acc
        return carry

    lax.fori_loop(0, D_MODEL // _LANES, col_body, 0)
    xc = _layer_norm(conv_scr[...], clnw_ref[...], clnb_ref[...])
    xc = (xc * _sigmoid(xc)).astype(BF16)
    branch_b = _dot(xc, wc_ref[...])

    merged = sga_ref[...].astype(F32) * branch_a + sgb_ref[...].astype(F32) * branch_b
    mix = _dot(merged.astype(BF16), wo_ref[...])

    mod = mod_ref[...]
    gate1 = mod[:, 2 * D_MODEL:3 * D_MODEL]
    shift2 = mod[:, 3 * D_MODEL:4 * D_MODEL]
    scale2 = mod[:, 4 * D_MODEL:5 * D_MODEL]
    x = x_ref[...]
    if has_pos:
        x = x + pos_ref[...]
    x1 = _layer_norm(alpha * x + gate1 * mix, ln1w_ref[...], ln1b_ref[...])
    x1_ref[...] = x1
    u2_ref[...] = (x1 * (1.0 + scale2) + shift2).astype(BF16)


def _tail(hf, hb, proj, x, pos, mod, row_fn, seq_len, tm, alpha, lw):
    t = x.shape[0]
    has_pos = pos is not None
    tiles_per_seq = seq_len // tm
    hb_per_tile = tm // CONV_HALO
    n_halo = t // CONV_HALO
    big = (tm, D_MODEL)
    halo = (CONV_HALO, D_MODEL)

    def colspec(c):
        return pl.BlockSpec(big, lambda i: (i, c))

    def prev(c):
        return pl.BlockSpec(halo, lambda i: (jnp.maximum(i * hb_per_tile - 1, 0), c))

    def nxt(c):
        return pl.BlockSpec(halo, lambda i: (jnp.minimum((i + 1) * hb_per_tile, n_halo - 1), c))

    def const(shape):
        return pl.BlockSpec(shape, lambda i: (0,) * len(shape))

    in_specs = [colspec(0), colspec(0), colspec(3), colspec(4), colspec(5), colspec(6), colspec(7),
                prev(4), prev(5), nxt(4), nxt(5), colspec(0)]
    args = [hf, hb, proj, proj, proj, proj, proj, proj, proj, proj, proj, x]
    if has_pos:
        in_specs.append(pl.BlockSpec(big, lambda i: (i % tiles_per_seq, 0)))
        args.append(pos)
    in_specs += [pl.BlockSpec((None, 1, mod.shape[2]), lambda i: (row_fn(i), 0, 0)),
                 const((1, D_MODEL)), const((D_MODEL, D_MODEL)), const((D_MODEL, D_MODEL)), const((D_MODEL, D_MODEL)),
                 const((32, D_MODEL)), const((1, D_MODEL)), const((1, D_MODEL)), const((1, D_MODEL)),
                 const((1, D_MODEL)), const((1, D_MODEL))]
    args += [mod, lw["norm_w"], lw["w_a"], lw["w_cout"], lw["w_out"], lw["dw_w"], lw["dw_b"], lw["cln_w"],
             lw["cln_b"], lw["ln1_w"], lw["ln1_b"]]
    return pl.pallas_call(
        functools.partial(_tail_kernel, has_pos, tm, tiles_per_seq, alpha),
        grid=(t // tm,),
        in_specs=in_specs,
        out_specs=[pl.BlockSpec(big, lambda i: (i, 0)), pl.BlockSpec(big, lambda i: (i, 0))],
        out_shape=[jax.ShapeDtypeStruct((t, D_MODEL), F32), jax.ShapeDtypeStruct((t, D_MODEL), BF16)],
        scratch_shapes=[pltpu.VMEM((tm + 2 * CONV_HALO, D_MODEL), F32), pltpu.VMEM((tm, D_MODEL), F32)],
        compiler_params=_params(1),
        name="mixer_tail",
    )(*args)


_CAND_ROWS_Q = (16, 8, 8, 8, 8, 8, 8, 8)
_CAND_VALID_Q = tuple(PEER_TOPK // (q + 1) for q in range(8))


def _extract_top(s, n_rounds, vals_scr=None):
    rank = jnp.full(s.shape, float(n_rounds), F32)
    cur = s
    first = None
    zsum = None
    mx = None
    for r in range(n_rounds):
        mx = jnp.max(cur, axis=0, keepdims=True)
        hit = cur == mx
        rank = jnp.where(hit, float(r), rank)
        cur = jnp.where(hit, -jnp.inf, cur)
        if vals_scr is not None:
            vals_scr[r:r + 1, :] = mx
        if r == 0:
            first = mx
            zsum = jnp.ones_like(mx)
        else:
            zsum = zsum + jnp.exp(mx - first)
    return mx, rank, zsum


def _route_kernel(u_ref, wq_ref, sk_ref, rank2_ref, e2_ref, r_ref, a1_ref, a_scr, b_scr):
    u = u_ref[...]
    q1 = _dot_nt(wq_ref[0:PEER_NKEYS, :], u).astype(BF16)
    q2 = _dot_nt(wq_ref[PEER_NKEYS:2 * PEER_NKEYS, :], u).astype(BF16)
    s1 = _dot(sk_ref[0], q1)
    s2 = _dot(sk_ref[1], q2)
    _, rank1, _ = _extract_top(s1, PEER_TOPK, a_scr)
    _, rank2, _ = _extract_top(s2, PEER_TOPK, b_scr)
    a = a_scr[...]
    b = b_scr[...]
    prow = lax.broadcasted_iota(jnp.int32, (PEER_TOPK, 1), 0)

    pieces = []
    for q in range(8):
        rows = _CAND_ROWS_Q[q]
        cq = a[0:rows, :] + b[q:q + 1, :]
        pieces.append(jnp.where(prow[0:rows, :] < _CAND_VALID_Q[q], cq, -jnp.inf))
    pieces.append(a[0:1, :] + b[8:16, :])
    cand = jnp.concatenate(pieces, axis=0)
    tau, _, zsum = _extract_top(cand, PEER_TOPK)

    sel = jnp.where(cand >= tau, 1.0, 0.0)
    cnt = sel[0:16, :]
    off = 16
    for q in range(1, 8):
        cnt = cnt + jnp.concatenate([sel[off:off + 8, :], jnp.zeros_like(sel[0:8, :])], axis=0)
        off += 8
    tail_cnt = jnp.sum(sel[off:off + 8, :], axis=0, keepdims=True)
    cnt = cnt + jnp.where(prow == 0, tail_cnt, 0.0)

    r = jnp.zeros_like(s1)
    for p in range(PEER_TOPK):
        r = jnp.where(rank1 == float(p), cnt[p:p + 1, :], r)
    rank2_ref[...] = rank2.astype(BF16)
    e2_ref[...] = jnp.exp(s2 - b[0:1, :]).astype(BF16)
    r_ref[...] = r
    a1_ref[...] = jnp.exp(s1 - a[0:1, :]) / zsum


def _route(u2, wq_t, subkeys, tm):
    t = u2.shape[0]
    out_blk = pl.BlockSpec((None, PEER_NKEYS, tm), lambda i, h: (h, 0, i))
    shape = (PEER_HEADS, PEER_NKEYS, t)
    return pl.pallas_call(
        _route_kernel,
        grid=(t // tm, PEER_HEADS),
        in_specs=[pl.BlockSpec((tm, D_MODEL), lambda i, h: (i, 0)),
                  pl.BlockSpec((2 * PEER_NKEYS, D_MODEL), lambda i, h: (h, 0)),
                  pl.BlockSpec((None, 2, PEER_NKEYS, PEER_NKEYS), lambda i, h: (h, 0, 0, 0))],
        out_specs=[out_blk, out_blk, out_blk, out_blk],
        out_shape=[jax.ShapeDtypeStruct(shape, BF16), jax.ShapeDtypeStruct(shape, BF16),
                   jax.ShapeDtypeStruct(shape, F32), jax.ShapeDtypeStruct(shape, F32)],
        scratch_shapes=[pltpu.VMEM((PEER_TOPK, tm), F32), pltpu.VMEM((PEER_TOPK, tm), F32)],
        compiler_params=_params(2),
        name="peer_route",
    )(u2, wq_t, subkeys)


_GELU_C = math.sqrt(2.0 / math.pi)


def _gelu_tanh(x):
    return 0.5 * x * (1.0 + jnp.tanh(_GELU_C * (x + 0.044715 * (x * x * x))))


def _experts_kernel(ib, n_eblk, alpha, u_ref, ut_ref, vt_ref, rank2_ref, e2_ref, r_ref, a1_ref, x1_ref, mod_ref,
                    lnw_ref, lnb_ref, y_ref, acc_scr, p_scr):
    e = pl.program_id(1)

    @pl.when(e == 0)
    def _():
        acc_scr[...] = jnp.zeros_like(acc_scr)

    act = _gelu_tanh(_dot_nt(ut_ref[...], u_ref[...]))
    for ii in range(ib):
        w = None
        for h in range(PEER_HEADS):
            r_row = r_ref[h, ii:ii + 1, :].astype(BF16)
            a_row = a1_ref[h, ii:ii + 1, :].astype(BF16)
            term = jnp.where(rank2_ref[h] < r_row, e2_ref[h] * a_row, jnp.zeros((), BF16))
            w = term if w is None else w + term
        rows = slice(ii * PEER_NKEYS, (ii + 1) * PEER_NKEYS)
        p_scr[rows, :] = w * act[rows, :].astype(BF16)
    acc_scr[...] += _dot(vt_ref[...], p_scr[...])

    @pl.when(e == n_eblk - 1)
    def _():
        mod = mod_ref[...]
        gate2 = mod[:, 5 * D_MODEL:6 * D_MODEL]
        y = alpha * x1_ref[...] + gate2 * acc_scr[...].T
        y_ref[...] = _layer_norm(y, lnw_ref[...], lnb_ref[...])


def _experts(u2, u_tab, vt_tab, rank2, e2, r, a1, x1, mod, row_fn, lnw, lnb, tm, eb, alpha):
    t = u2.shape[0]
    n_exp = u_tab.shape[0]
    ib = eb // PEER_NKEYS
    n_eblk = n_exp // eb
    r4 = r.reshape(PEER_HEADS, n_eblk, ib, t)
    a4 = a1.reshape(PEER_HEADS, n_eblk, ib, t)
    full = pl.BlockSpec((PEER_HEADS, PEER_NKEYS, tm), lambda i, e: (0, 0, i))
    rows = pl.BlockSpec((PEER_HEADS, None, ib, tm), lambda i, e: (0, e, 0, i))
    tok = pl.BlockSpec((tm, D_MODEL), lambda i, e: (i, 0))
    vec = pl.BlockSpec((1, D_MODEL), lambda i, e: (0, 0))
    return pl.pallas_call(
        functools.partial(_experts_kernel, ib, n_eblk, alpha),
        grid=(t // tm, n_eblk),
        in_specs=[tok,
                  pl.BlockSpec((eb, D_MODEL), lambda i, e: (e, 0)),
                  pl.BlockSpec((D_MODEL, eb), lambda i, e: (0, e)),
                  full, full, rows, rows, tok,
                  pl.BlockSpec((None, 1, mod.shape[2]), lambda i, e: (row_fn(i), 0, 0)),
                  vec, vec],
        out_specs=tok,
        out_shape=jax.ShapeDtypeStruct((t, D_MODEL), F32),
        scratch_shapes=[pltpu.VMEM((D_MODEL, tm), F32), pltpu.VMEM((eb, tm), BF16)],
        compiler_params=_params(2),
        name="peer_experts",
    )(u2, u_tab, vt_tab, rank2, e2, r4, a4, x1, mod, lnw, lnb)


def _grid_pos_embed(n_tokens):
    rows = n_tokens // GRID_W
    quarter = D_MODEL // 4
    freqs = jnp.exp(-math.log(POS_BASE) * jnp.arange(quarter, dtype=F32) / quarter)
    r = jnp.arange(rows, dtype=F32)[:, None] * freqs
    cl = jnp.arange(GRID_W, dtype=F32)[:, None] * freqs
    er = jnp.concatenate([jnp.sin(r), jnp.cos(r)], -1)
    ec = jnp.concatenate([jnp.sin(cl), jnp.cos(cl)], -1)
    emb = jnp.concatenate([jnp.broadcast_to(er[:, None, :], (rows, GRID_W, D_MODEL // 2)),
                           jnp.broadcast_to(ec[None, :, :], (rows, GRID_W, D_MODEL // 2))], -1)
    return emb.reshape(rows * GRID_W, D_MODEL)


def _pick_tile(seq_len, n_tokens, target):
    tm = min(target, seq_len)
    while seq_len % tm or n_tokens % tm:
        tm //= 2
    return tm


def _block(x, pos, mod, n_seq, seq_len, tm_in, row_of_token_tile, state0, want_state, lw, alpha):
    t = x.shape[0]
    proj, gates = _inproj(x, pos, mod, row_of_token_tile(tm_in), lw["w_main"], lw["b_main"], lw["wg_hi"],
                          lw["wg_lo"], lw["b_gate"], tm_in)
    res = _mlstm(proj, gates, n_seq, seq_len, state0, want_state)
    hf, hb = res[0], res[1]
    tm_tail = _pick_tile(seq_len, t, 512)
    x1, u2 = _tail(hf, hb, proj, x, pos, mod, row_of_token_tile(tm_tail), seq_len, tm_tail, alpha, lw)
    tm_p = _pick_tile(seq_len, t, 512)
    rank2, e2, r, a1 = _route(u2, lw["wq_t"], lw["subkeys"], tm_p)
    y = _experts(u2, lw["peer_u"], lw["peer_vt"], rank2, e2, r, a1, x1, mod, row_of_token_tile(tm_p),
                 lw["ln2_w"], lw["ln2_b"], tm_p, 512, alpha)
    return y, res[2:]


def kernel(x_prompt, x_sample, state_C, state_n, state_m, c, c_ctx, w_in, b_in, mlstm_norm_w, w_a, conv_dw_w,
           conv_dw_b, conv_ln_w, conv_ln_b, w_conv_out, w_out, w_mod, b_mod, ln1_w, ln1_b, ln2_w, ln2_b,
           peer_w_query, peer_subkeys, peer_u, peer_v):
    depth = w_in.shape[0]
    alpha = (2.0 * depth) ** 0.25
    bsz, seq, _ = x_prompt.shape
    dbsz, dseq, _ = x_sample.shape
    units = 2 * N_HEADS
    gate_off = 4 * D_MODEL

    ctx = x_prompt.reshape(bsz * seq, D_MODEL)
    lat = x_sample.reshape(dbsz * dseq, D_MODEL)
    pos = _grid_pos_embed(dseq)
    n_rows = 1 + dbsz
    pad_rows = (-n_rows) % 8
    cvec = jnp.concatenate([c_ctx[None, :], c, jnp.zeros((pad_rows, D_MODEL), F32)], axis=0)

    new_c, new_n, new_m = [], [], []
    for l in range(depth):
        vec = lambda a: a[l].reshape(1, -1)
        w_l = w_in[l]
        b_l = b_in[l]
        wg = jnp.pad(w_l[:, gate_off:gate_off + N_GATES], ((0, 0), (0, GATE_PAD - N_GATES)))
        wg_hi = wg.astype(BF16)
        lw = {
            "w_main": jnp.concatenate([w_l[:, :gate_off], w_l[:, gate_off + N_GATES:]], axis=1).astype(BF16),
            "b_main": jnp.concatenate([b_l[:gate_off], b_l[gate_off + N_GATES:]]).reshape(1, -1),
            "wg_hi": wg_hi,
            "wg_lo": (wg - wg_hi.astype(F32)).astype(BF16),
            "b_gate": jnp.pad(b_l[gate_off:gate_off + N_GATES], (0, GATE_PAD - N_GATES)).reshape(1, -1),
            "norm_w": vec(mlstm_norm_w), "w_a": w_a[l].astype(BF16), "w_cout": w_conv_out[l].astype(BF16),
            "w_out": w_out[l].astype(BF16),
            "dw_w": jnp.pad(conv_dw_w[l], ((0, 32 - CONV_WIDTH), (0, 0))), "dw_b": vec(conv_dw_b),
            "cln_w": vec(conv_ln_w), "cln_b": vec(conv_ln_b), "ln1_w": vec(ln1_w), "ln1_b": vec(ln1_b),
            "ln2_w": vec(ln2_w), "ln2_b": vec(ln2_b),
            "wq_t": peer_w_query[l].T.astype(BF16),
            "subkeys": peer_subkeys[l].astype(BF16),
            "peer_u": peer_u[l].astype(BF16),
            "peer_vt": peer_v[l].T.astype(BF16),
        }
        mod = _modulation(cvec, w_mod[l], b_mod[l]).reshape(n_rows + pad_rows, 1, 6 * D_MODEL)

        ctx, (c_fin, n_fin, m_fin) = _block(ctx, None, mod, bsz, seq, _pick_tile(bsz * seq, bsz * seq, 1024),
                                             lambda tm: (lambda i: 0), None, True, lw, alpha)
        new_c.append(c_fin.reshape(bsz, 2, N_HEADS, HEAD_DIM, HEAD_DIM))
        new_n.append(n_fin.reshape(bsz, 2, N_HEADS, HEAD_DIM))
        new_m.append(m_fin[:, :, 0].reshape(bsz, 2, N_HEADS))

        state0 = (state_C[:, l].reshape(dbsz, units, HEAD_DIM, HEAD_DIM),
                  state_n[:, l].reshape(dbsz, units, HEAD_DIM),
                  jnp.broadcast_to(state_m[:, l].reshape(dbsz, units, 1), (dbsz, units, GATE_PAD)))
        lat, _ = _block(lat, pos, mod, dbsz, dseq, _pick_tile(dseq, dbsz * dseq, 1024),
                        lambda tm: (lambda i: 1 + (i * tm) // dseq), state0, False, lw, alpha)

    return (ctx.reshape(bsz, seq, D_MODEL), lat.reshape(dbsz, dseq, D_MODEL),
            jnp.stack(new_c, axis=1), jnp.stack(new_n, axis=1), jnp.stack(new_m, axis=1))
```

```python
import functools
import math

import jax
import jax.numpy as jnp
from jax import lax
from jax.experimental import pallas as pl
from jax.experimental.pallas import tpu as pltpu

F32 = jnp.float32
BF16 = jnp.bfloat16

D_MODEL = 1024
N_HEADS = 4
HEAD_DIM = 256
CHUNK = 128
CONV_WIDTH = 31
CONV_HALO = 16
N_GATES = 16
GATE_PAD = 128
GRID_W = 64
POS_BASE = 10000.0
LN_EPS = 1e-6
PEER_HEADS = 8
PEER_NKEYS = 128
PEER_TOPK = 16
V7X_VMEM_LIMIT = 56 * 1024 * 1024


def _params(n_axes):
    return pltpu.CompilerParams(dimension_semantics=("arbitrary",) * n_axes,
                                vmem_limit_bytes=V7X_VMEM_LIMIT)


def _sigmoid(x):
    return 1.0 / (1.0 + jnp.exp(-x))


def _log_sigmoid(x):
    return jnp.minimum(x, 0.0) - jnp.log(1.0 + jnp.exp(-jnp.abs(x)))


def _dot(a, b):
    return jnp.dot(a, b, preferred_element_type=F32)


def _dot_nt(a, b):
    return lax.dot_general(a, b, (((1,), (1,)), ((), ())), preferred_element_type=F32)


def _split2(x):
    hi = x.astype(BF16)
    lo = (x - hi.astype(F32)).astype(BF16)
    return hi, lo


def _split3(x):
    a = x.astype(BF16)
    r = x - a.astype(F32)
    b = r.astype(BF16)
    c = (r - b.astype(F32)).astype(BF16)
    return a, b, c


def _layer_norm(x, w, b):
    mu = jnp.mean(x, axis=-1, keepdims=True)
    xc = x - mu
    var = jnp.mean(xc * xc, axis=-1, keepdims=True)
    return xc * lax.rsqrt(var + LN_EPS) * w + b


def _mod_kernel(c_ref, w_ref, b_ref, o_ref):
    c = c_ref[...]
    s = c * _sigmoid(c)
    sh, sl = _split2(s)
    wh, wl = _split2(w_ref[...])
    o_ref[...] = _dot(sh, wh) + _dot(sl, wh) + _dot(sh, wl) + b_ref[...]


def _modulation(cvec, w_mod, b_mod):
    rows = cvec.shape[0]
    tn = 1536
    n = w_mod.shape[1]
    return pl.pallas_call(
        _mod_kernel,
        grid=(n // tn,),
        in_specs=[pl.BlockSpec((rows, D_MODEL), lambda j: (0, 0)),
                  pl.BlockSpec((D_MODEL, tn), lambda j: (0, j)),
                  pl.BlockSpec((1, tn), lambda j: (0, j))],
        out_specs=pl.BlockSpec((rows, tn), lambda j: (0, j)),
        out_shape=jax.ShapeDtypeStruct((rows, n), F32),
        compiler_params=_params(1),
        name="modulation",
    )(cvec, w_mod, b_mod.reshape(1, n))


_COL_K = 1


def _inproj_kernel(has_pos, *refs):
    if has_pos:
        x_ref, pos_ref, mod_ref, w_ref, b_ref, wgh_ref, wgl_ref, bg_ref, proj_ref, gates_ref, u_scr = refs
    else:
        x_ref, mod_ref, w_ref, b_ref, wgh_ref, wgl_ref, bg_ref, proj_ref, gates_ref, u_scr = refs
    j = pl.program_id(1)

    @pl.when(j == 0)
    def _():
        x = x_ref[...]
        if has_pos:
            x = x + pos_ref[...]
        mod = mod_ref[...]
        u = x * (1.0 + mod[:, D_MODEL:2 * D_MODEL]) + mod[:, 0:D_MODEL]
        uh, ul = _split2(u)
        u_scr[...] = uh
        wgh = wgh_ref[...]
        gates_ref[...] = _dot(uh, wgh) + _dot(ul, wgh) + _dot(uh, wgl_ref[...]) + bg_ref[...]

    acc = _dot(u_scr[...], w_ref[...]) + b_ref[...]
    is_sig = jnp.logical_or(j == 3, j >= 5)
    is_k = j == _COL_K

    @pl.when(is_sig)
    def _():
        proj_ref[...] = _sigmoid(acc).astype(BF16)

    @pl.when(is_k)
    def _():
        proj_ref[...] = (acc * (HEAD_DIM ** -0.5)).astype(BF16)

    @pl.when(jnp.logical_not(jnp.logical_or(is_sig, is_k)))
    def _():
        proj_ref[...] = acc.astype(BF16)


def _inproj(x, pos, mod, row_fn, w_main, b_main, wg_hi, wg_lo, b_gate, tm):
    t = x.shape[0]
    n_col = w_main.shape[1] // D_MODEL
    has_pos = pos is not None
    in_specs = [pl.BlockSpec((tm, D_MODEL), lambda i, j: (i, 0))]
    args = [x]
    if has_pos:
        pos_blocks = pos.shape[0] // tm
        in_specs.append(pl.BlockSpec((tm, D_MODEL), lambda i, j: (i % pos_blocks, 0)))
        args.append(pos)
    in_specs += [
        pl.BlockSpec((None, 1, mod.shape[2]), lambda i, j: (row_fn(i), 0, 0)),
        pl.BlockSpec((D_MODEL, D_MODEL), lambda i, j: (0, j)),
        pl.BlockSpec((1, D_MODEL), lambda i, j: (0, j)),
        pl.BlockSpec((D_MODEL, GATE_PAD), lambda i, j: (0, 0)),
        pl.BlockSpec((D_MODEL, GATE_PAD), lambda i, j: (0, 0)),
        pl.BlockSpec((1, GATE_PAD), lambda i, j: (0, 0)),
    ]
    args += [mod, w_main, b_main, wg_hi, wg_lo, b_gate]
    return pl.pallas_call(
        functools.partial(_inproj_kernel, has_pos),
        grid=(t // tm, n_col),
        in_specs=in_specs,
        out_specs=[pl.BlockSpec((tm, D_MODEL), lambda i, j: (i, j)),
                   pl.BlockSpec((tm, GATE_PAD), lambda i, j: (i, 0))],
        out_shape=[jax.ShapeDtypeStruct((t, n_col * D_MODEL), BF16),
                   jax.ShapeDtypeStruct((t, GATE_PAD), F32)],
        scratch_shapes=[pltpu.VMEM((tm, D_MODEL), BF16)],
        compiler_params=_params(2),
        name="inproj",
    )(*args)


def _mlstm_kernel(has_state, want_state, nc, *refs):
    refs = list(refs)
    qf, kf, vf, qb, kb, vb, gf, gb = refs[:8]
    pos = 8
    if has_state:
        c0_ref, n0_ref, m0_ref = refs[pos:pos + 3]
        pos += 3
    hf_ref, hb_ref = refs[pos:pos + 2]
    pos += 2
    if want_state:
        co_ref, no_ref, mo_ref = refs[pos:pos + 3]
        pos += 3
    c_scr, n_scr, m_scr = refs[pos:pos + 3]
    step = pl.program_id(1)

    @pl.when(step == 0)
    def _():
        if has_state:
            c_scr[...] = c0_ref[...]
            n_scr[...] = n0_ref[...]
            m_scr[...] = m0_ref[...]
        else:
            c_scr[...] = jnp.zeros_like(c_scr)
            n_scr[...] = jnp.zeros_like(n_scr)
            m_scr[...] = jnp.zeros_like(m_scr)

    row = lax.broadcasted_iota(jnp.int32, (CHUNK, CHUNK), 0)
    col = lax.broadcasted_iota(jnp.int32, (CHUNK, CHUNK), 1)

    for d, (q_ref, k_ref, v_ref, g_ref, h_ref) in enumerate(((qf, kf, vf, gf, hf_ref), (qb, kb, vb, gb, hb_ref))):
        valid = (col <= row) if d == 0 else (col >= row)
        tri = jnp.where(valid, 1.0, 0.0).astype(BF16)
        tri_t = jnp.where((row <= col) if d == 0 else (row >= col), 1.0, 0.0).astype(BF16)
        g = g_ref[...]
        lf = _log_sigmoid(g)
        g_t = g.T
        lf_t = lf.T
        l1, l2, l3 = _split3(lf)
        b_col_all = _dot(tri, l1) + _dot(tri, l2) + _dot(tri, l3)
        t1, t2, t3 = _split3(lf_t)
        b_row_all = _dot(t1, tri_t) + _dot(t2, tri_t) + _dot(t3, tri_t)
        last = CHUNK - 1 if d == 0 else 0
        for h in range(N_HEADS):
            u = d * N_HEADS + h
            ci = d * 2 * N_HEADS + h
            cf = ci + N_HEADS
            b_col = b_col_all[:, cf:cf + 1]
            b_row = b_row_all[cf:cf + 1, :]
            i_col = g[:, ci:ci + 1]
            i_row = g_t[ci:ci + 1, :]
            m = m_scr[u:u + 1, 0:1]
            hs = slice(h * HEAD_DIM, (h + 1) * HEAD_DIM)
            q = q_ref[:, hs]
            k = k_ref[:, hs]
            v = v_ref[:, hs]
            c_state = c_scr[u]
            n_state = n_scr[u:u + 1, :]

            dmat = jnp.where(valid, b_col - b_row + i_row, -jnp.inf)
            inter = b_col + m
            m_t = jnp.maximum(inter, jnp.max(dmat, axis=-1, keepdims=True))
            w_ts = jnp.exp(dmat - m_t)
            a_t = jnp.exp(inter - m_t)
            s = _dot_nt(q, k) * w_ts
            num = _dot(s.astype(BF16), v) + a_t * _dot_nt(q, c_state.astype(BF16))
            qf32 = q.astype(F32)
            den = jnp.sum(s, axis=-1, keepdims=True) + a_t * jnp.sum(qf32 * n_state, axis=-1, keepdims=True)
            h_ref[:, hs] = num / jnp.maximum(jnp.abs(den), jnp.exp(-m_t))

            b_last = b_row[:, last:last + 1]
            dec_col = b_last - b_col + i_col
            dec_row = b_last - b_row + i_row
            m_new = jnp.maximum(b_last + m, jnp.max(dec_row, axis=-1, keepdims=True))
            w_col = jnp.exp(dec_col - m_new)
            a_c = jnp.exp(b_last + m - m_new)
            vw_t = (v.astype(F32) * w_col).T.astype(BF16)
            c_scr[u] = a_c * c_state + _dot(vw_t, k)
            n_scr[u:u + 1, :] = a_c * n_state + jnp.sum(k.astype(F32) * w_col, axis=0, keepdims=True)
            m_scr[u:u + 1, :] = jnp.broadcast_to(m_new, (1, GATE_PAD))

    if want_state:
        @pl.when(step == nc - 1)
        def _():
            co_ref[...] = c_scr[...]
            no_ref[...] = n_scr[...]
            mo_ref[...] = m_scr[...]


def _mlstm(proj, gates, n_seq, seq_len, state0, want_state):
    t = proj.shape[0]
    nc = seq_len // CHUNK
    units = 2 * N_HEADS
    has_state = state0 is not None

    def fwd(c):
        return lambda s, k: (s * nc + k, c)

    def bwd(c):
        return lambda s, k: (s * nc + nc - 1 - k, c)

    tile = (CHUNK, D_MODEL)
    in_specs = [pl.BlockSpec(tile, fwd(0)), pl.BlockSpec(tile, fwd(1)), pl.BlockSpec(tile, fwd(2)),
                pl.BlockSpec(tile, bwd(0)), pl.BlockSpec(tile, bwd(1)), pl.BlockSpec(tile, bwd(2)),
                pl.BlockSpec((CHUNK, GATE_PAD), fwd(0)), pl.BlockSpec((CHUNK, GATE_PAD), bwd(0))]
    args = [proj] * 6 + [gates, gates]
    state_specs = [pl.BlockSpec((None, units, HEAD_DIM, HEAD_DIM), lambda s, k: (s, 0, 0, 0)),
                   pl.BlockSpec((None, units, HEAD_DIM), lambda s, k: (s, 0, 0)),
                   pl.BlockSpec((None, units, GATE_PAD), lambda s, k: (s, 0, 0))]
    state_shapes = [jax.ShapeDtypeStruct((n_seq, units, HEAD_DIM, HEAD_DIM), F32),
                    jax.ShapeDtypeStruct((n_seq, units, HEAD_DIM), F32),
                    jax.ShapeDtypeStruct((n_seq, units, GATE_PAD), F32)]
    if has_state:
        in_specs += state_specs
        args += list(state0)
    out_specs = [pl.BlockSpec(tile, fwd(0)), pl.BlockSpec(tile, bwd(0))]
    out_shape = [jax.ShapeDtypeStruct((t, D_MODEL), F32), jax.ShapeDtypeStruct((t, D_MODEL), F32)]
    if want_state:
        out_specs += state_specs
        out_shape += state_shapes
    return pl.pallas_call(
        functools.partial(_mlstm_kernel, has_state, want_state, nc),
        grid=(n_seq, nc),
        in_specs=in_specs,
        out_specs=out_specs,
        out_shape=out_shape,
        scratch_shapes=[pltpu.VMEM((units, HEAD_DIM, HEAD_DIM), F32),
                        pltpu.VMEM((units, HEAD_DIM), F32),
                        pltpu.VMEM((units, GATE_PAD), F32)],
        compiler_params=_params(2),
        name="mlstm",
    )(*args)


_CONV_ROWS = 64
_LANES = 128


def _tail_kernel(has_pos, tm, tiles_per_seq, alpha, *refs):
    refs = list(refs)
    hf_ref, hb_ref, so_ref, val_ref, sg_ref, sga_ref, sgb_ref, vp_ref, gp_ref, vn_ref, gn_ref, x_ref = refs[:12]
    pos = 12
    if has_pos:
        pos_ref = refs[pos]
        pos += 1
    (mod_ref, normw_ref, wa_ref, wc_ref, wo_ref, dww_ref, dwb_ref, clnw_ref, clnb_ref, ln1w_ref, ln1b_ref,
     x1_ref, u2_ref, xpad, conv_scr) = refs[pos:]
    i = pl.program_id(0)

    hsum = hf_ref[...] + hb_ref[...]
    parts = []
    for h in range(N_HEADS):
        hh = hsum[:, h * HEAD_DIM:(h + 1) * HEAD_DIM]
        mu = jnp.mean(hh, axis=-1, keepdims=True)
        hc = hh - mu
        var = jnp.mean(hc * hc, axis=-1, keepdims=True)
        parts.append(hc * lax.rsqrt(var + LN_EPS))
    hn = jnp.concatenate(parts, axis=-1) * normw_ref[...]
    hg = (so_ref[...].astype(F32) * hn).astype(BF16)
    branch_a = _dot(hg, wa_ref[...])

    first = (i % tiles_per_seq) == 0
    last = (i % tiles_per_seq) == tiles_per_seq - 1
    keep_prev = jnp.where(first, 0.0, 1.0)
    keep_next = jnp.where(last, 0.0, 1.0)
    xpad[0:CONV_HALO, :] = vp_ref[...].astype(F32) * gp_ref[...].astype(F32) * keep_prev
    xpad[CONV_HALO:CONV_HALO + tm, :] = val_ref[...].astype(F32) * sg_ref[...].astype(F32)
    xpad[CONV_HALO + tm:2 * CONV_HALO + tm, :] = vn_ref[...].astype(F32) * gn_ref[...].astype(F32) * keep_next
    tap0 = CONV_HALO - CONV_WIDTH // 2

    def col_body(c, carry):
        cs = pl.ds(pl.multiple_of(c * _LANES, _LANES), _LANES)
        for r in range(tm // _CONV_ROWS):
            acc = jnp.broadcast_to(dwb_ref[:, cs], (_CONV_ROWS, _LANES))
            for k in range(CONV_WIDTH):
                start = tap0 + k + r * _CONV_ROWS
                acc = acc + xpad[pl.ds(start, _CONV_ROWS), cs] * dww_ref[k:k + 1, cs]
            conv_scr[pl.ds(r * _CONV_ROWS, _CONV_ROWS), cs] = acc
        return carry

    lax.fori_loop(0, D_MODEL // _LANES, col_body, 0)
    xc = _layer_norm(conv_scr[...], clnw_ref[...], clnb_ref[...])
    xc = (xc * _sigmoid(xc)).astype(BF16)
    branch_b = _dot(xc, wc_ref[...])

    merged = sga_ref[...].astype(F32) * branch_a + sgb_ref[...].astype(F32) * branch_b
    mix = _dot(merged.astype(BF16), wo_ref[...])

    mod = mod_ref[...]
    gate1 = mod[:, 2 * D_MODEL:3 * D_MODEL]
    shift2 = mod[:, 3 * D_MODEL:4 * D_MODEL]
    scale2 = mod[:, 4 * D_MODEL:5 * D_MODEL]
    x = x_ref[...]
    if has_pos:
        x = x + pos_ref[...]
    x1 = _layer_norm(alpha * x + gate1 * mix, ln1w_ref[...], ln1b_ref[...])
    x1_ref[...] = x1
    u2_ref[...] = (x1 * (1.0 + scale2) + shift2).astype(BF16)


def _tail(hf, hb, proj, x, pos, mod, row_fn, seq_len, tm, alpha, lw):
    t = x.shape[0]
    has_pos = pos is not None
    tiles_per_seq = seq_len // tm
    hb_per_tile = tm // CONV_HALO
    n_halo = t // CONV_HALO
    big = (tm, D_MODEL)
    halo = (CONV_HALO, D_MODEL)

    def colspec(c):
        return pl.BlockSpec(big, lambda i: (i, c))

    def prev(c):
        return pl.BlockSpec(halo, lambda i: (jnp.maximum(i * hb_per_tile - 1, 0), c))

    def nxt(c):
        return pl.BlockSpec(halo, lambda i: (jnp.minimum((i + 1) * hb_per_tile, n_halo - 1), c))

    def const(shape):
        return pl.BlockSpec(shape, lambda i: (0,) * len(shape))

    in_specs = [colspec(0), colspec(0), colspec(3), colspec(4), colspec(5), colspec(6), colspec(7),
                prev(4), prev(5), nxt(4), nxt(5), colspec(0)]
    args = [hf, hb, proj, proj, proj, proj, proj, proj, proj, proj, proj, x]
    if has_pos:
        in_specs.append(pl.BlockSpec(big, lambda i: (i % tiles_per_seq, 0)))
        args.append(pos)
    in_specs += [pl.BlockSpec((None, 1, mod.shape[2]), lambda i: (row_fn(i), 0, 0)),
                 const((1, D_MODEL)), const((D_MODEL, D_MODEL)), const((D_MODEL, D_MODEL)), const((D_MODEL, D_MODEL)),
                 const((32, D_MODEL)), const((1, D_MODEL)), const((1, D_MODEL)), const((1, D_MODEL)),
                 const((1, D_MODEL)), const((1, D_MODEL))]
    args += [mod, lw["norm_w"], lw["w_a"], lw["w_cout"], lw["w_out"], lw["dw_w"], lw["dw_b"], lw["cln_w"],
             lw["cln_b"], lw["ln1_w"], lw["ln1_b"]]
    return pl.pallas_call(
        functools.partial(_tail_kernel, has_pos, tm, tiles_per_seq, alpha),
        grid=(t // tm,),
        in_specs=in_specs,
        out_specs=[pl.BlockSpec(big, lambda i: (i, 0)), pl.BlockSpec(big, lambda i: (i, 0))],
        out_shape=[jax.ShapeDtypeStruct((t, D_MODEL), F32), jax.ShapeDtypeStruct((t, D_MODEL), BF16)],
        scratch_shapes=[pltpu.VMEM((tm + 2 * CONV_HALO, D_MODEL), F32), pltpu.VMEM((tm, D_MODEL), F32)],
        compiler_params=_params(1),
        name="mixer_tail",
    )(*args)


_CAND_ROWS_Q = (16, 8, 8, 8, 8, 8, 8, 8)
_CAND_VALID_Q = tuple(PEER_TOPK // (q + 1) for q in range(8))


def _extract_top(s, n_rounds, vals_scr=None):
    rank = jnp.full(s.shape, float(n_rounds), F32)
    cur = s
    first = None
    zsum = None
    mx = None
    for r in range(n_rounds):
        mx = jnp.max(cur, axis=0, keepdims=True)
        hit = cur == mx
        rank = jnp.where(hit, float(r), rank)
        cur = jnp.where(hit, -jnp.inf, cur)
        if vals_scr is not None:
            vals_scr[r:r + 1, :] = mx
        if r == 0:
            first = mx
            zsum = jnp.ones_like(mx)
        else:
            zsum = zsum + jnp.exp(mx - first)
    return mx, rank, zsum


def _route_kernel(u_ref, wq_ref, sk_ref, rank2_ref, e2_ref, r_ref, a1_ref, a_scr, b_scr):
    u = u_ref[...]
    q1 = _dot_nt(wq_ref[0:PEER_NKEYS, :], u).astype(BF16)
    q2 = _dot_nt(wq_ref[PEER_NKEYS:2 * PEER_NKEYS, :], u).astype(BF16)
    s1 = _dot(sk_ref[0], q1)
    s2 = _dot(sk_ref[1], q2)
    _, rank1, _ = _extract_top(s1, PEER_TOPK, a_scr)
    _, rank2, _ = _extract_top(s2, PEER_TOPK, b_scr)
    a = a_scr[...]
    b = b_scr[...]
    prow = lax.broadcasted_iota(jnp.int32, (PEER_TOPK, 1), 0)

    pieces = []
    for q in range(8):
        rows = _CAND_ROWS_Q[q]
        cq = a[0:rows, :] + b[q:q + 1, :]
        pieces.append(jnp.where(prow[0:rows, :] < _CAND_VALID_Q[q], cq, -jnp.inf))
    pieces.append(a[0:1, :] + b[8:16, :])
    cand = jnp.concatenate(pieces, axis=0)
    tau, _, zsum = _extract_top(cand, PEER_TOPK)

    sel = jnp.where(cand >= tau, 1.0, 0.0)
    cnt = sel[0:16, :]
    off = 16
    for q in range(1, 8):
        cnt = cnt + jnp.concatenate([sel[off:off + 8, :], jnp.zeros_like(sel[0:8, :])], axis=0)
        off += 8
    tail_cnt = jnp.sum(sel[off:off + 8, :], axis=0, keepdims=True)
    cnt = cnt + jnp.where(prow == 0, tail_cnt, 0.0)

    r = jnp.zeros_like(s1)
    for p in range(PEER_TOPK):
        r = jnp.where(rank1 == float(p), cnt[p:p + 1, :], r)
    rank2_ref[...] = rank2.astype(BF16)
    e2_ref[...] = jnp.exp(s2 - b[0:1, :]).astype(BF16)
    r_ref[...] = r
    a1_ref[...] = jnp.exp(s1 - a[0:1, :]) / zsum


def _route(u2, wq_t, subkeys, tm):
    t = u2.shape[0]
    out_blk = pl.BlockSpec((None, PEER_NKEYS, tm), lambda i, h: (h, 0, i))
    shape = (PEER_HEADS, PEER_NKEYS, t)
    return pl.pallas_call(
        _route_kernel,
        grid=(t // tm, PEER_HEADS),
        in_specs=[pl.BlockSpec((tm, D_MODEL), lambda i, h: (i, 0)),
                  pl.BlockSpec((2 * PEER_NKEYS, D_MODEL), lambda i, h: (h, 0)),
                  pl.BlockSpec((None, 2, PEER_NKEYS, PEER_NKEYS), lambda i, h: (h, 0, 0, 0))],
        out_specs=[out_blk, out_blk, out_blk, out_blk],
        out_shape=[jax.ShapeDtypeStruct(shape, BF16), jax.ShapeDtypeStruct(shape, BF16),
                   jax.ShapeDtypeStruct(shape, F32), jax.ShapeDtypeStruct(shape, F32)],
        scratch_shapes=[pltpu.VMEM((PEER_TOPK, tm), F32), pltpu.VMEM((PEER_TOPK, tm), F32)],
        compiler_params=_params(2),
        name="peer_route",
    )(u2, wq_t, subkeys)


_GELU_C = math.sqrt(2.0 / math.pi)
_BF16_ROWS = 16
_F32_ROWS = 8
_MXU_N = 256


def _gelu_tanh(x):
    neg2z = x * (x * x * (-2.0 * _GELU_C * 0.044715) + (-2.0 * _GELU_C))
    return x / (1.0 + jnp.exp(neg2z))


def _experts_kernel(ib, n_eblk, alpha, u_ref, ut_ref, vt_ref, rank2_ref, e2_ref, r_ref, a1_ref, x1_ref, mod_ref,
                    lnw_ref, lnb_ref, y_ref, acc_scr, at_scr, p_scr, uT_scr):
    s = pl.program_id(1)

    @pl.when(s == 0)
    def _():
        acc_scr[...] = jnp.zeros_like(acc_scr)
        uT_scr[...] = u_ref[...].astype(F32).T.astype(BF16)

    tm = u_ref.shape[0]
    key_tiles = PEER_NKEYS // _BF16_ROWS
    rows_per_chunk = _MXU_N // PEER_NKEYS
    n_chunks = ib // rows_per_chunk

    def tile_rows(ref, h, ii, cols):
        group = ref[h, pl.ds(pl.multiple_of(s * ib + (ii // _F32_ROWS) * _F32_ROWS, _F32_ROWS), _F32_ROWS), cols]
        row = group[ii % _F32_ROWS:ii % _F32_ROWS + 1, :]
        tile = jnp.broadcast_to(row, (_BF16_ROWS, _MXU_N)).astype(BF16)
        return jnp.concatenate([tile] * key_tiles, axis=0)

    def weights(ii):
        rows = slice(ii * PEER_NKEYS, (ii + 1) * PEER_NKEYS)
        for c in range(tm // _MXU_N):
            cols = slice(c * _MXU_N, (c + 1) * _MXU_N)
            act = _gelu_tanh(at_scr[rows, cols].astype(BF16))
            w = None
            for h in range(PEER_HEADS):
                r_b = tile_rows(r_ref, h, ii, cols)
                a_b = tile_rows(a1_ref, h, ii, cols)
                term = jnp.where(rank2_ref[h, :, cols] < r_b, e2_ref[h, :, cols] * a_b, jnp.zeros((), BF16))
                w = term if w is None else w + term
            p_scr[rows, cols] = w * act

    for ch in range(n_chunks):
        erows = slice(ch * _MXU_N, (ch + 1) * _MXU_N)
        at_scr[erows, :] = _dot(ut_ref[erows, :], uT_scr[...])
    for ch in range(n_chunks):
        erows = slice(ch * _MXU_N, (ch + 1) * _MXU_N)
        for ii in range(ch * rows_per_chunk, (ch + 1) * rows_per_chunk):
            weights(ii)
        acc_scr[...] += _dot(vt_ref[:, erows], p_scr[erows, :])

    @pl.when(s == n_eblk - 1)
    def _():
        mod = mod_ref[...]
        gate2 = mod[:, 5 * D_MODEL:6 * D_MODEL]
        y = alpha * x1_ref[...] + gate2 * acc_scr[...].T
        y_ref[...] = _layer_norm(y, lnw_ref[...], lnb_ref[...])


def _experts(u2, u_tab, vt_tab, rank2, e2, r, a1, x1, mod, row_fn, lnw, lnb, tm, eb, alpha):
    t = u2.shape[0]
    n_exp = u_tab.shape[0]
    ib = eb // PEER_NKEYS
    n_eblk = n_exp // eb
    full = pl.BlockSpec((PEER_HEADS, PEER_NKEYS, tm), lambda i, s: (0, 0, i))
    tok = pl.BlockSpec((tm, D_MODEL), lambda i, s: (i, 0))
    vec = pl.BlockSpec((1, D_MODEL), lambda i, s: (0, 0))
    return pl.pallas_call(
        functools.partial(_experts_kernel, ib, n_eblk, alpha),
        grid=(t // tm, n_eblk),
        in_specs=[tok,
                  pl.BlockSpec((eb, D_MODEL), lambda i, s: (s, 0)),
                  pl.BlockSpec((D_MODEL, eb), lambda i, s: (0, s)),
                  full, full, full, full, tok,
                  pl.BlockSpec((None, 1, mod.shape[2]), lambda i, s: (row_fn(i), 0, 0)),
                  vec, vec],
        out_specs=tok,
        out_shape=jax.ShapeDtypeStruct((t, D_MODEL), F32),
        scratch_shapes=[pltpu.VMEM((D_MODEL, tm), F32), pltpu.VMEM((eb, tm), F32),
                        pltpu.VMEM((eb, tm), BF16), pltpu.VMEM((D_MODEL, tm), BF16)],
        compiler_params=_params(2),
        name="peer_experts",
    )(u2, u_tab, vt_tab, rank2, e2, r, a1, x1, mod, lnw, lnb)


def _grid_pos_embed(n_tokens):
    rows = n_tokens // GRID_W
    quarter = D_MODEL // 4
    freqs = jnp.exp(-math.log(POS_BASE) * jnp.arange(quarter, dtype=F32) / quarter)
    r = jnp.arange(rows, dtype=F32)[:, None] * freqs
    cl = jnp.arange(GRID_W, dtype=F32)[:, None] * freqs
    er = jnp.concatenate([jnp.sin(r), jnp.cos(r)], -1)
    ec = jnp.concatenate([jnp.sin(cl), jnp.cos(cl)], -1)
    emb = jnp.concatenate([jnp.broadcast_to(er[:, None, :], (rows, GRID_W, D_MODEL // 2)),
                           jnp.broadcast_to(ec[None, :, :], (rows, GRID_W, D_MODEL // 2))], -1)
    return emb.reshape(rows * GRID_W, D_MODEL)


def _pick_tile(seq_len, n_tokens, target):
    tm = min(target, seq_len)
    while seq_len % tm or n_tokens % tm:
        tm //= 2
    return tm


def _block(x, pos, mod, n_seq, seq_len, tm_in, row_of_token_tile, state0, want_state, lw, alpha):
    t = x.shape[0]
    proj, gates = _inproj(x, pos, mod, row_of_token_tile(tm_in), lw["w_main"], lw["b_main"], lw["wg_hi"],
                          lw["wg_lo"], lw["b_gate"], tm_in)
    res = _mlstm(proj, gates, n_seq, seq_len, state0, want_state)
    hf, hb = res[0], res[1]
    tm_tail = _pick_tile(seq_len, t, 512)
    x1, u2 = _tail(hf, hb, proj, x, pos, mod, row_of_token_tile(tm_tail), seq_len, tm_tail, alpha, lw)
    tm_p = _pick_tile(seq_len, t, 512)
    rank2, e2, r, a1 = _route(u2, lw["wq_t"], lw["subkeys"], tm_p)
    y = _experts(u2, lw["peer_u"], lw["peer_vt"], rank2, e2, r, a1, x1, mod, row_of_token_tile(tm_p),
                 lw["ln2_w"], lw["ln2_b"], tm_p, 1024, alpha)
    return y, res[2:]


def kernel(x_prompt, x_sample, state_C, state_n, state_m, c, c_ctx, w_in, b_in, mlstm_norm_w, w_a, conv_dw_w,
           conv_dw_b, conv_ln_w, conv_ln_b, w_conv_out, w_out, w_mod, b_mod, ln1_w, ln1_b, ln2_w, ln2_b,
           peer_w_query, peer_subkeys, peer_u, peer_v):
    depth = w_in.shape[0]
    alpha = (2.0 * depth) ** 0.25
    bsz, seq, _ = x_prompt.shape
    dbsz, dseq, _ = x_sample.shape
    units = 2 * N_HEADS
    gate_off = 4 * D_MODEL

    ctx = x_prompt.reshape(bsz * seq, D_MODEL)
    lat = x_sample.reshape(dbsz * dseq, D_MODEL)
    pos = _grid_pos_embed(dseq)
    n_rows = 1 + dbsz
    pad_rows = (-n_rows) % 8
    cvec = jnp.concatenate([c_ctx[None, :], c, jnp.zeros((pad_rows, D_MODEL), F32)], axis=0)

    new_c, new_n, new_m = [], [], []
    for l in range(depth):
        vec = lambda a: a[l].reshape(1, -1)
        w_l = w_in[l]
        b_l = b_in[l]
        wg = jnp.pad(w_l[:, gate_off:gate_off + N_GATES], ((0, 0), (0, GATE_PAD - N_GATES)))
        wg_hi = wg.astype(BF16)
        lw = {
            "w_main": jnp.concatenate([w_l[:, :gate_off], w_l[:, gate_off + N_GATES:]], axis=1).astype(BF16),
            "b_main": jnp.concatenate([b_l[:gate_off], b_l[gate_off + N_GATES:]]).reshape(1, -1),
            "wg_hi": wg_hi,
            "wg_lo": (wg - wg_hi.astype(F32)).astype(BF16),
            "b_gate": jnp.pad(b_l[gate_off:gate_off + N_GATES], (0, GATE_PAD - N_GATES)).reshape(1, -1),
            "norm_w": vec(mlstm_norm_w), "w_a": w_a[l].astype(BF16), "w_cout": w_conv_out[l].astype(BF16),
            "w_out": w_out[l].astype(BF16),
            "dw_w": jnp.pad(conv_dw_w[l], ((0, 32 - CONV_WIDTH), (0, 0))), "dw_b": vec(conv_dw_b),
            "cln_w": vec(conv_ln_w), "cln_b": vec(conv_ln_b), "ln1_w": vec(ln1_w), "ln1_b": vec(ln1_b),
            "ln2_w": vec(ln2_w), "ln2_b": vec(ln2_b),
            "wq_t": peer_w_query[l].T.astype(BF16),
            "subkeys": peer_subkeys[l].astype(BF16),
            "peer_u": peer_u[l].astype(BF16),
            "peer_vt": peer_v[l].T.astype(BF16),
        }
        mod = _modulation(cvec, w_mod[l], b_mod[l]).reshape(n_rows + pad_rows, 1, 6 * D_MODEL)

        ctx, (c_fin, n_fin, m_fin) = _block(ctx, None, mod, bsz, seq, _pick_tile(bsz * seq, bsz * seq, 1024),
                                             lambda tm: (lambda i: 0), None, True, lw, alpha)
        new_c.append(c_fin.reshape(bsz, 2, N_HEADS, HEAD_DIM, HEAD_DIM))
        new_n.append(n_fin.reshape(bsz, 2, N_HEADS, HEAD_DIM))
        new_m.append(m_fin[:, :, 0].reshape(bsz, 2, N_HEADS))

        state0 = (state_C[:, l].reshape(dbsz, units, HEAD_DIM, HEAD_DIM),
                  state_n[:, l].reshape(dbsz, units, HEAD_DIM),
                  jnp.broadcast_to(state_m[:, l].reshape(dbsz, units, 1), (dbsz, units, GATE_PAD)))
        lat, _ = _block(lat, pos, mod, dbsz, dseq, _pick_tile(dseq, dbsz * dseq, 1024),
                        lambda tm: (lambda i: 1 + (i * tm) // dseq), state0, False, lw, alpha)

    return (ctx.reshape(bsz, seq, D_MODEL), lat.reshape(dbsz, dseq, D_MODEL),
            jnp.stack(new_c, axis=1), jnp.stack(new_n, axis=1), jnp.stack(new_m, axis=1))
```

```python
import functools
import math

import jax
import jax.numpy as jnp
from jax import lax
from jax.experimental import pallas as pl
from jax.experimental.pallas import tpu as pltpu

F32 = jnp.float32
BF16 = jnp.bfloat16

D_MODEL = 1024
N_HEADS = 4
HEAD_DIM = 256
CHUNK = 128
CONV_WIDTH = 31
CONV_HALO = 16
N_GATES = 16
GATE_PAD = 128
GRID_W = 64
POS_BASE = 10000.0
LN_EPS = 1e-6
PEER_HEADS = 8
PEER_NKEYS = 128
PEER_TOPK = 16
V7X_VMEM_LIMIT = 56 * 1024 * 1024


def _params(n_axes):
    return pltpu.CompilerParams(dimension_semantics=("arbitrary",) * n_axes,
                                vmem_limit_bytes=V7X_VMEM_LIMIT)


def _sigmoid(x):
    return 1.0 / (1.0 + jnp.exp(-x))


def _log_sigmoid(x):
    return jnp.minimum(x, 0.0) - jnp.log(1.0 + jnp.exp(-jnp.abs(x)))


def _dot(a, b):
    return jnp.dot(a, b, preferred_element_type=F32)


def _dot_nt(a, b):
    return lax.dot_general(a, b, (((1,), (1,)), ((), ())), preferred_element_type=F32)


def _split2(x):
    hi = x.astype(BF16)
    lo = (x - hi.astype(F32)).astype(BF16)
    return hi, lo


def _split3(x):
    a = x.astype(BF16)
    r = x - a.astype(F32)
    b = r.astype(BF16)
    c = (r - b.astype(F32)).astype(BF16)
    return a, b, c


def _layer_norm(x, w, b):
    mu = jnp.mean(x, axis=-1, keepdims=True)
    xc = x - mu
    var = jnp.mean(xc * xc, axis=-1, keepdims=True)
    return xc * lax.rsqrt(var + LN_EPS) * w + b


def _mod_kernel(c_ref, w_ref, b_ref, o_ref):
    c = c_ref[...]
    s = c * _sigmoid(c)
    sh, sl = _split2(s)
    wh, wl = _split2(w_ref[...])
    o_ref[...] = _dot(sh, wh) + _dot(sl, wh) + _dot(sh, wl) + b_ref[...]


def _modulation(cvec, w_mod, b_mod):
    rows = cvec.shape[0]
    tn = 1536
    n = w_mod.shape[1]
    return pl.pallas_call(
        _mod_kernel,
        grid=(n // tn,),
        in_specs=[pl.BlockSpec((rows, D_MODEL), lambda j: (0, 0)),
                  pl.BlockSpec((D_MODEL, tn), lambda j: (0, j)),
                  pl.BlockSpec((1, tn), lambda j: (0, j))],
        out_specs=pl.BlockSpec((rows, tn), lambda j: (0, j)),
        out_shape=jax.ShapeDtypeStruct((rows, n), F32),
        compiler_params=_params(1),
        name="modulation",
    )(cvec, w_mod, b_mod.reshape(1, n))


_COL_K = 1


def _inproj_kernel(has_pos, *refs):
    if has_pos:
        x_ref, pos_ref, mod_ref, w_ref, b_ref, wgh_ref, wgl_ref, bg_ref, proj_ref, gates_ref, u_scr = refs
    else:
        x_ref, mod_ref, w_ref, b_ref, wgh_ref, wgl_ref, bg_ref, proj_ref, gates_ref, u_scr = refs
    j = pl.program_id(1)

    @pl.when(j == 0)
    def _():
        x = x_ref[...]
        if has_pos:
            x = x + pos_ref[...]
        mod = mod_ref[...]
        u = x * (1.0 + mod[:, D_MODEL:2 * D_MODEL]) + mod[:, 0:D_MODEL]
        uh, ul = _split2(u)
        u_scr[...] = uh
        wgh = wgh_ref[...]
        gates_ref[...] = _dot(uh, wgh) + _dot(ul, wgh) + _dot(uh, wgl_ref[...]) + bg_ref[...]

    acc = _dot(u_scr[...], w_ref[...]) + b_ref[...]
    is_sig = jnp.logical_or(j == 3, j >= 5)
    is_k = j == _COL_K

    @pl.when(is_sig)
    def _():
        proj_ref[...] = _sigmoid(acc).astype(BF16)

    @pl.when(is_k)
    def _():
        proj_ref[...] = (acc * (HEAD_DIM ** -0.5)).astype(BF16)

    @pl.when(jnp.logical_not(jnp.logical_or(is_sig, is_k)))
    def _():
        proj_ref[...] = acc.astype(BF16)


def _inproj(x, pos, mod, row_fn, w_main, b_main, wg_hi, wg_lo, b_gate, tm):
    t = x.shape[0]
    n_col = w_main.shape[1] // D_MODEL
    has_pos = pos is not None
    in_specs = [pl.BlockSpec((tm, D_MODEL), lambda i, j: (i, 0))]
    args = [x]
    if has_pos:
        pos_blocks = pos.shape[0] // tm
        in_specs.append(pl.BlockSpec((tm, D_MODEL), lambda i, j: (i % pos_blocks, 0)))
        args.append(pos)
    in_specs += [
        pl.BlockSpec((None, 1, mod.shape[2]), lambda i, j: (row_fn(i), 0, 0)),
        pl.BlockSpec((D_MODEL, D_MODEL), lambda i, j: (0, j)),
        pl.BlockSpec((1, D_MODEL), lambda i, j: (0, j)),
        pl.BlockSpec((D_MODEL, GATE_PAD), lambda i, j: (0, 0)),
        pl.BlockSpec((D_MODEL, GATE_PAD), lambda i, j: (0, 0)),
        pl.BlockSpec((1, GATE_PAD), lambda i, j: (0, 0)),
    ]
    args += [mod, w_main, b_main, wg_hi, wg_lo, b_gate]
    return pl.pallas_call(
        functools.partial(_inproj_kernel, has_pos),
        grid=(t // tm, n_col),
        in_specs=in_specs,
        out_specs=[pl.BlockSpec((tm, D_MODEL), lambda i, j: (i, j)),
                   pl.BlockSpec((tm, GATE_PAD), lambda i, j: (i, 0))],
        out_shape=[jax.ShapeDtypeStruct((t, n_col * D_MODEL), BF16),
                   jax.ShapeDtypeStruct((t, GATE_PAD), F32)],
        scratch_shapes=[pltpu.VMEM((tm, D_MODEL), BF16)],
        compiler_params=_params(2),
        name="inproj",
    )(*args)


def _mlstm_kernel(has_state, want_state, nc, *refs):
    refs = list(refs)
    qf, kf, vf, qb, kb, vb, gf, gb = refs[:8]
    pos = 8
    if has_state:
        c0_ref, n0_ref, m0_ref = refs[pos:pos + 3]
        pos += 3
    hf_ref, hb_ref = refs[pos:pos + 2]
    pos += 2
    if want_state:
        co_ref, no_ref, mo_ref = refs[pos:pos + 3]
        pos += 3
    c_scr, n_scr, m_scr = refs[pos:pos + 3]
    step = pl.program_id(1)

    @pl.when(step == 0)
    def _():
        if has_state:
            c_scr[...] = c0_ref[...]
            n_scr[...] = n0_ref[...]
            m_scr[...] = m0_ref[...]
        else:
            c_scr[...] = jnp.zeros_like(c_scr)
            n_scr[...] = jnp.zeros_like(n_scr)
            m_scr[...] = jnp.zeros_like(m_scr)

    row = lax.broadcasted_iota(jnp.int32, (CHUNK, CHUNK), 0)
    col = lax.broadcasted_iota(jnp.int32, (CHUNK, CHUNK), 1)

    for d, (q_ref, k_ref, v_ref, g_ref, h_ref) in enumerate(((qf, kf, vf, gf, hf_ref), (qb, kb, vb, gb, hb_ref))):
        valid = (col <= row) if d == 0 else (col >= row)
        tri = jnp.where(valid, 1.0, 0.0).astype(BF16)
        tri_t = jnp.where((row <= col) if d == 0 else (row >= col), 1.0, 0.0).astype(BF16)
        g = g_ref[...]
        lf = _log_sigmoid(g)
        g_t = g.T
        lf_t = lf.T
        l1, l2, l3 = _split3(lf)
        b_col_all = _dot(tri, l1) + _dot(tri, l2) + _dot(tri, l3)
        t1, t2, t3 = _split3(lf_t)
        b_row_all = _dot(t1, tri_t) + _dot(t2, tri_t) + _dot(t3, tri_t)
        last = CHUNK - 1 if d == 0 else 0
        for h in range(N_HEADS):
            u = d * N_HEADS + h
            ci = d * 2 * N_HEADS + h
            cf = ci + N_HEADS
            b_col = b_col_all[:, cf:cf + 1]
            b_row = b_row_all[cf:cf + 1, :]
            i_col = g[:, ci:ci + 1]
            i_row = g_t[ci:ci + 1, :]
            m = m_scr[u:u + 1, 0:1]
            hs = slice(h * HEAD_DIM, (h + 1) * HEAD_DIM)
            q = q_ref[:, hs]
            k = k_ref[:, hs]
            v = v_ref[:, hs]
            c_state = c_scr[u]
            n_state = n_scr[u:u + 1, :]

            dmat = jnp.where(valid, b_col - b_row + i_row, -jnp.inf)
            inter = b_col + m
            m_t = jnp.maximum(inter, jnp.max(dmat, axis=-1, keepdims=True))
            w_ts = jnp.exp(dmat - m_t)
            a_t = jnp.exp(inter - m_t)
            s = _dot_nt(q, k) * w_ts
            num = _dot(s.astype(BF16), v) + a_t * _dot_nt(q, c_state.astype(BF16))
            qf32 = q.astype(F32)
            den = jnp.sum(s, axis=-1, keepdims=True) + a_t * jnp.sum(qf32 * n_state, axis=-1, keepdims=True)
            h_ref[:, hs] = num / jnp.maximum(jnp.abs(den), jnp.exp(-m_t))

            b_last = b_row[:, last:last + 1]
            dec_col = b_last - b_col + i_col
            dec_row = b_last - b_row + i_row
            m_new = jnp.maximum(b_last + m, jnp.max(dec_row, axis=-1, keepdims=True))
            w_col = jnp.exp(dec_col - m_new)
            a_c = jnp.exp(b_last + m - m_new)
            vw_t = (v.astype(F32) * w_col).T.astype(BF16)
            c_scr[u] = a_c * c_state + _dot(vw_t, k)
            n_scr[u:u + 1, :] = a_c * n_state + jnp.sum(k.astype(F32) * w_col, axis=0, keepdims=True)
            m_scr[u:u + 1, :] = jnp.broadcast_to(m_new, (1, GATE_PAD))

    if want_state:
        @pl.when(step == nc - 1)
        def _():
            co_ref[...] = c_scr[...]
            no_ref[...] = n_scr[...]
            mo_ref[...] = m_scr[...]


def _mlstm(proj, gates, n_seq, seq_len, state0, want_state):
    t = proj.shape[0]
    nc = seq_len // CHUNK
    units = 2 * N_HEADS
    has_state = state0 is not None

    def fwd(c):
        return lambda s, k: (s * nc + k, c)

    def bwd(c):
        return lambda s, k: (s * nc + nc - 1 - k, c)

    tile = (CHUNK, D_MODEL)
    in_specs = [pl.BlockSpec(tile, fwd(0)), pl.BlockSpec(tile, fwd(1)), pl.BlockSpec(tile, fwd(2)),
                pl.BlockSpec(tile, bwd(0)), pl.BlockSpec(tile, bwd(1)), pl.BlockSpec(tile, bwd(2)),
                pl.BlockSpec((CHUNK, GATE_PAD), fwd(0)), pl.BlockSpec((CHUNK, GATE_PAD), bwd(0))]
    args = [proj] * 6 + [gates, gates]
    state_specs = [pl.BlockSpec((None, units, HEAD_DIM, HEAD_DIM), lambda s, k: (s, 0, 0, 0)),
                   pl.BlockSpec((None, units, HEAD_DIM), lambda s, k: (s, 0, 0)),
                   pl.BlockSpec((None, units, GATE_PAD), lambda s, k: (s, 0, 0))]
    state_shapes = [jax.ShapeDtypeStruct((n_seq, units, HEAD_DIM, HEAD_DIM), F32),
                    jax.ShapeDtypeStruct((n_seq, units, HEAD_DIM), F32),
                    jax.ShapeDtypeStruct((n_seq, units, GATE_PAD), F32)]
    if has_state:
        in_specs += state_specs
        args += list(state0)
    out_specs = [pl.BlockSpec(tile, fwd(0)), pl.BlockSpec(tile, bwd(0))]
    out_shape = [jax.ShapeDtypeStruct((t, D_MODEL), F32), jax.ShapeDtypeStruct((t, D_MODEL), F32)]
    if want_state:
        out_specs += state_specs
        out_shape += state_shapes
    return pl.pallas_call(
        functools.partial(_mlstm_kernel, has_state, want_state, nc),
        grid=(n_seq, nc),
        in_specs=in_specs,
        out_specs=out_specs,
        out_shape=out_shape,
        scratch_shapes=[pltpu.VMEM((units, HEAD_DIM, HEAD_DIM), F32),
                        pltpu.VMEM((units, HEAD_DIM), F32),
                        pltpu.VMEM((units, GATE_PAD), F32)],
        compiler_params=_params(2),
        name="mlstm",
    )(*args)


_CONV_ROWS = 64
_LANES = 128


def _tail_kernel(has_pos, tm, tiles_per_seq, alpha, *refs):
    refs = list(refs)
    hf_ref, hb_ref, so_ref, val_ref, sg_ref, sga_ref, sgb_ref, vp_ref, gp_ref, vn_ref, gn_ref, x_ref = refs[:12]
    pos = 12
    if has_pos:
        pos_ref = refs[pos]
        pos += 1
    (mod_ref, normw_ref, wa_ref, wc_ref, wo_ref, dww_ref, dwb_ref, clnw_ref, clnb_ref, ln1w_ref, ln1b_ref, wq_ref,
     x1_ref, u2_ref, q_ref, xpad, conv_scr) = refs[pos:]
    i = pl.program_id(0)

    hsum = hf_ref[...] + hb_ref[...]
    parts = []
    for h in range(N_HEADS):
        hh = hsum[:, h * HEAD_DIM:(h + 1) * HEAD_DIM]
        mu = jnp.mean(hh, axis=-1, keepdims=True)
        hc = hh - mu
        var = jnp.mean(hc * hc, axis=-1, keepdims=True)
        parts.append(hc * lax.rsqrt(var + LN_EPS))
    hn = jnp.concatenate(parts, axis=-1) * normw_ref[...]
    hg = (so_ref[...].astype(F32) * hn).astype(BF16)
    branch_a = _dot(hg, wa_ref[...])

    first = (i % tiles_per_seq) == 0
    last = (i % tiles_per_seq) == tiles_per_seq - 1
    keep_prev = jnp.where(first, 0.0, 1.0)
    keep_next = jnp.where(last, 0.0, 1.0)
    xpad[0:CONV_HALO, :] = vp_ref[...].astype(F32) * gp_ref[...].astype(F32) * keep_prev
    xpad[CONV_HALO:CONV_HALO + tm, :] = val_ref[...].astype(F32) * sg_ref[...].astype(F32)
    xpad[CONV_HALO + tm:2 * CONV_HALO + tm, :] = vn_ref[...].astype(F32) * gn_ref[...].astype(F32) * keep_next
    tap0 = CONV_HALO - CONV_WIDTH // 2

    def col_body(c, carry):
        cs = pl.ds(pl.multiple_of(c * _LANES, _LANES), _LANES)
        for r in range(tm // _CONV_ROWS):
            acc = jnp.broadcast_to(dwb_ref[:, cs], (_CONV_ROWS, _LANES))
            for k in range(CONV_WIDTH):
                start = tap0 + k + r * _CONV_ROWS
                acc = acc + xpad[pl.ds(start, _CONV_ROWS), cs] * dww_ref[k:k + 1, cs]
            conv_scr[pl.ds(r * _CONV_ROWS, _CONV_ROWS), cs] = acc
        return carry

    lax.fori_loop(0, D_MODEL // _LANES, col_body, 0)
    xc = _layer_norm(conv_scr[...], clnw_ref[...], clnb_ref[...])
    xc = (xc * _sigmoid(xc)).astype(BF16)
    branch_b = _dot(xc, wc_ref[...])

    merged = sga_ref[...].astype(F32) * branch_a + sgb_ref[...].astype(F32) * branch_b
    mix = _dot(merged.astype(BF16), wo_ref[...])

    mod = mod_ref[...]
    gate1 = mod[:, 2 * D_MODEL:3 * D_MODEL]
    shift2 = mod[:, 3 * D_MODEL:4 * D_MODEL]
    scale2 = mod[:, 4 * D_MODEL:5 * D_MODEL]
    x = x_ref[...]
    if has_pos:
        x = x + pos_ref[...]
    x1 = _layer_norm(alpha * x + gate1 * mix, ln1w_ref[...], ln1b_ref[...])
    x1_ref[...] = x1
    u2 = (x1 * (1.0 + scale2) + shift2).astype(BF16)
    u2_ref[...] = u2
    q_ref[...] = _dot(u2, wq_ref[...]).astype(BF16)


def _tail(hf, hb, proj, x, pos, mod, row_fn, seq_len, tm, alpha, lw):
    t = x.shape[0]
    has_pos = pos is not None
    tiles_per_seq = seq_len // tm
    hb_per_tile = tm // CONV_HALO
    n_halo = t // CONV_HALO
    big = (tm, D_MODEL)
    halo = (CONV_HALO, D_MODEL)

    def colspec(c):
        return pl.BlockSpec(big, lambda i: (i, c))

    def prev(c):
        return pl.BlockSpec(halo, lambda i: (jnp.maximum(i * hb_per_tile - 1, 0), c))

    def nxt(c):
        return pl.BlockSpec(halo, lambda i: (jnp.minimum((i + 1) * hb_per_tile, n_halo - 1), c))

    def const(shape):
        return pl.BlockSpec(shape, lambda i: (0,) * len(shape))

    in_specs = [colspec(0), colspec(0), colspec(3), colspec(4), colspec(5), colspec(6), colspec(7),
                prev(4), prev(5), nxt(4), nxt(5), colspec(0)]
    args = [hf, hb, proj, proj, proj, proj, proj, proj, proj, proj, proj, x]
    if has_pos:
        in_specs.append(pl.BlockSpec(big, lambda i: (i % tiles_per_seq, 0)))
        args.append(pos)
    in_specs += [pl.BlockSpec((None, 1, mod.shape[2]), lambda i: (row_fn(i), 0, 0)),
                 const((1, D_MODEL)), const((D_MODEL, D_MODEL)), const((D_MODEL, D_MODEL)), const((D_MODEL, D_MODEL)),
                 const((32, D_MODEL)), const((1, D_MODEL)), const((1, D_MODEL)), const((1, D_MODEL)),
                 const((1, D_MODEL)), const((1, D_MODEL)), const(lw["wq"].shape)]
    args += [mod, lw["norm_w"], lw["w_a"], lw["w_cout"], lw["w_out"], lw["dw_w"], lw["dw_b"], lw["cln_w"],
             lw["cln_b"], lw["ln1_w"], lw["ln1_b"], lw["wq"]]
    n_q = lw["wq"].shape[1]
    return pl.pallas_call(
        functools.partial(_tail_kernel, has_pos, tm, tiles_per_seq, alpha),
        grid=(t // tm,),
        in_specs=in_specs,
        out_specs=[pl.BlockSpec(big, lambda i: (i, 0)), pl.BlockSpec(big, lambda i: (i, 0)),
                   pl.BlockSpec((tm, n_q), lambda i: (i, 0))],
        out_shape=[jax.ShapeDtypeStruct((t, D_MODEL), F32), jax.ShapeDtypeStruct((t, D_MODEL), BF16),
                   jax.ShapeDtypeStruct((t, n_q), BF16)],
        scratch_shapes=[pltpu.VMEM((tm + 2 * CONV_HALO, D_MODEL), F32), pltpu.VMEM((tm, D_MODEL), F32)],
        compiler_params=_params(1),
        name="mixer_tail",
    )(*args)


_TOK_BLOCKS = 8
_LANE = 128
_ROUTE_TM = _TOK_BLOCKS * _LANE


def _merge_exchange_pairs(n):
    pairs = []
    t = (n - 1).bit_length()
    p = 1 << (t - 1)
    while p > 0:
        q, r, d = 1 << (t - 1), 0, p
        while d > 0:
            pairs.extend((i, i + d) for i in range(n - d) if (i & p) == r)
            d, q, r = q - p, q >> 1, p
        p >>= 1
    return tuple(pairs)


_SORT16 = _merge_exchange_pairs(PEER_TOPK)


def _sort_desc(vals):
    vals = list(vals)
    for i, j in _SORT16:
        vals[i], vals[j] = jnp.maximum(vals[i], vals[j]), jnp.minimum(vals[i], vals[j])
    return vals


def _bitonic_desc(vals):
    vals = list(vals)
    d = len(vals) // 2
    while d > 0:
        for i in range(len(vals)):
            if (i & d) == 0:
                vals[i], vals[i + d] = jnp.maximum(vals[i], vals[i + d]), jnp.minimum(vals[i], vals[i + d])
        d //= 2
    return vals


def _top_merge(a, b):
    n = len(a)
    return _bitonic_desc([jnp.maximum(a[i], b[n - 1 - i]) for i in range(n)])


def _top16(keys):
    if len(keys) == PEER_TOPK:
        return _sort_desc(keys)
    half = len(keys) // 2
    return _top_merge(_top16(keys[:half]), _top16(keys[half:]))


def _prefix_count(test, v):
    t8 = test(v[7])
    t4 = test(jnp.where(t8, v[11], v[3]))
    t2 = test(jnp.where(t8, jnp.where(t4, v[13], v[9]), jnp.where(t4, v[5], v[1])))
    lo = jnp.where(t4, jnp.where(t2, v[6], v[4]), jnp.where(t2, v[2], v[0]))
    hi = jnp.where(t4, jnp.where(t2, v[14], v[12]), jnp.where(t2, v[10], v[8]))
    t1 = test(jnp.where(t8, hi, lo))
    cnt = (jnp.where(t8, 8.0, 0.0) + jnp.where(t4, 4.0, 0.0)) + (jnp.where(t2, 2.0, 0.0) + jnp.where(t1, 1.0, 0.0))
    return jnp.where(test(v[15]), 16.0, cnt)


def _joint_top16(a, b):
    q0 = [a[p] + b[0] for p in range(16)]
    q1 = [a[p] + b[1] for p in range(8)]
    p0 = [a[0] + b[q] for q in range(8, 16)]
    m1 = _bitonic_desc(q1 + p0[::-1])
    m2 = _sort_desc([a[p] + b[q] for q, n in ((2, 5), (3, 4), (4, 3), (5, 2), (6, 2)) for p in range(n)])
    q7 = [a[0] + b[7], a[1] + b[7]]
    t2 = _bitonic_desc(m2[:14] + [jnp.maximum(m2[14], q7[1]), jnp.maximum(m2[15], q7[0])])
    return _top_merge(_top_merge(q0, m1), t2)


def _route_kernel(q_ref, skx_ref, rank2_ref, e2_ref, r_ref, a1_ref, km_rank2, km_e2, km_r, km_a1):
    nb = _TOK_BLOCKS
    keys = []
    for p in range(2):
        qp = jnp.concatenate([q_ref[a * _LANE:(a + 1) * _LANE, p * PEER_NKEYS:(p + 1) * PEER_NKEYS]
                              for a in range(nb)], axis=1)
        sp = _dot_nt(skx_ref[p], qp)
        keys.append([sp[k * nb:(k + 1) * nb, :] for k in range(PEER_NKEYS)])
    s1, s2 = keys
    a = _top16(s1)
    b = _top16(s2)
    top = _joint_top16(a, b)
    tau = top[PEER_TOPK - 1]
    zsum = jnp.ones_like(tau)
    for c in top[1:]:
        zsum = zsum + jnp.exp(c - top[0])
    inv_z = 1.0 / zsum
    b_asc = b[::-1]
    for k in range(PEER_NKEYS):
        rows = slice(k * nb, (k + 1) * nb)
        km_r[rows, :] = _prefix_count(lambda bq, x=s1[k]: x + bq >= tau, b)
        km_a1[rows, :] = jnp.exp(s1[k] - a[0]) * inv_z
        km_rank2[rows, :] = float(PEER_TOPK) - _prefix_count(lambda bq, x=s2[k]: x >= bq, b_asc)
        km_e2[rows, :] = jnp.exp(s2[k] - b[0])
    for blk in range(nb):
        cols = slice(blk * _LANE, (blk + 1) * _LANE)
        rows = pl.ds(blk, PEER_NKEYS, stride=nb)
        rank2_ref[:, cols] = km_rank2[rows, :].astype(BF16)
        e2_ref[:, cols] = km_e2[rows, :].astype(BF16)
        r_ref[:, cols] = km_r[rows, :]
        a1_ref[:, cols] = km_a1[rows, :]


def _route(q, skx):
    t = q.shape[0]
    tm = _ROUTE_TM
    assert t % tm == 0, (t, tm)
    out_blk = pl.BlockSpec((None, PEER_NKEYS, tm), lambda h, i: (h, 0, i))
    shape = (PEER_HEADS, PEER_NKEYS, t)
    km = pltpu.VMEM((PEER_NKEYS * _TOK_BLOCKS, _LANE), F32)
    return pl.pallas_call(
        _route_kernel,
        grid=(PEER_HEADS, t // tm),
        in_specs=[pl.BlockSpec((tm, 2 * PEER_NKEYS), lambda h, i: (i, h)),
                  pl.BlockSpec((None, 2, PEER_NKEYS * _TOK_BLOCKS, _TOK_BLOCKS * PEER_NKEYS),
                               lambda h, i: (h, 0, 0, 0))],
        out_specs=[out_blk, out_blk, out_blk, out_blk],
        out_shape=[jax.ShapeDtypeStruct(shape, BF16), jax.ShapeDtypeStruct(shape, BF16),
                   jax.ShapeDtypeStruct(shape, F32), jax.ShapeDtypeStruct(shape, F32)],
        scratch_shapes=[km, km, km, km],
        compiler_params=_params(2),
        name="peer_route",
    )(q, skx)


_GELU_C = math.sqrt(2.0 / math.pi)
_BF16_ROWS = 16
_F32_ROWS = 8
_MXU_N = 256


def _gelu_tanh(x):
    neg2z = x * (x * x * (-2.0 * _GELU_C * 0.044715) + (-2.0 * _GELU_C))
    return x / (1.0 + jnp.exp(neg2z))


def _experts_kernel(ib, n_eblk, alpha, u_ref, ut_ref, vt_ref, rank2_ref, e2_ref, r_ref, a1_ref, x1_ref, mod_ref,
                    lnw_ref, lnb_ref, y_ref, acc_scr, at_scr, p_scr, uT_scr):
    s = pl.program_id(1)

    @pl.when(s == 0)
    def _():
        acc_scr[...] = jnp.zeros_like(acc_scr)
        uT_scr[...] = u_ref[...].astype(F32).T.astype(BF16)

    tm = u_ref.shape[0]
    key_tiles = PEER_NKEYS // _BF16_ROWS
    rows_per_chunk = _MXU_N // PEER_NKEYS
    n_chunks = ib // rows_per_chunk

    def tile_rows(ref, h, ii, cols):
        group = ref[h, pl.ds(pl.multiple_of(s * ib + (ii // _F32_ROWS) * _F32_ROWS, _F32_ROWS), _F32_ROWS), cols]
        row = group[ii % _F32_ROWS:ii % _F32_ROWS + 1, :]
        tile = jnp.broadcast_to(row, (_BF16_ROWS, _MXU_N)).astype(BF16)
        return jnp.concatenate([tile] * key_tiles, axis=0)

    def weights(ii):
        rows = slice(ii * PEER_NKEYS, (ii + 1) * PEER_NKEYS)
        for c in range(tm // _MXU_N):
            cols = slice(c * _MXU_N, (c + 1) * _MXU_N)
            act = _gelu_tanh(at_scr[rows, cols].astype(BF16))
            w = None
            for h in range(PEER_HEADS):
                r_b = tile_rows(r_ref, h, ii, cols)
                a_b = tile_rows(a1_ref, h, ii, cols)
                term = jnp.where(rank2_ref[h, :, cols] < r_b, e2_ref[h, :, cols] * a_b, jnp.zeros((), BF16))
                w = term if w is None else w + term
            p_scr[rows, cols] = w * act

    for ch in range(n_chunks):
        erows = slice(ch * _MXU_N, (ch + 1) * _MXU_N)
        at_scr[erows, :] = _dot(ut_ref[erows, :], uT_scr[...])
    for ch in range(n_chunks):
        erows = slice(ch * _MXU_N, (ch + 1) * _MXU_N)
        for ii in range(ch * rows_per_chunk, (ch + 1) * rows_per_chunk):
            weights(ii)
        acc_scr[...] += _dot(vt_ref[:, erows], p_scr[erows, :])

    @pl.when(s == n_eblk - 1)
    def _():
        mod = mod_ref[...]
        gate2 = mod[:, 5 * D_MODEL:6 * D_MODEL]
        y = alpha * x1_ref[...] + gate2 * acc_scr[...].T
        y_ref[...] = _layer_norm(y, lnw_ref[...], lnb_ref[...])


def _experts(u2, u_tab, vt_tab, rank2, e2, r, a1, x1, mod, row_fn, lnw, lnb, tm, eb, alpha):
    t = u2.shape[0]
    n_exp = u_tab.shape[0]
    ib = eb // PEER_NKEYS
    n_eblk = n_exp // eb
    full = pl.BlockSpec((PEER_HEADS, PEER_NKEYS, tm), lambda i, s: (0, 0, i))
    tok = pl.BlockSpec((tm, D_MODEL), lambda i, s: (i, 0))
    vec = pl.BlockSpec((1, D_MODEL), lambda i, s: (0, 0))
    return pl.pallas_call(
        functools.partial(_experts_kernel, ib, n_eblk, alpha),
        grid=(t // tm, n_eblk),
        in_specs=[tok,
                  pl.BlockSpec((eb, D_MODEL), lambda i, s: (s, 0)),
                  pl.BlockSpec((D_MODEL, eb), lambda i, s: (0, s)),
                  full, full, full, full, tok,
                  pl.BlockSpec((None, 1, mod.shape[2]), lambda i, s: (row_fn(i), 0, 0)),
                  vec, vec],
        out_specs=tok,
        out_shape=jax.ShapeDtypeStruct((t, D_MODEL), F32),
        scratch_shapes=[pltpu.VMEM((D_MODEL, tm), F32), pltpu.VMEM((eb, tm), F32),
                        pltpu.VMEM((eb, tm), BF16), pltpu.VMEM((D_MODEL, tm), BF16)],
        compiler_params=_params(2),
        name="peer_experts",
    )(u2, u_tab, vt_tab, rank2, e2, r, a1, x1, mod, lnw, lnb)


def _grid_pos_embed(n_tokens):
    rows = n_tokens // GRID_W
    quarter = D_MODEL // 4
    freqs = jnp.exp(-math.log(POS_BASE) * jnp.arange(quarter, dtype=F32) / quarter)
    r = jnp.arange(rows, dtype=F32)[:, None] * freqs
    cl = jnp.arange(GRID_W, dtype=F32)[:, None] * freqs
    er = jnp.concatenate([jnp.sin(r), jnp.cos(r)], -1)
    ec = jnp.concatenate([jnp.sin(cl), jnp.cos(cl)], -1)
    emb = jnp.concatenate([jnp.broadcast_to(er[:, None, :], (rows, GRID_W, D_MODEL // 2)),
                           jnp.broadcast_to(ec[None, :, :], (rows, GRID_W, D_MODEL // 2))], -1)
    return emb.reshape(rows * GRID_W, D_MODEL)


def _pick_tile(seq_len, n_tokens, target):
    tm = min(target, seq_len)
    while seq_len % tm or n_tokens % tm:
        tm //= 2
    return tm


def _block(x, pos, mod, n_seq, seq_len, tm_in, row_of_token_tile, state0, want_state, lw, alpha):
    t = x.shape[0]
    proj, gates = _inproj(x, pos, mod, row_of_token_tile(tm_in), lw["w_main"], lw["b_main"], lw["wg_hi"],
                          lw["wg_lo"], lw["b_gate"], tm_in)
    res = _mlstm(proj, gates, n_seq, seq_len, state0, want_state)
    hf, hb = res[0], res[1]
    tm_tail = _pick_tile(seq_len, t, 512)
    x1, u2, q = _tail(hf, hb, proj, x, pos, mod, row_of_token_tile(tm_tail), seq_len, tm_tail, alpha, lw)
    tm_p = _pick_tile(seq_len, t, 512)
    rank2, e2, r, a1 = _route(q, lw["skx"])
    y = _experts(u2, lw["peer_u"], lw["peer_vt"], rank2, e2, r, a1, x1, mod, row_of_token_tile(tm_p),
                 lw["ln2_w"], lw["ln2_b"], tm_p, 1024, alpha)
    return y, res[2:]


def kernel(x_prompt, x_sample, state_C, state_n, state_m, c, c_ctx, w_in, b_in, mlstm_norm_w, w_a, conv_dw_w,
           conv_dw_b, conv_ln_w, conv_ln_b, w_conv_out, w_out, w_mod, b_mod, ln1_w, ln1_b, ln2_w, ln2_b,
           peer_w_query, peer_subkeys, peer_u, peer_v):
    depth = w_in.shape[0]
    alpha = (2.0 * depth) ** 0.25
    bsz, seq, _ = x_prompt.shape
    dbsz, dseq, _ = x_sample.shape
    units = 2 * N_HEADS
    gate_off = 4 * D_MODEL

    ctx = x_prompt.reshape(bsz * seq, D_MODEL)
    lat = x_sample.reshape(dbsz * dseq, D_MODEL)
    pos = _grid_pos_embed(dseq)
    n_rows = 1 + dbsz
    pad_rows = (-n_rows) % 8
    cvec = jnp.concatenate([c_ctx[None, :], c, jnp.zeros((pad_rows, D_MODEL), F32)], axis=0)

    new_c, new_n, new_m = [], [], []
    for l in range(depth):
        vec = lambda a: a[l].reshape(1, -1)
        w_l = w_in[l]
        b_l = b_in[l]
        wg = jnp.pad(w_l[:, gate_off:gate_off + N_GATES], ((0, 0), (0, GATE_PAD - N_GATES)))
        wg_hi = wg.astype(BF16)
        lw = {
            "w_main": jnp.concatenate([w_l[:, :gate_off], w_l[:, gate_off + N_GATES:]], axis=1).astype(BF16),
            "b_main": jnp.concatenate([b_l[:gate_off], b_l[gate_off + N_GATES:]]).reshape(1, -1),
            "wg_hi": wg_hi,
            "wg_lo": (wg - wg_hi.astype(F32)).astype(BF16),
            "b_gate": jnp.pad(b_l[gate_off:gate_off + N_GATES], (0, GATE_PAD - N_GATES)).reshape(1, -1),
            "norm_w": vec(mlstm_norm_w), "w_a": w_a[l].astype(BF16), "w_cout": w_conv_out[l].astype(BF16),
            "w_out": w_out[l].astype(BF16),
            "dw_w": jnp.pad(conv_dw_w[l], ((0, 32 - CONV_WIDTH), (0, 0))), "dw_b": vec(conv_dw_b),
            "cln_w": vec(conv_ln_w), "cln_b": vec(conv_ln_b), "ln1_w": vec(ln1_w), "ln1_b": vec(ln1_b),
            "ln2_w": vec(ln2_w), "ln2_b": vec(ln2_b),
            "wq": peer_w_query[l].astype(BF16),
            "skx": jnp.einsum("hpkd,ab->hpkabd", peer_subkeys[l].astype(BF16), jnp.eye(_TOK_BLOCKS, dtype=BF16))
            .reshape(PEER_HEADS, 2, PEER_NKEYS * _TOK_BLOCKS, _TOK_BLOCKS * PEER_NKEYS),
            "peer_u": peer_u[l].astype(BF16),
            "peer_vt": peer_v[l].T.astype(BF16),
        }
        mod = _modulation(cvec, w_mod[l], b_mod[l]).reshape(n_rows + pad_rows, 1, 6 * D_MODEL)

        ctx, (c_fin, n_fin, m_fin) = _block(ctx, None, mod, bsz, seq, _pick_tile(bsz * seq, bsz * seq, 1024),
                                             lambda tm: (lambda i: 0), None, True, lw, alpha)
        new_c.append(c_fin.reshape(bsz, 2, N_HEADS, HEAD_DIM, HEAD_DIM))
        new_n.append(n_fin.reshape(bsz, 2, N_HEADS, HEAD_DIM))
        new_m.append(m_fin[:, :, 0].reshape(bsz, 2, N_HEADS))

        state0 = (state_C[:, l].reshape(dbsz, units, HEAD_DIM, HEAD_DIM),
                  state_n[:, l].reshape(dbsz, units, HEAD_DIM),
                  jnp.broadcast_to(state_m[:, l].reshape(dbsz, units, 1), (dbsz, units, GATE_PAD)))
        lat, _ = _block(lat, pos, mod, dbsz, dseq, _pick_tile(dseq, dbsz * dseq, 1024),
                        lambda tm: (lambda i: 1 + (i * tm) // dseq), state0, False, lw, alpha)

    return (ctx.reshape(bsz, seq, D_MODEL), lat.reshape(dbsz, dseq, D_MODEL),
            jnp.stack(new_c, axis=1), jnp.stack(new_n, axis=1), jnp.stack(new_m, axis=1))
```

```python
import functools
import math

import jax
import jax.numpy as jnp
from jax import lax
from jax.experimental import pallas as pl
from jax.experimental.pallas import tpu as pltpu

F32 = jnp.float32
BF16 = jnp.bfloat16

D_MODEL = 1024
N_HEADS = 4
HEAD_DIM = 256
CHUNK = 128
CONV_WIDTH = 31
CONV_HALO = 16
N_GATES = 16
GATE_PAD = 128
GRID_W = 64
POS_BASE = 10000.0
LN_EPS = 1e-6
PEER_HEADS = 8
PEER_NKEYS = 128
PEER_TOPK = 16
V7X_VMEM_LIMIT = 56 * 1024 * 1024


def _params(n_axes):
    return pltpu.CompilerParams(dimension_semantics=("arbitrary",) * n_axes,
                                vmem_limit_bytes=V7X_VMEM_LIMIT)


def _sigmoid(x):
    return 1.0 / (1.0 + jnp.exp(-x))


def _log_sigmoid(x):
    return jnp.minimum(x, 0.0) - jnp.log(1.0 + jnp.exp(-jnp.abs(x)))


def _dot(a, b):
    return jnp.dot(a, b, preferred_element_type=F32)


def _dot_nt(a, b):
    return lax.dot_general(a, b, (((1,), (1,)), ((), ())), preferred_element_type=F32)


def _split2(x):
    hi = x.astype(BF16)
    lo = (x - hi.astype(F32)).astype(BF16)
    return hi, lo


def _split3(x):
    a = x.astype(BF16)
    r = x - a.astype(F32)
    b = r.astype(BF16)
    c = (r - b.astype(F32)).astype(BF16)
    return a, b, c


def _layer_norm(x, w, b):
    mu = jnp.mean(x, axis=-1, keepdims=True)
    xc = x - mu
    var = jnp.mean(xc * xc, axis=-1, keepdims=True)
    return xc * lax.rsqrt(var + LN_EPS) * w + b


def _mod_kernel(c_ref, w_ref, b_ref, o_ref):
    c = c_ref[...]
    s = c * _sigmoid(c)
    sh, sl = _split2(s)
    wh, wl = _split2(w_ref[...])
    o_ref[...] = _dot(sh, wh) + _dot(sl, wh) + _dot(sh, wl) + b_ref[...]


def _modulation(cvec, w_mod, b_mod):
    rows = cvec.shape[0]
    tn = 1536
    n = w_mod.shape[1]
    return pl.pallas_call(
        _mod_kernel,
        grid=(n // tn,),
        in_specs=[pl.BlockSpec((rows, D_MODEL), lambda j: (0, 0)),
                  pl.BlockSpec((D_MODEL, tn), lambda j: (0, j)),
                  pl.BlockSpec((1, tn), lambda j: (0, j))],
        out_specs=pl.BlockSpec((rows, tn), lambda j: (0, j)),
        out_shape=jax.ShapeDtypeStruct((rows, n), F32),
        compiler_params=_params(1),
        name="modulation",
    )(cvec, w_mod, b_mod.reshape(1, n))


_COL_K = 1


def _inproj_kernel(has_pos, *refs):
    if has_pos:
        x_ref, pos_ref, mod_ref, w_ref, b_ref, wgh_ref, wgl_ref, bg_ref, proj_ref, gates_ref, u_scr = refs
    else:
        x_ref, mod_ref, w_ref, b_ref, wgh_ref, wgl_ref, bg_ref, proj_ref, gates_ref, u_scr = refs
    j = pl.program_id(1)

    @pl.when(j == 0)
    def _():
        x = x_ref[...]
        if has_pos:
            x = x + pos_ref[...]
        mod = mod_ref[...]
        u = x * (1.0 + mod[:, D_MODEL:2 * D_MODEL]) + mod[:, 0:D_MODEL]
        uh, ul = _split2(u)
        u_scr[...] = uh
        wgh = wgh_ref[...]
        gates_ref[...] = _dot(uh, wgh) + _dot(ul, wgh) + _dot(uh, wgl_ref[...]) + bg_ref[...]

    acc = _dot(u_scr[...], w_ref[...]) + b_ref[...]
    is_sig = jnp.logical_or(j == 3, j >= 5)
    is_k = j == _COL_K

    @pl.when(is_sig)
    def _():
        proj_ref[...] = _sigmoid(acc).astype(BF16)

    @pl.when(is_k)
    def _():
        proj_ref[...] = (acc * (HEAD_DIM ** -0.5)).astype(BF16)

    @pl.when(jnp.logical_not(jnp.logical_or(is_sig, is_k)))
    def _():
        proj_ref[...] = acc.astype(BF16)


def _inproj(x, pos, mod, row_fn, w_main, b_main, wg_hi, wg_lo, b_gate, tm):
    t = x.shape[0]
    n_col = w_main.shape[1] // D_MODEL
    has_pos = pos is not None
    in_specs = [pl.BlockSpec((tm, D_MODEL), lambda i, j: (i, 0))]
    args = [x]
    if has_pos:
        pos_blocks = pos.shape[0] // tm
        in_specs.append(pl.BlockSpec((tm, D_MODEL), lambda i, j: (i % pos_blocks, 0)))
        args.append(pos)
    in_specs += [
        pl.BlockSpec((None, 1, mod.shape[2]), lambda i, j: (row_fn(i), 0, 0)),
        pl.BlockSpec((D_MODEL, D_MODEL), lambda i, j: (0, j)),
        pl.BlockSpec((1, D_MODEL), lambda i, j: (0, j)),
        pl.BlockSpec((D_MODEL, GATE_PAD), lambda i, j: (0, 0)),
        pl.BlockSpec((D_MODEL, GATE_PAD), lambda i, j: (0, 0)),
        pl.BlockSpec((1, GATE_PAD), lambda i, j: (0, 0)),
    ]
    args += [mod, w_main, b_main, wg_hi, wg_lo, b_gate]
    return pl.pallas_call(
        functools.partial(_inproj_kernel, has_pos),
        grid=(t // tm, n_col),
        in_specs=in_specs,
        out_specs=[pl.BlockSpec((tm, D_MODEL), lambda i, j: (i, j)),
                   pl.BlockSpec((tm, GATE_PAD), lambda i, j: (i, 0))],
        out_shape=[jax.ShapeDtypeStruct((t, n_col * D_MODEL), BF16),
                   jax.ShapeDtypeStruct((t, GATE_PAD), F32)],
        scratch_shapes=[pltpu.VMEM((tm, D_MODEL), BF16)],
        compiler_params=_params(2),
        name="inproj",
    )(*args)


def _mlstm_kernel(has_state, want_state, nc, *refs):
    refs = list(refs)
    qf, kf, vf, qb, kb, vb, gf, gb = refs[:8]
    pos = 8
    if has_state:
        c0_ref, n0_ref, m0_ref = refs[pos:pos + 3]
        pos += 3
    hf_ref, hb_ref = refs[pos:pos + 2]
    pos += 2
    if want_state:
        co_ref, no_ref, mo_ref = refs[pos:pos + 3]
        pos += 3
    c_scr, n_scr, m_scr = refs[pos:pos + 3]
    step = pl.program_id(1)

    @pl.when(step == 0)
    def _():
        if has_state:
            c_scr[...] = c0_ref[...]
            n_scr[...] = n0_ref[...]
            m_scr[...] = m0_ref[...]
        else:
            c_scr[...] = jnp.zeros_like(c_scr)
            n_scr[...] = jnp.zeros_like(n_scr)
            m_scr[...] = jnp.zeros_like(m_scr)

    row = lax.broadcasted_iota(jnp.int32, (CHUNK, CHUNK), 0)
    col = lax.broadcasted_iota(jnp.int32, (CHUNK, CHUNK), 1)

    for d, (q_ref, k_ref, v_ref, g_ref, h_ref) in enumerate(((qf, kf, vf, gf, hf_ref), (qb, kb, vb, gb, hb_ref))):
        valid = (col <= row) if d == 0 else (col >= row)
        tri = jnp.where(valid, 1.0, 0.0).astype(BF16)
        tri_t = jnp.where((row <= col) if d == 0 else (row >= col), 1.0, 0.0).astype(BF16)
        g = g_ref[...]
        lf = _log_sigmoid(g)
        g_t = g.T
        lf_t = lf.T
        l1, l2, l3 = _split3(lf)
        b_col_all = _dot(tri, l1) + _dot(tri, l2) + _dot(tri, l3)
        t1, t2, t3 = _split3(lf_t)
        b_row_all = _dot(t1, tri_t) + _dot(t2, tri_t) + _dot(t3, tri_t)
        last = CHUNK - 1 if d == 0 else 0
        for h in range(N_HEADS):
            u = d * N_HEADS + h
            ci = d * 2 * N_HEADS + h
            cf = ci + N_HEADS
            b_col = b_col_all[:, cf:cf + 1]
            b_row = b_row_all[cf:cf + 1, :]
            i_col = g[:, ci:ci + 1]
            i_row = g_t[ci:ci + 1, :]
            m = m_scr[u:u + 1, 0:1]
            hs = slice(h * HEAD_DIM, (h + 1) * HEAD_DIM)
            q = q_ref[:, hs]
            k = k_ref[:, hs]
            v = v_ref[:, hs]
            c_state = c_scr[u]
            n_state = n_scr[u:u + 1, :]

            dmat = jnp.where(valid, b_col - b_row + i_row, -jnp.inf)
            inter = b_col + m
            m_t = jnp.maximum(inter, jnp.max(dmat, axis=-1, keepdims=True))
            w_ts = jnp.exp(dmat - m_t)
            a_t = jnp.exp(inter - m_t)
            s = _dot_nt(q, k) * w_ts
            num = _dot(s.astype(BF16), v) + a_t * _dot_nt(q, c_state.astype(BF16))
            qf32 = q.astype(F32)
            den = jnp.sum(s, axis=-1, keepdims=True) + a_t * jnp.sum(qf32 * n_state, axis=-1, keepdims=True)
            h_ref[:, hs] = num / jnp.maximum(jnp.abs(den), jnp.exp(-m_t))

            b_last = b_row[:, last:last + 1]
            dec_col = b_last - b_col + i_col
            dec_row = b_last - b_row + i_row
            m_new = jnp.maximum(b_last + m, jnp.max(dec_row, axis=-1, keepdims=True))
            w_col = jnp.exp(dec_col - m_new)
            a_c = jnp.exp(b_last + m - m_new)
            vw_t = (v.astype(F32) * w_col).T.astype(BF16)
            c_scr[u] = a_c * c_state + _dot(vw_t, k)
            n_scr[u:u + 1, :] = a_c * n_state + jnp.sum(k.astype(F32) * w_col, axis=0, keepdims=True)
            m_scr[u:u + 1, :] = jnp.broadcast_to(m_new, (1, GATE_PAD))

    if want_state:
        @pl.when(step == nc - 1)
        def _():
            co_ref[...] = c_scr[...]
            no_ref[...] = n_scr[...]
            mo_ref[...] = m_scr[...]


def _mlstm(proj, gates, n_seq, seq_len, state0, want_state):
    t = proj.shape[0]
    nc = seq_len // CHUNK
    units = 2 * N_HEADS
    has_state = state0 is not None

    def fwd(c):
        return lambda s, k: (s * nc + k, c)

    def bwd(c):
        return lambda s, k: (s * nc + nc - 1 - k, c)

    tile = (CHUNK, D_MODEL)
    in_specs = [pl.BlockSpec(tile, fwd(0)), pl.BlockSpec(tile, fwd(1)), pl.BlockSpec(tile, fwd(2)),
                pl.BlockSpec(tile, bwd(0)), pl.BlockSpec(tile, bwd(1)), pl.BlockSpec(tile, bwd(2)),
                pl.BlockSpec((CHUNK, GATE_PAD), fwd(0)), pl.BlockSpec((CHUNK, GATE_PAD), bwd(0))]
    args = [proj] * 6 + [gates, gates]
    state_specs = [pl.BlockSpec((None, units, HEAD_DIM, HEAD_DIM), lambda s, k: (s, 0, 0, 0)),
                   pl.BlockSpec((None, units, HEAD_DIM), lambda s, k: (s, 0, 0)),
                   pl.BlockSpec((None, units, GATE_PAD), lambda s, k: (s, 0, 0))]
    state_shapes = [jax.ShapeDtypeStruct((n_seq, units, HEAD_DIM, HEAD_DIM), F32),
                    jax.ShapeDtypeStruct((n_seq, units, HEAD_DIM), F32),
                    jax.ShapeDtypeStruct((n_seq, units, GATE_PAD), F32)]
    if has_state:
        in_specs += state_specs
        args += list(state0)
    out_specs = [pl.BlockSpec(tile, fwd(0)), pl.BlockSpec(tile, bwd(0))]
    out_shape = [jax.ShapeDtypeStruct((t, D_MODEL), F32), jax.ShapeDtypeStruct((t, D_MODEL), F32)]
    if want_state:
        out_specs += state_specs
        out_shape += state_shapes
    return pl.pallas_call(
        functools.partial(_mlstm_kernel, has_state, want_state, nc),
        grid=(n_seq, nc),
        in_specs=in_specs,
        out_specs=out_specs,
        out_shape=out_shape,
        scratch_shapes=[pltpu.VMEM((units, HEAD_DIM, HEAD_DIM), F32),
                        pltpu.VMEM((units, HEAD_DIM), F32),
                        pltpu.VMEM((units, GATE_PAD), F32)],
        compiler_params=_params(2),
        name="mlstm",
    )(*args)


_CONV_ROWS = 64
_LANES = 128


def _tail_kernel(has_pos, tm, tiles_per_seq, alpha, *refs):
    refs = list(refs)
    hf_ref, hb_ref, so_ref, val_ref, sg_ref, sga_ref, sgb_ref, vp_ref, gp_ref, vn_ref, gn_ref, x_ref = refs[:12]
    pos = 12
    if has_pos:
        pos_ref = refs[pos]
        pos += 1
    (mod_ref, normw_ref, wa_ref, wc_ref, wo_ref, dww_ref, dwb_ref, clnw_ref, clnb_ref, ln1w_ref, ln1b_ref, wq_ref,
     x1_ref, u2_ref, q_ref, xpad, conv_scr, shifted) = refs[pos:]
    i = pl.program_id(0)

    hsum = hf_ref[...] + hb_ref[...]
    parts = []
    for h in range(N_HEADS):
        hh = hsum[:, h * HEAD_DIM:(h + 1) * HEAD_DIM]
        mu = jnp.mean(hh, axis=-1, keepdims=True)
        hc = hh - mu
        var = jnp.mean(hc * hc, axis=-1, keepdims=True)
        parts.append(hc * lax.rsqrt(var + LN_EPS))
    hn = jnp.concatenate(parts, axis=-1) * normw_ref[...]
    hg = (so_ref[...].astype(F32) * hn).astype(BF16)
    branch_a = _dot(hg, wa_ref[...])

    first = (i % tiles_per_seq) == 0
    last = (i % tiles_per_seq) == tiles_per_seq - 1
    keep_prev = jnp.where(first, 0.0, 1.0)
    keep_next = jnp.where(last, 0.0, 1.0)
    xpad[0:CONV_HALO, :] = vp_ref[...].astype(F32) * gp_ref[...].astype(F32) * keep_prev
    xpad[CONV_HALO:CONV_HALO + tm, :] = val_ref[...].astype(F32) * sg_ref[...].astype(F32)
    xpad[CONV_HALO + tm:2 * CONV_HALO + tm, :] = vn_ref[...].astype(F32) * gn_ref[...].astype(F32) * keep_next
    tap0 = CONV_HALO - CONV_WIDTH // 2

    n_shift_rows = tm + 2 * CONV_HALO - _F32_ROWS

    def col_body(c, carry):
        cs = pl.ds(pl.multiple_of(c * _LANES, _LANES), _LANES)
        for r in range(1, _F32_ROWS):
            shifted[r - 1, 0:n_shift_rows, :] = xpad[pl.ds(r, n_shift_rows), cs]
        for rb in range(tm // _CONV_ROWS):
            acc = jnp.broadcast_to(dwb_ref[:, cs], (_CONV_ROWS, _LANES))
            for k in range(CONV_WIDTH):
                tiles, r = divmod(tap0 + k, _F32_ROWS)
                start = tiles * _F32_ROWS + rb * _CONV_ROWS
                if r == 0:
                    src = xpad[pl.ds(start, _CONV_ROWS), cs]
                else:
                    src = shifted[r - 1, pl.ds(start, _CONV_ROWS), :]
                acc = acc + src * dww_ref[k:k + 1, cs]
            conv_scr[pl.ds(rb * _CONV_ROWS, _CONV_ROWS), cs] = acc
        return carry

    lax.fori_loop(0, D_MODEL // _LANES, col_body, 0)
    xc = _layer_norm(conv_scr[...], clnw_ref[...], clnb_ref[...])
    xc = (xc * _sigmoid(xc)).astype(BF16)
    branch_b = _dot(xc, wc_ref[...])

    merged = sga_ref[...].astype(F32) * branch_a + sgb_ref[...].astype(F32) * branch_b
    mix = _dot(merged.astype(BF16), wo_ref[...])

    mod = mod_ref[...]
    gate1 = mod[:, 2 * D_MODEL:3 * D_MODEL]
    shift2 = mod[:, 3 * D_MODEL:4 * D_MODEL]
    scale2 = mod[:, 4 * D_MODEL:5 * D_MODEL]
    x = x_ref[...]
    if has_pos:
        x = x + pos_ref[...]
    x1 = _layer_norm(alpha * x + gate1 * mix, ln1w_ref[...], ln1b_ref[...])
    x1_ref[...] = x1
    u2 = (x1 * (1.0 + scale2) + shift2).astype(BF16)
    u2_ref[...] = u2
    q_ref[...] = _dot(u2, wq_ref[...]).astype(BF16)


def _tail(hf, hb, proj, x, pos, mod, row_fn, seq_len, tm, alpha, lw):
    t = x.shape[0]
    has_pos = pos is not None
    tiles_per_seq = seq_len // tm
    hb_per_tile = tm // CONV_HALO
    n_halo = t // CONV_HALO
    big = (tm, D_MODEL)
    halo = (CONV_HALO, D_MODEL)

    def colspec(c):
        return pl.BlockSpec(big, lambda i: (i, c))

    def prev(c):
        return pl.BlockSpec(halo, lambda i: (jnp.maximum(i * hb_per_tile - 1, 0), c))

    def nxt(c):
        return pl.BlockSpec(halo, lambda i: (jnp.minimum((i + 1) * hb_per_tile, n_halo - 1), c))

    def const(shape):
        return pl.BlockSpec(shape, lambda i: (0,) * len(shape), pipeline_mode=pl.Buffered(1))

    in_specs = [colspec(0), colspec(0), colspec(3), colspec(4), colspec(5), colspec(6), colspec(7),
                prev(4), prev(5), nxt(4), nxt(5), colspec(0)]
    args = [hf, hb, proj, proj, proj, proj, proj, proj, proj, proj, proj, x]
    if has_pos:
        in_specs.append(pl.BlockSpec(big, lambda i: (i % tiles_per_seq, 0)))
        args.append(pos)
    in_specs += [pl.BlockSpec((None, 1, mod.shape[2]), lambda i: (row_fn(i), 0, 0)),
                 const((1, D_MODEL)), const((D_MODEL, D_MODEL)), const((D_MODEL, D_MODEL)), const((D_MODEL, D_MODEL)),
                 const((32, D_MODEL)), const((1, D_MODEL)), const((1, D_MODEL)), const((1, D_MODEL)),
                 const((1, D_MODEL)), const((1, D_MODEL)), const(lw["wq"].shape)]
    args += [mod, lw["norm_w"], lw["w_a"], lw["w_cout"], lw["w_out"], lw["dw_w"], lw["dw_b"], lw["cln_w"],
             lw["cln_b"], lw["ln1_w"], lw["ln1_b"], lw["wq"]]
    n_q = lw["wq"].shape[1]
    return pl.pallas_call(
        functools.partial(_tail_kernel, has_pos, tm, tiles_per_seq, alpha),
        grid=(t // tm,),
        in_specs=in_specs,
        out_specs=[pl.BlockSpec(big, lambda i: (i, 0)), pl.BlockSpec(big, lambda i: (i, 0)),
                   pl.BlockSpec((tm, n_q), lambda i: (i, 0))],
        out_shape=[jax.ShapeDtypeStruct((t, D_MODEL), F32), jax.ShapeDtypeStruct((t, D_MODEL), BF16),
                   jax.ShapeDtypeStruct((t, n_q), BF16)],
        scratch_shapes=[pltpu.VMEM((tm + 2 * CONV_HALO, D_MODEL), F32), pltpu.VMEM((tm, D_MODEL), F32),
                        pltpu.VMEM((_F32_ROWS - 1, tm + 2 * CONV_HALO, _LANES), F32)],
        compiler_params=_params(1),
        name="mixer_tail",
    )(*args)


_TOK_BLOCKS = 8
_LANE = 128
_ROUTE_TM = _TOK_BLOCKS * _LANE


def _merge_exchange_pairs(n):
    pairs = []
    t = (n - 1).bit_length()
    p = 1 << (t - 1)
    while p > 0:
        q, r, d = 1 << (t - 1), 0, p
        while d > 0:
            pairs.extend((i, i + d) for i in range(n - d) if (i & p) == r)
            d, q, r = q - p, q >> 1, p
        p >>= 1
    return tuple(pairs)


_SORT16 = _merge_exchange_pairs(PEER_TOPK)


def _sort_desc(vals):
    vals = list(vals)
    for i, j in _SORT16:
        vals[i], vals[j] = jnp.maximum(vals[i], vals[j]), jnp.minimum(vals[i], vals[j])
    return vals


def _bitonic_desc(vals):
    vals = list(vals)
    d = len(vals) // 2
    while d > 0:
        for i in range(len(vals)):
            if (i & d) == 0:
                vals[i], vals[i + d] = jnp.maximum(vals[i], vals[i + d]), jnp.minimum(vals[i], vals[i + d])
        d //= 2
    return vals


def _top_merge(a, b):
    n = len(a)
    return _bitonic_desc([jnp.maximum(a[i], b[n - 1 - i]) for i in range(n)])


def _top16(keys):
    if len(keys) == PEER_TOPK:
        return _sort_desc(keys)
    half = len(keys) // 2
    return _top_merge(_top16(keys[:half]), _top16(keys[half:]))


def _prefix_count(test, v):
    t8 = test(v[7])
    t4 = test(jnp.where(t8, v[11], v[3]))
    t2 = test(jnp.where(t8, jnp.where(t4, v[13], v[9]), jnp.where(t4, v[5], v[1])))
    lo = jnp.where(t4, jnp.where(t2, v[6], v[4]), jnp.where(t2, v[2], v[0]))
    hi = jnp.where(t4, jnp.where(t2, v[14], v[12]), jnp.where(t2, v[10], v[8]))
    t1 = test(jnp.where(t8, hi, lo))
    cnt = (jnp.where(t8, 8.0, 0.0) + jnp.where(t4, 4.0, 0.0)) + (jnp.where(t2, 2.0, 0.0) + jnp.where(t1, 1.0, 0.0))
    return jnp.where(test(v[15]), 16.0, cnt)


def _joint_top16(a, b):
    q0 = [a[p] + b[0] for p in range(16)]
    q1 = [a[p] + b[1] for p in range(8)]
    p0 = [a[0] + b[q] for q in range(8, 16)]
    m1 = _bitonic_desc(q1 + p0[::-1])
    m2 = _sort_desc([a[p] + b[q] for q, n in ((2, 5), (3, 4), (4, 3), (5, 2), (6, 2)) for p in range(n)])
    q7 = [a[0] + b[7], a[1] + b[7]]
    t2 = _bitonic_desc(m2[:14] + [jnp.maximum(m2[14], q7[1]), jnp.maximum(m2[15], q7[0])])
    return _top_merge(_top_merge(q0, m1), t2)


def _route_kernel(q_ref, skx_ref, rank2_ref, e2_ref, r_ref, a1_ref, km_rank2, km_e2, km_r, km_a1):
    nb = _TOK_BLOCKS
    keys = []
    for p in range(2):
        qp = jnp.concatenate([q_ref[a * _LANE:(a + 1) * _LANE, p * PEER_NKEYS:(p + 1) * PEER_NKEYS]
                              for a in range(nb)], axis=1)
        sp = _dot_nt(skx_ref[p], qp)
        keys.append([sp[k * nb:(k + 1) * nb, :] for k in range(PEER_NKEYS)])
    s1, s2 = keys
    a = _top16(s1)
    b = _top16(s2)
    top = _joint_top16(a, b)
    tau = top[PEER_TOPK - 1]
    zsum = jnp.ones_like(tau)
    for c in top[1:]:
        zsum = zsum + jnp.exp(c - top[0])
    inv_z = 1.0 / zsum
    b_asc = b[::-1]
    for k in range(PEER_NKEYS):
        rows = slice(k * nb, (k + 1) * nb)
        km_r[rows, :] = _prefix_count(lambda bq, x=s1[k]: x + bq >= tau, b)
        km_a1[rows, :] = jnp.exp(s1[k] - a[0]) * inv_z
        km_rank2[rows, :] = float(PEER_TOPK) - _prefix_count(lambda bq, x=s2[k]: x >= bq, b_asc)
        km_e2[rows, :] = jnp.exp(s2[k] - b[0])
    for blk in range(nb):
        cols = slice(blk * _LANE, (blk + 1) * _LANE)
        rows = pl.ds(blk, PEER_NKEYS, stride=nb)
        rank2_ref[:, cols] = km_rank2[rows, :].astype(BF16)
        e2_ref[:, cols] = km_e2[rows, :].astype(BF16)
        r_ref[:, cols] = km_r[rows, :]
        a1_ref[:, cols] = km_a1[rows, :]


def _route(q, skx):
    t = q.shape[0]
    tm = _ROUTE_TM
    assert t % tm == 0, (t, tm)
    out_blk = pl.BlockSpec((None, PEER_NKEYS, tm), lambda h, i: (h, 0, i))
    shape = (PEER_HEADS, PEER_NKEYS, t)
    km = pltpu.VMEM((PEER_NKEYS * _TOK_BLOCKS, _LANE), F32)
    return pl.pallas_call(
        _route_kernel,
        grid=(PEER_HEADS, t // tm),
        in_specs=[pl.BlockSpec((tm, 2 * PEER_NKEYS), lambda h, i: (i, h)),
                  pl.BlockSpec((None, 2, PEER_NKEYS * _TOK_BLOCKS, _TOK_BLOCKS * PEER_NKEYS),
                               lambda h, i: (h, 0, 0, 0))],
        out_specs=[out_blk, out_blk, out_blk, out_blk],
        out_shape=[jax.ShapeDtypeStruct(shape, BF16), jax.ShapeDtypeStruct(shape, BF16),
                   jax.ShapeDtypeStruct(shape, F32), jax.ShapeDtypeStruct(shape, F32)],
        scratch_shapes=[km, km, km, km],
        compiler_params=_params(2),
        name="peer_route",
    )(q, skx)


_GELU_C = math.sqrt(2.0 / math.pi)
_BF16_ROWS = 16
_F32_ROWS = 8
_MXU_N = 256


def _gelu_tanh(x):
    neg2z = x * (x * x * (-2.0 * _GELU_C * 0.044715) + (-2.0 * _GELU_C))
    return x / (1.0 + jnp.exp(neg2z))


def _experts_kernel(ib, n_eblk, alpha, u_ref, ut_ref, vt_ref, rank2_ref, e2_ref, r_ref, a1_ref, x1_ref, mod_ref,
                    lnw_ref, lnb_ref, y_ref, acc_scr, at_scr, p_scr, uT_scr):
    s = pl.program_id(1)

    @pl.when(s == 0)
    def _():
        acc_scr[...] = jnp.zeros_like(acc_scr)
        uT_scr[...] = u_ref[...].astype(F32).T.astype(BF16)

    tm = u_ref.shape[0]
    key_tiles = PEER_NKEYS // _BF16_ROWS
    rows_per_chunk = _MXU_N // PEER_NKEYS
    n_chunks = ib // rows_per_chunk

    def tile_rows(ref, h, ii, cols):
        group = ref[h, pl.ds(pl.multiple_of(s * ib + (ii // _F32_ROWS) * _F32_ROWS, _F32_ROWS), _F32_ROWS), cols]
        row = group[ii % _F32_ROWS:ii % _F32_ROWS + 1, :]
        tile = jnp.broadcast_to(row, (_BF16_ROWS, _MXU_N)).astype(BF16)
        return jnp.concatenate([tile] * key_tiles, axis=0)

    def weights(ii):
        rows = slice(ii * PEER_NKEYS, (ii + 1) * PEER_NKEYS)
        for c in range(tm // _MXU_N):
            cols = slice(c * _MXU_N, (c + 1) * _MXU_N)
            act = _gelu_tanh(at_scr[rows, cols].astype(BF16))
            w = None
            for h in range(PEER_HEADS):
                r_b = tile_rows(r_ref, h, ii, cols)
                a_b = tile_rows(a1_ref, h, ii, cols)
                term = jnp.where(rank2_ref[h, :, cols] < r_b, e2_ref[h, :, cols] * a_b, jnp.zeros((), BF16))
                w = term if w is None else w + term
            p_scr[rows, cols] = w * act

    for ch in range(n_chunks):
        erows = slice(ch * _MXU_N, (ch + 1) * _MXU_N)
        at_scr[erows, :] = _dot(ut_ref[erows, :], uT_scr[...])
    for ch in range(n_chunks):
        erows = slice(ch * _MXU_N, (ch + 1) * _MXU_N)
        for ii in range(ch * rows_per_chunk, (ch + 1) * rows_per_chunk):
            weights(ii)
        acc_scr[...] += _dot(vt_ref[:, erows], p_scr[erows, :])

    @pl.when(s == n_eblk - 1)
    def _():
        mod = mod_ref[...]
        gate2 = mod[:, 5 * D_MODEL:6 * D_MODEL]
        y = alpha * x1_ref[...] + gate2 * acc_scr[...].T
        y_ref[...] = _layer_norm(y, lnw_ref[...], lnb_ref[...])


def _experts(u2, u_tab, vt_tab, rank2, e2, r, a1, x1, mod, row_fn, lnw, lnb, tm, eb, alpha):
    t = u2.shape[0]
    n_exp = u_tab.shape[0]
    ib = eb // PEER_NKEYS
    n_eblk = n_exp // eb
    full = pl.BlockSpec((PEER_HEADS, PEER_NKEYS, tm), lambda i, s: (0, 0, i))
    tok = pl.BlockSpec((tm, D_MODEL), lambda i, s: (i, 0))
    vec = pl.BlockSpec((1, D_MODEL), lambda i, s: (0, 0))
    return pl.pallas_call(
        functools.partial(_experts_kernel, ib, n_eblk, alpha),
        grid=(t // tm, n_eblk),
        in_specs=[tok,
                  pl.BlockSpec((eb, D_MODEL), lambda i, s: (s, 0)),
                  pl.BlockSpec((D_MODEL, eb), lambda i, s: (0, s)),
                  full, full, full, full, tok,
                  pl.BlockSpec((None, 1, mod.shape[2]), lambda i, s: (row_fn(i), 0, 0)),
                  vec, vec],
        out_specs=tok,
        out_shape=jax.ShapeDtypeStruct((t, D_MODEL), F32),
        scratch_shapes=[pltpu.VMEM((D_MODEL, tm), F32), pltpu.VMEM((eb, tm), F32),
                        pltpu.VMEM((eb, tm), BF16), pltpu.VMEM((D_MODEL, tm), BF16)],
        compiler_params=_params(2),
        name="peer_experts",
    )(u2, u_tab, vt_tab, rank2, e2, r, a1, x1, mod, lnw, lnb)


def _grid_pos_embed(n_tokens):
    rows = n_tokens // GRID_W
    quarter = D_MODEL // 4
    freqs = jnp.exp(-math.log(POS_BASE) * jnp.arange(quarter, dtype=F32) / quarter)
    r = jnp.arange(rows, dtype=F32)[:, None] * freqs
    cl = jnp.arange(GRID_W, dtype=F32)[:, None] * freqs
    er = jnp.concatenate([jnp.sin(r), jnp.cos(r)], -1)
    ec = jnp.concatenate([jnp.sin(cl), jnp.cos(cl)], -1)
    emb = jnp.concatenate([jnp.broadcast_to(er[:, None, :], (rows, GRID_W, D_MODEL // 2)),
                           jnp.broadcast_to(ec[None, :, :], (rows, GRID_W, D_MODEL // 2))], -1)
    return emb.reshape(rows * GRID_W, D_MODEL)


def _pick_tile(seq_len, n_tokens, target):
    tm = min(target, seq_len)
    while seq_len % tm or n_tokens % tm:
        tm //= 2
    return tm


def _block(x, pos, mod, n_seq, seq_len, tm_in, row_of_token_tile, state0, want_state, lw, alpha):
    t = x.shape[0]
    proj, gates = _inproj(x, pos, mod, row_of_token_tile(tm_in), lw["w_main"], lw["b_main"], lw["wg_hi"],
                          lw["wg_lo"], lw["b_gate"], tm_in)
    res = _mlstm(proj, gates, n_seq, seq_len, state0, want_state)
    hf, hb = res[0], res[1]
    tm_tail = _pick_tile(seq_len, t, 256)
    x1, u2, q = _tail(hf, hb, proj, x, pos, mod, row_of_token_tile(tm_tail), seq_len, tm_tail, alpha, lw)
    tm_p = _pick_tile(seq_len, t, 512)
    rank2, e2, r, a1 = _route(q, lw["skx"])
    y = _experts(u2, lw["peer_u"], lw["peer_vt"], rank2, e2, r, a1, x1, mod, row_of_token_tile(tm_p),
                 lw["ln2_w"], lw["ln2_b"], tm_p, 2048, alpha)
    return y, res[2:]


def kernel(x_prompt, x_sample, state_C, state_n, state_m, c, c_ctx, w_in, b_in, mlstm_norm_w, w_a, conv_dw_w,
           conv_dw_b, conv_ln_w, conv_ln_b, w_conv_out, w_out, w_mod, b_mod, ln1_w, ln1_b, ln2_w, ln2_b,
           peer_w_query, peer_subkeys, peer_u, peer_v):
    depth = w_in.shape[0]
    alpha = (2.0 * depth) ** 0.25
    bsz, seq, _ = x_prompt.shape
    dbsz, dseq, _ = x_sample.shape
    units = 2 * N_HEADS
    gate_off = 4 * D_MODEL

    ctx = x_prompt.reshape(bsz * seq, D_MODEL)
    lat = x_sample.reshape(dbsz * dseq, D_MODEL)
    pos = _grid_pos_embed(dseq)
    n_rows = 1 + dbsz
    pad_rows = (-n_rows) % 8
    cvec = jnp.concatenate([c_ctx[None, :], c, jnp.zeros((pad_rows, D_MODEL), F32)], axis=0)

    new_c, new_n, new_m = [], [], []
    for l in range(depth):
        vec = lambda a: a[l].reshape(1, -1)
        w_l = w_in[l]
        b_l = b_in[l]
        wg = jnp.pad(w_l[:, gate_off:gate_off + N_GATES], ((0, 0), (0, GATE_PAD - N_GATES)))
        wg_hi = wg.astype(BF16)
        lw = {
            "w_main": jnp.concatenate([w_l[:, :gate_off], w_l[:, gate_off + N_GATES:]], axis=1).astype(BF16),
            "b_main": jnp.concatenate([b_l[:gate_off], b_l[gate_off + N_GATES:]]).reshape(1, -1),
            "wg_hi": wg_hi,
            "wg_lo": (wg - wg_hi.astype(F32)).astype(BF16),
            "b_gate": jnp.pad(b_l[gate_off:gate_off + N_GATES], (0, GATE_PAD - N_GATES)).reshape(1, -1),
            "norm_w": vec(mlstm_norm_w), "w_a": w_a[l].astype(BF16), "w_cout": w_conv_out[l].astype(BF16),
            "w_out": w_out[l].astype(BF16),
            "dw_w": jnp.pad(conv_dw_w[l], ((0, 32 - CONV_WIDTH), (0, 0))), "dw_b": vec(conv_dw_b),
            "cln_w": vec(conv_ln_w), "cln_b": vec(conv_ln_b), "ln1_w": vec(ln1_w), "ln1_b": vec(ln1_b),
            "ln2_w": vec(ln2_w), "ln2_b": vec(ln2_b),
            "wq": peer_w_query[l].astype(BF16),
            "skx": jnp.einsum("hpkd,ab->hpkabd", peer_subkeys[l].astype(BF16), jnp.eye(_TOK_BLOCKS, dtype=BF16))
            .reshape(PEER_HEADS, 2, PEER_NKEYS * _TOK_BLOCKS, _TOK_BLOCKS * PEER_NKEYS),
            "peer_u": peer_u[l].astype(BF16),
            "peer_vt": peer_v[l].T.astype(BF16),
        }
        mod = _modulation(cvec, w_mod[l], b_mod[l]).reshape(n_rows + pad_rows, 1, 6 * D_MODEL)

        ctx, (c_fin, n_fin, m_fin) = _block(ctx, None, mod, bsz, seq, _pick_tile(bsz * seq, bsz * seq, 1024),
                                             lambda tm: (lambda i: 0), None, True, lw, alpha)
        new_c.append(c_fin.reshape(bsz, 2, N_HEADS, HEAD_DIM, HEAD_DIM))
        new_n.append(n_fin.reshape(bsz, 2, N_HEADS, HEAD_DIM))
        new_m.append(m_fin[:, :, 0].reshape(bsz, 2, N_HEADS))

        state0 = (state_C[:, l].reshape(dbsz, units, HEAD_DIM, HEAD_DIM),
                  state_n[:, l].reshape(dbsz, units, HEAD_DIM),
                  jnp.broadcast_to(state_m[:, l].reshape(dbsz, units, 1), (dbsz, units, GATE_PAD)))
        lat, _ = _block(lat, pos, mod, dbsz, dseq, _pick_tile(dseq, dbsz * dseq, 1024),
                        lambda tm: (lambda i: 1 + (i * tm) // dseq), state0, False, lw, alpha)

    return (ctx.reshape(bsz, seq, D_MODEL), lat.reshape(dbsz, dseq, D_MODEL),
            jnp.stack(new_c, axis=1), jnp.stack(new_n, axis=1), jnp.stack(new_m, axis=1))
```

```python
import functools
import math

import jax
import jax.numpy as jnp
from jax import lax
from jax.experimental import pallas as pl
from jax.experimental.pallas import tpu as pltpu

F32 = jnp.float32
BF16 = jnp.bfloat16

D_MODEL = 1024
N_HEADS = 4
HEAD_DIM = 256
CHUNK = 128
CONV_WIDTH = 31
CONV_HALO = 16
N_GATES = 16
GATE_PAD = 128
GRID_W = 64
POS_BASE = 10000.0
LN_EPS = 1e-6
PEER_HEADS = 8
PEER_NKEYS = 128
PEER_TOPK = 16
V7X_VMEM_LIMIT = 56 * 1024 * 1024


def _params(n_axes):
    return pltpu.CompilerParams(dimension_semantics=("arbitrary",) * n_axes,
                                vmem_limit_bytes=V7X_VMEM_LIMIT)


def _sigmoid(x):
    return 1.0 / (1.0 + jnp.exp(-x))


def _log_sigmoid(x):
    return jnp.minimum(x, 0.0) - jnp.log(1.0 + jnp.exp(-jnp.abs(x)))


def _dot(a, b):
    return jnp.dot(a, b, preferred_element_type=F32)


def _dot_nt(a, b):
    return lax.dot_general(a, b, (((1,), (1,)), ((), ())), preferred_element_type=F32)


def _split2(x):
    hi = x.astype(BF16)
    lo = (x - hi.astype(F32)).astype(BF16)
    return hi, lo


def _split3(x):
    a = x.astype(BF16)
    r = x - a.astype(F32)
    b = r.astype(BF16)
    c = (r - b.astype(F32)).astype(BF16)
    return a, b, c


def _layer_norm(x, w, b):
    mu = jnp.mean(x, axis=-1, keepdims=True)
    xc = x - mu
    var = jnp.mean(xc * xc, axis=-1, keepdims=True)
    return xc * lax.rsqrt(var + LN_EPS) * w + b


def _mod_kernel(c_ref, w_ref, b_ref, o_ref):
    c = c_ref[...]
    s = c * _sigmoid(c)
    sh, sl = _split2(s)
    wh, wl = _split2(w_ref[...])
    o_ref[...] = _dot(sh, wh) + _dot(sl, wh) + _dot(sh, wl) + b_ref[...]


def _modulation(cvec, w_mod, b_mod):
    rows = cvec.shape[0]
    tn = 1536
    n = w_mod.shape[1]
    return pl.pallas_call(
        _mod_kernel,
        grid=(n // tn,),
        in_specs=[pl.BlockSpec((rows, D_MODEL), lambda j: (0, 0)),
                  pl.BlockSpec((D_MODEL, tn), lambda j: (0, j)),
                  pl.BlockSpec((1, tn), lambda j: (0, j))],
        out_specs=pl.BlockSpec((rows, tn), lambda j: (0, j)),
        out_shape=jax.ShapeDtypeStruct((rows, n), F32),
        compiler_params=_params(1),
        name="modulation",
    )(cvec, w_mod, b_mod.reshape(1, n))


_STEP_K = 1
_FEATURE_MAJOR_STEPS = (0, 2, 3)
_N_TOKEN_MAJOR = 5


def _lane_tile(block, n_lanes):
    return jnp.concatenate([block] * (n_lanes // block.shape[1]), axis=1)


def _inproj_kernel(has_pos, *refs):
    if has_pos:
        (x_ref, pos_ref, mod_ref, w_ref, b_ref, bt_ref, wgh_ref, wgl_ref, bg_ref, wgth_ref, wgtl_ref, bgt_ref,
         proj_ref, projt_ref, gates_ref, gatest_ref, u_scr) = refs
    else:
        (x_ref, mod_ref, w_ref, b_ref, bt_ref, wgh_ref, wgl_ref, bg_ref, wgth_ref, wgtl_ref, bgt_ref,
         proj_ref, projt_ref, gates_ref, gatest_ref, u_scr) = refs
    j = pl.program_id(1)
    tm = u_scr.shape[0]

    @pl.when(j == 0)
    def _():
        x = x_ref[...]
        if has_pos:
            x = x + pos_ref[...]
        mod = mod_ref[...]
        u = x * (1.0 + mod[:, D_MODEL:2 * D_MODEL]) + mod[:, 0:D_MODEL]
        uh, ul = _split2(u)
        u_scr[...] = uh
        wgh = wgh_ref[...]
        gates_ref[...] = _dot(uh, wgh) + _dot(ul, wgh) + _dot(uh, wgl_ref[...]) + bg_ref[...]
        wgth = wgth_ref[...]
        gatest_ref[...] = (_dot_nt(wgth, uh) + _dot_nt(wgth, ul) + _dot_nt(wgtl_ref[...], uh)
                           + _lane_tile(bgt_ref[...], tm))

    is_t = jnp.logical_or(j == 0, jnp.logical_or(j == 2, j == 3))

    @pl.when(is_t)
    def _():
        acc = _dot_nt(w_ref[...], u_scr[...]) + _lane_tile(bt_ref[...], tm)

        @pl.when(j == 3)
        def _():
            projt_ref[...] = _sigmoid(acc).astype(BF16)

        @pl.when(j != 3)
        def _():
            projt_ref[...] = acc.astype(BF16)

    @pl.when(jnp.logical_not(is_t))
    def _():
        acc = _dot(u_scr[...], w_ref[...]) + b_ref[...]

        @pl.when(j == _STEP_K)
        def _():
            proj_ref[...] = (acc * (HEAD_DIM ** -0.5)).astype(BF16)

        @pl.when(j == 4)
        def _():
            proj_ref[...] = acc.astype(BF16)

        @pl.when(j >= 5)
        def _():
            proj_ref[...] = _sigmoid(acc).astype(BF16)


def _inproj(x, pos, mod, row_fn, lw, tm):
    t = x.shape[0]
    n_steps = lw["w_main"].shape[1] // D_MODEL
    has_pos = pos is not None
    in_specs = [pl.BlockSpec((tm, D_MODEL), lambda i, j: (i, 0))]
    args = [x]
    if has_pos:
        pos_blocks = pos.shape[0] // tm
        in_specs.append(pl.BlockSpec((tm, D_MODEL), lambda i, j: (i % pos_blocks, 0)))
        args.append(pos)

    def token_major_block(j):
        return jnp.maximum(j - 3, 0)

    def feature_major_block(j):
        return jnp.clip(j - 1, 0, 2)

    in_specs += [
        pl.BlockSpec((None, 1, mod.shape[2]), lambda i, j: (row_fn(i), 0, 0)),
        pl.BlockSpec((D_MODEL, D_MODEL), lambda i, j: (0, j)),
        pl.BlockSpec((1, D_MODEL), lambda i, j: (0, j)),
        pl.BlockSpec((None, D_MODEL, _LANES), lambda i, j: (feature_major_block(j), 0, 0)),
        pl.BlockSpec((D_MODEL, GATE_PAD), lambda i, j: (0, 0)),
        pl.BlockSpec((D_MODEL, GATE_PAD), lambda i, j: (0, 0)),
        pl.BlockSpec((1, GATE_PAD), lambda i, j: (0, 0)),
        pl.BlockSpec((GATE_PAD, D_MODEL), lambda i, j: (0, 0)),
        pl.BlockSpec((GATE_PAD, D_MODEL), lambda i, j: (0, 0)),
        pl.BlockSpec((GATE_PAD, _LANES), lambda i, j: (0, 0)),
    ]
    args += [mod, lw["w_main"], lw["b_main"], lw["b_feat"], lw["wg_hi"], lw["wg_lo"], lw["b_gate"],
             lw["wgt_hi"], lw["wgt_lo"], lw["b_gate_t"]]
    return pl.pallas_call(
        functools.partial(_inproj_kernel, has_pos),
        grid=(t // tm, n_steps),
        in_specs=in_specs,
        out_specs=[pl.BlockSpec((tm, D_MODEL), lambda i, j: (i, token_major_block(j))),
                   pl.BlockSpec((D_MODEL, tm), lambda i, j: (feature_major_block(j), i)),
                   pl.BlockSpec((tm, GATE_PAD), lambda i, j: (i, 0)),
                   pl.BlockSpec((GATE_PAD, tm), lambda i, j: (0, i))],
        out_shape=[jax.ShapeDtypeStruct((t, _N_TOKEN_MAJOR * D_MODEL), BF16),
                   jax.ShapeDtypeStruct((len(_FEATURE_MAJOR_STEPS) * D_MODEL, t), BF16),
                   jax.ShapeDtypeStruct((t, GATE_PAD), F32),
                   jax.ShapeDtypeStruct((GATE_PAD, t), F32)],
        scratch_shapes=[pltpu.VMEM((tm, D_MODEL), BF16)],
        compiler_params=_params(2),
        name="inproj",
    )(*args)


def _mlstm_kernel(has_state, want_state, nc, *refs):
    refs = list(refs)
    qf, kf, vf, qb, kb, vb, gf, gb, gtf, gtb = refs[:10]
    pos = 10
    if has_state:
        c0_ref, n0_ref, m0_ref = refs[pos:pos + 3]
        pos += 3
    hf_ref, hb_ref = refs[pos:pos + 2]
    pos += 2
    if want_state:
        co_ref, no_ref, mo_ref = refs[pos:pos + 3]
        pos += 3
    c_scr, n_scr, m_scr = refs[pos:pos + 3]
    step = pl.program_id(1)

    @pl.when(step == 0)
    def _():
        if has_state:
            c_scr[...] = c0_ref[...]
            n_scr[...] = n0_ref[...]
            m_scr[...] = m0_ref[...]
        else:
            c_scr[...] = jnp.zeros_like(c_scr)
            n_scr[...] = jnp.zeros_like(n_scr)
            m_scr[...] = jnp.zeros_like(m_scr)

    row = lax.broadcasted_iota(jnp.int32, (CHUNK, CHUNK), 0)
    col = lax.broadcasted_iota(jnp.int32, (CHUNK, CHUNK), 1)

    for d, (qt_ref, k_ref, vt_ref, g_ref, gt_ref, h_ref) in enumerate(
            ((qf, kf, vf, gf, gtf, hf_ref), (qb, kb, vb, gb, gtb, hb_ref))):
        visible = (row <= col) if d == 0 else (row >= col)
        tri_t = jnp.where(visible, 1.0, 0.0).astype(BF16)
        tri = jnp.where((col <= row) if d == 0 else (col >= row), 1.0, 0.0).astype(BF16)
        g = g_ref[...]
        g_t = gt_ref[...]
        l1, l2, l3 = _split3(_log_sigmoid(g))
        b_col_all = _dot(tri, l1) + _dot(tri, l2) + _dot(tri, l3)
        t1, t2, t3 = _split3(_log_sigmoid(g_t))
        b_row_all = _dot(t1, tri_t) + _dot(t2, tri_t) + _dot(t3, tri_t)
        last = CHUNK - 1 if d == 0 else 0
        for h in range(N_HEADS):
            u = d * N_HEADS + h
            ci = d * 2 * N_HEADS + h
            cf = ci + N_HEADS
            b_row = b_row_all[cf:cf + 1, :]
            i_row = g_t[ci:ci + 1, :]
            src_col = g[:, ci:ci + 1] - b_col_all[:, cf:cf + 1]
            m = m_scr[u:u + 1, 0:1]
            hs = slice(h * HEAD_DIM, (h + 1) * HEAD_DIM)
            q_t = qt_ref[hs, :]
            k = k_ref[:, hs]
            v_t = vt_ref[hs, :]
            c_state = c_scr[u]
            n_state = n_scr[u:u + 1, :]

            dmat_t = jnp.where(visible, src_col + b_row, -jnp.inf)
            inter = b_row + m
            m_t = jnp.maximum(inter, jnp.max(dmat_t, axis=0, keepdims=True))
            w_st = jnp.exp(dmat_t - m_t)
            a_t = jnp.exp(inter - m_t)
            s_t = _dot(k, q_t) * w_st
            num_t = _dot(v_t, s_t.astype(BF16)) + a_t * _dot(c_state.astype(BF16), q_t)
            n_rows = jnp.broadcast_to(n_state, (_F32_ROWS, HEAD_DIM)).astype(BF16)
            den = jnp.sum(s_t, axis=0, keepdims=True) + a_t * _dot(n_rows, q_t)[0:1, :]
            h_ref[hs, :] = num_t / jnp.maximum(jnp.abs(den), jnp.exp(-m_t))

            b_last = b_row[:, last:last + 1]
            dec_row = b_last - b_row + i_row
            m_new = jnp.maximum(b_last + m, jnp.max(dec_row, axis=-1, keepdims=True))
            w_row = jnp.exp(dec_row - m_new)
            a_c = jnp.exp(b_last + m - m_new)
            vw_t = (v_t.astype(F32) * w_row).astype(BF16)
            c_scr[u] = a_c * c_state + _dot(vw_t, k)
            w_rows = jnp.broadcast_to(w_row, (_F32_ROWS, CHUNK)).astype(BF16)
            n_scr[u:u + 1, :] = a_c * n_state + _dot(w_rows, k)[0:1, :]
            m_scr[u:u + 1, :] = jnp.broadcast_to(m_new, (1, GATE_PAD))

    if want_state:
        @pl.when(step == nc - 1)
        def _():
            co_ref[...] = c_scr[...]
            no_ref[...] = n_scr[...]
            mo_ref[...] = m_scr[...]


def _mlstm(proj, projt, gates, gatest, n_seq, seq_len, state0, want_state):
    t = proj.shape[0]
    nc = seq_len // CHUNK
    units = 2 * N_HEADS
    has_state = state0 is not None

    def fwd(c):
        return lambda s, k: (s * nc + k, c)

    def bwd(c):
        return lambda s, k: (s * nc + nc - 1 - k, c)

    def fwd_t(r):
        return lambda s, k: (r, s * nc + k)

    def bwd_t(r):
        return lambda s, k: (r, s * nc + nc - 1 - k)

    tile = (CHUNK, D_MODEL)
    tile_t = (D_MODEL, CHUNK)
    in_specs = [pl.BlockSpec(tile_t, fwd_t(0)), pl.BlockSpec(tile, fwd(0)), pl.BlockSpec(tile_t, fwd_t(1)),
                pl.BlockSpec(tile_t, bwd_t(0)), pl.BlockSpec(tile, bwd(0)), pl.BlockSpec(tile_t, bwd_t(1)),
                pl.BlockSpec((CHUNK, GATE_PAD), fwd(0)), pl.BlockSpec((CHUNK, GATE_PAD), bwd(0)),
                pl.BlockSpec((GATE_PAD, CHUNK), fwd_t(0)), pl.BlockSpec((GATE_PAD, CHUNK), bwd_t(0))]
    args = [projt, proj, projt, projt, proj, projt, gates, gates, gatest, gatest]
    state_specs = [pl.BlockSpec((None, units, HEAD_DIM, HEAD_DIM), lambda s, k: (s, 0, 0, 0)),
                   pl.BlockSpec((None, units, HEAD_DIM), lambda s, k: (s, 0, 0)),
                   pl.BlockSpec((None, units, GATE_PAD), lambda s, k: (s, 0, 0))]
    state_shapes = [jax.ShapeDtypeStruct((n_seq, units, HEAD_DIM, HEAD_DIM), F32),
                    jax.ShapeDtypeStruct((n_seq, units, HEAD_DIM), F32),
                    jax.ShapeDtypeStruct((n_seq, units, GATE_PAD), F32)]
    if has_state:
        in_specs += state_specs
        args += list(state0)
    out_specs = [pl.BlockSpec(tile_t, fwd_t(0)), pl.BlockSpec(tile_t, bwd_t(0))]
    out_shape = [jax.ShapeDtypeStruct((D_MODEL, t), F32), jax.ShapeDtypeStruct((D_MODEL, t), F32)]
    if want_state:
        out_specs += state_specs
        out_shape += state_shapes
    return pl.pallas_call(
        functools.partial(_mlstm_kernel, has_state, want_state, nc),
        grid=(n_seq, nc),
        in_specs=in_specs,
        out_specs=out_specs,
        out_shape=out_shape,
        scratch_shapes=[pltpu.VMEM((units, HEAD_DIM, HEAD_DIM), F32),
                        pltpu.VMEM((units, HEAD_DIM), F32),
                        pltpu.VMEM((units, GATE_PAD), F32)],
        compiler_params=_params(2),
        name="mlstm",
    )(*args)


_CONV_ROWS = 64
_LANES = 128


def _tail_kernel(has_pos, tm, tiles_per_seq, alpha, *refs):
    refs = list(refs)
    hf_ref, hb_ref, so_ref, val_ref, sg_ref, sga_ref, sgb_ref, vp_ref, gp_ref, vn_ref, gn_ref, x_ref = refs[:12]
    pos = 12
    if has_pos:
        pos_ref = refs[pos]
        pos += 1
    (mod_ref, normw_ref, wa_ref, wc_ref, wo_ref, dww_ref, dwb_ref, clnw_ref, clnb_ref, ln1w_ref, ln1b_ref, wq_ref,
     x1_ref, u2_ref, q_ref, xpad, conv_scr, shifted) = refs[pos:]
    i = pl.program_id(0)

    hsum = hf_ref[...] + hb_ref[...]
    parts = []
    for h in range(N_HEADS):
        hh = hsum[h * HEAD_DIM:(h + 1) * HEAD_DIM, :]
        mu = jnp.mean(hh, axis=0, keepdims=True)
        hc = hh - mu
        var = jnp.mean(hc * hc, axis=0, keepdims=True)
        parts.append(hc * lax.rsqrt(var + LN_EPS))
    hn = jnp.concatenate(parts, axis=0) * _lane_tile(normw_ref[...], tm)
    hg = (so_ref[...].astype(F32) * hn).T.astype(BF16)
    branch_a = _dot(hg, wa_ref[...])

    first = (i % tiles_per_seq) == 0
    last = (i % tiles_per_seq) == tiles_per_seq - 1
    keep_prev = jnp.where(first, 0.0, 1.0)
    keep_next = jnp.where(last, 0.0, 1.0)
    xpad[0:CONV_HALO, :] = vp_ref[...].astype(F32) * gp_ref[...].astype(F32) * keep_prev
    xpad[CONV_HALO:CONV_HALO + tm, :] = val_ref[...].astype(F32) * sg_ref[...].astype(F32)
    xpad[CONV_HALO + tm:2 * CONV_HALO + tm, :] = vn_ref[...].astype(F32) * gn_ref[...].astype(F32) * keep_next
    tap0 = CONV_HALO - CONV_WIDTH // 2

    n_shift_rows = tm + 2 * CONV_HALO - _F32_ROWS

    def col_body(c, carry):
        cs = pl.ds(pl.multiple_of(c * _LANES, _LANES), _LANES)
        for r in range(1, _F32_ROWS):
            shifted[r - 1, 0:n_shift_rows, :] = xpad[pl.ds(r, n_shift_rows), cs]
        for rb in range(tm // _CONV_ROWS):
            acc = jnp.broadcast_to(dwb_ref[:, cs], (_CONV_ROWS, _LANES))
            for k in range(CONV_WIDTH):
                tiles, r = divmod(tap0 + k, _F32_ROWS)
                start = tiles * _F32_ROWS + rb * _CONV_ROWS
                if r == 0:
                    src = xpad[pl.ds(start, _CONV_ROWS), cs]
                else:
                    src = shifted[r - 1, pl.ds(start, _CONV_ROWS), :]
                acc = acc + src * dww_ref[k:k + 1, cs]
            conv_scr[pl.ds(rb * _CONV_ROWS, _CONV_ROWS), cs] = acc
        return carry

    lax.fori_loop(0, D_MODEL // _LANES, col_body, 0)
    xc = _layer_norm(conv_scr[...], clnw_ref[...], clnb_ref[...])
    xc = (xc * _sigmoid(xc)).astype(BF16)
    branch_b = _dot(xc, wc_ref[...])

    merged = sga_ref[...].astype(F32) * branch_a + sgb_ref[...].astype(F32) * branch_b
    mix = _dot(merged.astype(BF16), wo_ref[...])

    mod = mod_ref[...]
    gate1 = mod[:, 2 * D_MODEL:3 * D_MODEL]
    shift2 = mod[:, 3 * D_MODEL:4 * D_MODEL]
    scale2 = mod[:, 4 * D_MODEL:5 * D_MODEL]
    x = x_ref[...]
    if has_pos:
        x = x + pos_ref[...]
    x1 = _layer_norm(alpha * x + gate1 * mix, ln1w_ref[...], ln1b_ref[...])
    x1_ref[...] = x1
    u2 = (x1 * (1.0 + scale2) + shift2).astype(BF16)
    u2_ref[...] = u2
    q_ref[...] = _dot(u2, wq_ref[...]).astype(BF16)


def _tail(hf, hb, proj, projt, x, pos, mod, row_fn, seq_len, tm, alpha, lw):
    t = x.shape[0]
    has_pos = pos is not None
    tiles_per_seq = seq_len // tm
    hb_per_tile = tm // CONV_HALO
    n_halo = t // CONV_HALO
    big = (tm, D_MODEL)
    halo = (CONV_HALO, D_MODEL)

    def colspec(c):
        return pl.BlockSpec(big, lambda i: (i, c))

    def prev(c):
        return pl.BlockSpec(halo, lambda i: (jnp.maximum(i * hb_per_tile - 1, 0), c))

    def nxt(c):
        return pl.BlockSpec(halo, lambda i: (jnp.minimum((i + 1) * hb_per_tile, n_halo - 1), c))

    def const(shape):
        return pl.BlockSpec(shape, lambda i: (0,) * len(shape), pipeline_mode=pl.Buffered(1))

    def rowspec(r):
        return pl.BlockSpec((D_MODEL, tm), lambda i: (r, i))

    in_specs = [rowspec(0), rowspec(0), rowspec(2), colspec(1), colspec(2), colspec(3), colspec(4),
                prev(1), prev(2), nxt(1), nxt(2), colspec(0)]
    args = [hf, hb, projt, proj, proj, proj, proj, proj, proj, proj, proj, x]
    if has_pos:
        in_specs.append(pl.BlockSpec(big, lambda i: (i % tiles_per_seq, 0)))
        args.append(pos)
    in_specs += [pl.BlockSpec((None, 1, mod.shape[2]), lambda i: (row_fn(i), 0, 0)),
                 const((D_MODEL, _LANES)), const((D_MODEL, D_MODEL)), const((D_MODEL, D_MODEL)),
                 const((D_MODEL, D_MODEL)),
                 const((32, D_MODEL)), const((1, D_MODEL)), const((1, D_MODEL)), const((1, D_MODEL)),
                 const((1, D_MODEL)), const((1, D_MODEL)), const(lw["wq"].shape)]
    args += [mod, lw["norm_w"], lw["w_a"], lw["w_cout"], lw["w_out"], lw["dw_w"], lw["dw_b"], lw["cln_w"],
             lw["cln_b"], lw["ln1_w"], lw["ln1_b"], lw["wq"]]
    n_q = lw["wq"].shape[1]
    return pl.pallas_call(
        functools.partial(_tail_kernel, has_pos, tm, tiles_per_seq, alpha),
        grid=(t // tm,),
        in_specs=in_specs,
        out_specs=[pl.BlockSpec(big, lambda i: (i, 0)), pl.BlockSpec(big, lambda i: (i, 0)),
                   pl.BlockSpec((tm, n_q), lambda i: (i, 0))],
        out_shape=[jax.ShapeDtypeStruct((t, D_MODEL), F32), jax.ShapeDtypeStruct((t, D_MODEL), BF16),
                   jax.ShapeDtypeStruct((t, n_q), BF16)],
        scratch_shapes=[pltpu.VMEM((tm + 2 * CONV_HALO, D_MODEL), F32), pltpu.VMEM((tm, D_MODEL), F32),
                        pltpu.VMEM((_F32_ROWS - 1, tm + 2 * CONV_HALO, _LANES), F32)],
        compiler_params=_params(1),
        name="mixer_tail",
    )(*args)


_TOK_BLOCKS = 8
_LANE = 128
_ROUTE_TM = _TOK_BLOCKS * _LANE


def _merge_exchange_pairs(n):
    pairs = []
    t = (n - 1).bit_length()
    p = 1 << (t - 1)
    while p > 0:
        q, r, d = 1 << (t - 1), 0, p
        while d > 0:
            pairs.extend((i, i + d) for i in range(n - d) if (i & p) == r)
            d, q, r = q - p, q >> 1, p
        p >>= 1
    return tuple(pairs)


_SORT16 = _merge_exchange_pairs(PEER_TOPK)


def _sort_desc(vals):
    vals = list(vals)
    for i, j in _SORT16:
        vals[i], vals[j] = jnp.maximum(vals[i], vals[j]), jnp.minimum(vals[i], vals[j])
    return vals


def _bitonic_desc(vals):
    vals = list(vals)
    d = len(vals) // 2
    while d > 0:
        for i in range(len(vals)):
            if (i & d) == 0:
                vals[i], vals[i + d] = jnp.maximum(vals[i], vals[i + d]), jnp.minimum(vals[i], vals[i + d])
        d //= 2
    return vals


def _top_merge(a, b):
    n = len(a)
    return _bitonic_desc([jnp.maximum(a[i], b[n - 1 - i]) for i in range(n)])


def _top16(keys):
    if len(keys) == PEER_TOPK:
        return _sort_desc(keys)
    half = len(keys) // 2
    return _top_merge(_top16(keys[:half]), _top16(keys[half:]))


def _prefix_count(test, v):
    t8 = test(v[7])
    t4 = test(jnp.where(t8, v[11], v[3]))
    t2 = test(jnp.where(t8, jnp.where(t4, v[13], v[9]), jnp.where(t4, v[5], v[1])))
    lo = jnp.where(t4, jnp.where(t2, v[6], v[4]), jnp.where(t2, v[2], v[0]))
    hi = jnp.where(t4, jnp.where(t2, v[14], v[12]), jnp.where(t2, v[10], v[8]))
    t1 = test(jnp.where(t8, hi, lo))
    cnt = (jnp.where(t8, 8.0, 0.0) + jnp.where(t4, 4.0, 0.0)) + (jnp.where(t2, 2.0, 0.0) + jnp.where(t1, 1.0, 0.0))
    return jnp.where(test(v[15]), 16.0, cnt)


def _joint_top16(a, b):
    q0 = [a[p] + b[0] for p in range(16)]
    q1 = [a[p] + b[1] for p in range(8)]
    p0 = [a[0] + b[q] for q in range(8, 16)]
    m1 = _bitonic_desc(q1 + p0[::-1])
    m2 = _sort_desc([a[p] + b[q] for q, n in ((2, 5), (3, 4), (4, 3), (5, 2), (6, 2)) for p in range(n)])
    q7 = [a[0] + b[7], a[1] + b[7]]
    t2 = _bitonic_desc(m2[:14] + [jnp.maximum(m2[14], q7[1]), jnp.maximum(m2[15], q7[0])])
    return _top_merge(_top_merge(q0, m1), t2)


def _route_kernel(q_ref, skx_ref, rank2_ref, e2_ref, r_ref, a1_ref, km_rank2, km_e2, km_r, km_a1):
    nb = _TOK_BLOCKS
    keys = []
    for p in range(2):
        qp = jnp.concatenate([q_ref[a * _LANE:(a + 1) * _LANE, p * PEER_NKEYS:(p + 1) * PEER_NKEYS]
                              for a in range(nb)], axis=1)
        sp = _dot_nt(skx_ref[p], qp)
        keys.append([sp[k * nb:(k + 1) * nb, :] for k in range(PEER_NKEYS)])
    s1, s2 = keys
    a = _top16(s1)
    b = _top16(s2)
    top = _joint_top16(a, b)
    tau = top[PEER_TOPK - 1]
    zsum = jnp.ones_like(tau)
    for c in top[1:]:
        zsum = zsum + jnp.exp(c - top[0])
    inv_z = 1.0 / zsum
    b_asc = b[::-1]
    for k in range(PEER_NKEYS):
        rows = slice(k * nb, (k + 1) * nb)
        km_r[rows, :] = _prefix_count(lambda bq, x=s1[k]: x + bq >= tau, b)
        km_a1[rows, :] = jnp.exp(s1[k] - a[0]) * inv_z
        km_rank2[rows, :] = float(PEER_TOPK) - _prefix_count(lambda bq, x=s2[k]: x >= bq, b_asc)
        km_e2[rows, :] = jnp.exp(s2[k] - b[0])
    for blk in range(nb):
        cols = slice(blk * _LANE, (blk + 1) * _LANE)
        rows = pl.ds(blk, PEER_NKEYS, stride=nb)
        rank2_ref[:, cols] = km_rank2[rows, :].astype(BF16)
        e2_ref[:, cols] = km_e2[rows, :].astype(BF16)
        r_ref[:, cols] = km_r[rows, :]
        a1_ref[:, cols] = km_a1[rows, :]


def _route(q, skx):
    t = q.shape[0]
    tm = _ROUTE_TM
    assert t % tm == 0, (t, tm)
    out_blk = pl.BlockSpec((None, PEER_NKEYS, tm), lambda h, i: (h, 0, i))
    shape = (PEER_HEADS, PEER_NKEYS, t)
    km = pltpu.VMEM((PEER_NKEYS * _TOK_BLOCKS, _LANE), F32)
    return pl.pallas_call(
        _route_kernel,
        grid=(PEER_HEADS, t // tm),
        in_specs=[pl.BlockSpec((tm, 2 * PEER_NKEYS), lambda h, i: (i, h)),
                  pl.BlockSpec((None, 2, PEER_NKEYS * _TOK_BLOCKS, _TOK_BLOCKS * PEER_NKEYS),
                               lambda h, i: (h, 0, 0, 0))],
        out_specs=[out_blk, out_blk, out_blk, out_blk],
        out_shape=[jax.ShapeDtypeStruct(shape, BF16), jax.ShapeDtypeStruct(shape, BF16),
                   jax.ShapeDtypeStruct(shape, F32), jax.ShapeDtypeStruct(shape, F32)],
        scratch_shapes=[km, km, km, km],
        compiler_params=_params(2),
        name="peer_route",
    )(q, skx)


_GELU_C = math.sqrt(2.0 / math.pi)
_BF16_ROWS = 16
_F32_ROWS = 8
_MXU_N = 256


def _gelu_tanh(x):
    neg2z = x * (x * x * (-2.0 * _GELU_C * 0.044715) + (-2.0 * _GELU_C))
    return x / (1.0 + jnp.exp(neg2z))


def _experts_kernel(ib, n_eblk, alpha, u_ref, ut_ref, vt_ref, rank2_ref, e2_ref, r_ref, a1_ref, x1_ref, mod_ref,
                    lnw_ref, lnb_ref, y_ref, acc_scr, at_scr, p_scr, uT_scr):
    s = pl.program_id(1)

    @pl.when(s == 0)
    def _():
        acc_scr[...] = jnp.zeros_like(acc_scr)
        uT_scr[...] = u_ref[...].astype(F32).T.astype(BF16)

    tm = u_ref.shape[0]
    key_tiles = PEER_NKEYS // _BF16_ROWS
    rows_per_chunk = _MXU_N // PEER_NKEYS
    n_chunks = ib // rows_per_chunk

    def tile_rows(ref, h, ii, cols):
        group = ref[h, pl.ds(pl.multiple_of(s * ib + (ii // _F32_ROWS) * _F32_ROWS, _F32_ROWS), _F32_ROWS), cols]
        row = group[ii % _F32_ROWS:ii % _F32_ROWS + 1, :]
        tile = jnp.broadcast_to(row, (_BF16_ROWS, _MXU_N)).astype(BF16)
        return jnp.concatenate([tile] * key_tiles, axis=0)

    def weights(ii):
        rows = slice(ii * PEER_NKEYS, (ii + 1) * PEER_NKEYS)
        for c in range(tm // _MXU_N):
            cols = slice(c * _MXU_N, (c + 1) * _MXU_N)
            act = _gelu_tanh(at_scr[rows, cols].astype(BF16))
            w = None
            for h in range(PEER_HEADS):
                r_b = tile_rows(r_ref, h, ii, cols)
                a_b = tile_rows(a1_ref, h, ii, cols)
                term = jnp.where(rank2_ref[h, :, cols] < r_b, e2_ref[h, :, cols] * a_b, jnp.zeros((), BF16))
                w = term if w is None else w + term
            p_scr[rows, cols] = w * act

    for ch in range(n_chunks):
        erows = slice(ch * _MXU_N, (ch + 1) * _MXU_N)
        at_scr[erows, :] = _dot(ut_ref[erows, :], uT_scr[...])
    for ch in range(n_chunks):
        erows = slice(ch * _MXU_N, (ch + 1) * _MXU_N)
        for ii in range(ch * rows_per_chunk, (ch + 1) * rows_per_chunk):
            weights(ii)
        acc_scr[...] += _dot(vt_ref[:, erows], p_scr[erows, :])

    @pl.when(s == n_eblk - 1)
    def _():
        mod = mod_ref[...]
        gate2 = mod[:, 5 * D_MODEL:6 * D_MODEL]
        y = alpha * x1_ref[...] + gate2 * acc_scr[...].T
        y_ref[...] = _layer_norm(y, lnw_ref[...], lnb_ref[...])


def _experts(u2, u_tab, vt_tab, rank2, e2, r, a1, x1, mod, row_fn, lnw, lnb, tm, eb, alpha):
    t = u2.shape[0]
    n_exp = u_tab.shape[0]
    ib = eb // PEER_NKEYS
    n_eblk = n_exp // eb
    full = pl.BlockSpec((PEER_HEADS, PEER_NKEYS, tm), lambda i, s: (0, 0, i))
    tok = pl.BlockSpec((tm, D_MODEL), lambda i, s: (i, 0))
    vec = pl.BlockSpec((1, D_MODEL), lambda i, s: (0, 0))
    return pl.pallas_call(
        functools.partial(_experts_kernel, ib, n_eblk, alpha),
        grid=(t // tm, n_eblk),
        in_specs=[tok,
                  pl.BlockSpec((eb, D_MODEL), lambda i, s: (s, 0)),
                  pl.BlockSpec((D_MODEL, eb), lambda i, s: (0, s)),
                  full, full, full, full, tok,
                  pl.BlockSpec((None, 1, mod.shape[2]), lambda i, s: (row_fn(i), 0, 0)),
                  vec, vec],
        out_specs=tok,
        out_shape=jax.ShapeDtypeStruct((t, D_MODEL), F32),
        scratch_shapes=[pltpu.VMEM((D_MODEL, tm), F32), pltpu.VMEM((eb, tm), F32),
                        pltpu.VMEM((eb, tm), BF16), pltpu.VMEM((D_MODEL, tm), BF16)],
        compiler_params=_params(2),
        name="peer_experts",
    )(u2, u_tab, vt_tab, rank2, e2, r, a1, x1, mod, lnw, lnb)


def _grid_pos_embed(n_tokens):
    rows = n_tokens // GRID_W
    quarter = D_MODEL // 4
    freqs = jnp.exp(-math.log(POS_BASE) * jnp.arange(quarter, dtype=F32) / quarter)
    r = jnp.arange(rows, dtype=F32)[:, None] * freqs
    cl = jnp.arange(GRID_W, dtype=F32)[:, None] * freqs
    er = jnp.concatenate([jnp.sin(r), jnp.cos(r)], -1)
    ec = jnp.concatenate([jnp.sin(cl), jnp.cos(cl)], -1)
    emb = jnp.concatenate([jnp.broadcast_to(er[:, None, :], (rows, GRID_W, D_MODEL // 2)),
                           jnp.broadcast_to(ec[None, :, :], (rows, GRID_W, D_MODEL // 2))], -1)
    return emb.reshape(rows * GRID_W, D_MODEL)


def _pick_tile(seq_len, n_tokens, target):
    tm = min(target, seq_len)
    while seq_len % tm or n_tokens % tm:
        tm //= 2
    return tm


def _block(x, pos, mod, n_seq, seq_len, tm_in, row_of_token_tile, state0, want_state, lw, alpha):
    t = x.shape[0]
    proj, projt, gates, gatest = _inproj(x, pos, mod, row_of_token_tile(tm_in), lw, tm_in)
    res = _mlstm(proj, projt, gates, gatest, n_seq, seq_len, state0, want_state)
    hf, hb = res[0], res[1]
    tm_tail = _pick_tile(seq_len, t, 256)
    x1, u2, q = _tail(hf, hb, proj, projt, x, pos, mod, row_of_token_tile(tm_tail), seq_len, tm_tail, alpha, lw)
    tm_p = _pick_tile(seq_len, t, 512)
    rank2, e2, r, a1 = _route(q, lw["skx"])
    y = _experts(u2, lw["peer_u"], lw["peer_vt"], rank2, e2, r, a1, x1, mod, row_of_token_tile(tm_p),
                 lw["ln2_w"], lw["ln2_b"], tm_p, 2048, alpha)
    return y, res[2:]


def kernel(x_prompt, x_sample, state_C, state_n, state_m, c, c_ctx, w_in, b_in, mlstm_norm_w, w_a, conv_dw_w,
           conv_dw_b, conv_ln_w, conv_ln_b, w_conv_out, w_out, w_mod, b_mod, ln1_w, ln1_b, ln2_w, ln2_b,
           peer_w_query, peer_subkeys, peer_u, peer_v):
    depth = w_in.shape[0]
    alpha = (2.0 * depth) ** 0.25
    bsz, seq, _ = x_prompt.shape
    dbsz, dseq, _ = x_sample.shape
    units = 2 * N_HEADS
    gate_off = 4 * D_MODEL

    ctx = x_prompt.reshape(bsz * seq, D_MODEL)
    lat = x_sample.reshape(dbsz * dseq, D_MODEL)
    pos = _grid_pos_embed(dseq)
    n_rows = 1 + dbsz
    pad_rows = (-n_rows) % 8
    cvec = jnp.concatenate([c_ctx[None, :], c, jnp.zeros((pad_rows, D_MODEL), F32)], axis=0)

    new_c, new_n, new_m = [], [], []
    for l in range(depth):
        vec = lambda a: a[l].reshape(1, -1)
        w_l = w_in[l]
        b_l = b_in[l]
        wg = jnp.pad(w_l[:, gate_off:gate_off + N_GATES], ((0, 0), (0, GATE_PAD - N_GATES)))
        wg_hi = wg.astype(BF16)
        wg_lo = (wg - wg_hi.astype(F32)).astype(BF16)
        b_gate = jnp.pad(b_l[gate_off:gate_off + N_GATES], (0, GATE_PAD - N_GATES))
        w_cols = jnp.concatenate([w_l[:, :gate_off], w_l[:, gate_off + N_GATES:]], axis=1)
        b_cols = jnp.concatenate([b_l[:gate_off], b_l[gate_off + N_GATES:]])
        n_steps = w_cols.shape[1] // D_MODEL
        blocks = [w_cols[:, s * D_MODEL:(s + 1) * D_MODEL] for s in range(n_steps)]
        lanes = lambda col: jnp.broadcast_to(col[:, None], (col.shape[0], _LANES))
        lw = {
            "w_main": jnp.concatenate([blk.T if s in _FEATURE_MAJOR_STEPS else blk for s, blk in enumerate(blocks)],
                                      axis=1).astype(BF16),
            "b_main": b_cols.reshape(1, -1),
            "b_feat": jnp.stack([lanes(b_cols[s * D_MODEL:(s + 1) * D_MODEL]) for s in _FEATURE_MAJOR_STEPS]),
            "wg_hi": wg_hi, "wg_lo": wg_lo, "b_gate": b_gate.reshape(1, -1),
            "wgt_hi": wg_hi.T, "wgt_lo": wg_lo.T, "b_gate_t": lanes(b_gate),
            "norm_w": lanes(mlstm_norm_w[l]), "w_a": w_a[l].astype(BF16), "w_cout": w_conv_out[l].astype(BF16),
            "w_out": w_out[l].astype(BF16),
            "dw_w": jnp.pad(conv_dw_w[l], ((0, 32 - CONV_WIDTH), (0, 0))), "dw_b": vec(conv_dw_b),
            "cln_w": vec(conv_ln_w), "cln_b": vec(conv_ln_b), "ln1_w": vec(ln1_w), "ln1_b": vec(ln1_b),
            "ln2_w": vec(ln2_w), "ln2_b": vec(ln2_b),
            "wq": peer_w_query[l].astype(BF16),
            "skx": jnp.einsum("hpkd,ab->hpkabd", peer_subkeys[l].astype(BF16), jnp.eye(_TOK_BLOCKS, dtype=BF16))
            .reshape(PEER_HEADS, 2, PEER_NKEYS * _TOK_BLOCKS, _TOK_BLOCKS * PEER_NKEYS),
            "peer_u": peer_u[l].astype(BF16),
            "peer_vt": peer_v[l].T.astype(BF16),
        }
        mod = _modulation(cvec, w_mod[l], b_mod[l]).reshape(n_rows + pad_rows, 1, 6 * D_MODEL)

        ctx, (c_fin, n_fin, m_fin) = _block(ctx, None, mod, bsz, seq, _pick_tile(bsz * seq, bsz * seq, 1024),
                                             lambda tm: (lambda i: 0), None, True, lw, alpha)
        new_c.append(c_fin.reshape(bsz, 2, N_HEADS, HEAD_DIM, HEAD_DIM))
        new_n.append(n_fin.reshape(bsz, 2, N_HEADS, HEAD_DIM))
        new_m.append(m_fin[:, :, 0].reshape(bsz, 2, N_HEADS))

        state0 = (state_C[:, l].reshape(dbsz, units, HEAD_DIM, HEAD_DIM),
                  state_n[:, l].reshape(dbsz, units, HEAD_DIM),
                  jnp.broadcast_to(state_m[:, l].reshape(dbsz, units, 1), (dbsz, units, GATE_PAD)))
        lat, _ = _block(lat, pos, mod, dbsz, dseq, _pick_tile(dseq, dbsz * dseq, 1024),
                        lambda tm: (lambda i: 1 + (i * tm) // dseq), state0, False, lw, alpha)

    return (ctx.reshape(bsz, seq, D_MODEL), lat.reshape(dbsz, dseq, D_MODEL),
            jnp.stack(new_c, axis=1), jnp.stack(new_n, axis=1), jnp.stack(new_m, axis=1))
```

```python
import functools
import math

import jax
import jax.numpy as jnp
from jax import lax
from jax.experimental import pallas as pl
from jax.experimental.pallas import tpu as pltpu

F32 = jnp.float32
BF16 = jnp.bfloat16

D_MODEL = 1024
N_HEADS = 4
HEAD_DIM = 256
CHUNK = 128
CONV_WIDTH = 31
CONV_HALO = 16
N_GATES = 16
GATE_PAD = 128
GRID_W = 64
POS_BASE = 10000.0
LN_EPS = 1e-6
PEER_HEADS = 8
PEER_NKEYS = 128
PEER_TOPK = 16
V7X_VMEM_LIMIT = 56 * 1024 * 1024


def _params(n_axes):
    return pltpu.CompilerParams(dimension_semantics=("arbitrary",) * n_axes,
                                vmem_limit_bytes=V7X_VMEM_LIMIT)


def _sigmoid(x):
    return 0.5 + 0.5 * jnp.tanh(0.5 * x)


def _log_sigmoid(x):
    return jnp.minimum(x, 0.0) - jnp.log(1.0 + jnp.exp(-jnp.abs(x)))


def _dot(a, b):
    return jnp.dot(a, b, preferred_element_type=F32)


def _dot_nt(a, b):
    return lax.dot_general(a, b, (((1,), (1,)), ((), ())), preferred_element_type=F32)


def _split2(x):
    hi = x.astype(BF16)
    lo = (x - hi.astype(F32)).astype(BF16)
    return hi, lo


def _split3(x):
    a = x.astype(BF16)
    r = x - a.astype(F32)
    b = r.astype(BF16)
    c = (r - b.astype(F32)).astype(BF16)
    return a, b, c


def _layer_norm(x, w, b):
    mu = jnp.mean(x, axis=-1, keepdims=True)
    xc = x - mu
    var = jnp.mean(xc * xc, axis=-1, keepdims=True)
    return xc * lax.rsqrt(var + LN_EPS) * w + b


def _mod_kernel(c_ref, w_ref, b_ref, o_ref):
    c = c_ref[...]
    s = c * _sigmoid(c)
    sh, sl = _split2(s)
    wh, wl = _split2(w_ref[...])
    o_ref[...] = _dot(sh, wh) + _dot(sl, wh) + _dot(sh, wl) + b_ref[...]


def _modulation(cvec, w_mod, b_mod):
    rows = cvec.shape[0]
    tn = 1536
    n = w_mod.shape[1]
    return pl.pallas_call(
        _mod_kernel,
        grid=(n // tn,),
        in_specs=[pl.BlockSpec((rows, D_MODEL), lambda j: (0, 0)),
                  pl.BlockSpec((D_MODEL, tn), lambda j: (0, j)),
                  pl.BlockSpec((1, tn), lambda j: (0, j))],
        out_specs=pl.BlockSpec((rows, tn), lambda j: (0, j)),
        out_shape=jax.ShapeDtypeStruct((rows, n), F32),
        compiler_params=_params(1),
        name="modulation",
    )(cvec, w_mod, b_mod.reshape(1, n))


_STEP_K = 1
_FEATURE_MAJOR_STEPS = (0, 2, 3)
_N_TOKEN_MAJOR = 5


def _lane_tile(block, n_lanes):
    return jnp.concatenate([block] * (n_lanes // block.shape[1]), axis=1)


def _inproj_kernel(has_pos, *refs):
    if has_pos:
        (x_ref, pos_ref, mod_ref, w_ref, b_ref, bt_ref, wgh_ref, wgl_ref, bg_ref,
         proj_ref, projt_ref, gates_ref, gatest_ref, u_scr) = refs
    else:
        (x_ref, mod_ref, w_ref, b_ref, bt_ref, wgh_ref, wgl_ref, bg_ref,
         proj_ref, projt_ref, gates_ref, gatest_ref, u_scr) = refs
    j = pl.program_id(1)
    tm = u_scr.shape[0]

    @pl.when(j == 0)
    def _():
        x = x_ref[...]
        if has_pos:
            x = x + pos_ref[...]
        mod = mod_ref[...]
        u = x * (1.0 + mod[:, D_MODEL:2 * D_MODEL]) + mod[:, 0:D_MODEL]
        uh, ul = _split2(u)
        u_scr[...] = uh
        wgh = wgh_ref[...]
        gates = _dot(uh, wgh) + _dot(ul, wgh) + _dot(uh, wgl_ref[...]) + bg_ref[...]
        gates_ref[...] = gates
        gatest_ref[...] = gates.T

    is_t = jnp.logical_or(j == 0, jnp.logical_or(j == 2, j == 3))

    @pl.when(is_t)
    def _():
        acc = _dot_nt(w_ref[...], u_scr[...]) + _lane_tile(bt_ref[...], tm)

        @pl.when(j == 3)
        def _():
            projt_ref[...] = _sigmoid(acc.astype(BF16))

        @pl.when(j != 3)
        def _():
            projt_ref[...] = acc.astype(BF16)

    @pl.when(jnp.logical_not(is_t))
    def _():
        acc = _dot(u_scr[...], w_ref[...]) + b_ref[...]

        @pl.when(j == _STEP_K)
        def _():
            proj_ref[...] = (acc * (HEAD_DIM ** -0.5)).astype(BF16)

        @pl.when(j == 4)
        def _():
            proj_ref[...] = acc.astype(BF16)

        @pl.when(j >= 5)
        def _():
            proj_ref[...] = _sigmoid(acc.astype(BF16))


def _inproj(x, pos, mod, row_fn, lw, tm):
    t = x.shape[0]
    n_steps = lw["w_main"].shape[1] // D_MODEL
    has_pos = pos is not None
    in_specs = [pl.BlockSpec((tm, D_MODEL), lambda i, j: (i, 0))]
    args = [x]
    if has_pos:
        pos_blocks = pos.shape[0] // tm
        in_specs.append(pl.BlockSpec((tm, D_MODEL), lambda i, j: (i % pos_blocks, 0)))
        args.append(pos)

    def token_major_block(j):
        return jnp.maximum(j - 3, 0)

    def feature_major_block(j):
        return jnp.clip(j - 1, 0, 2)

    in_specs += [
        pl.BlockSpec((None, 1, mod.shape[2]), lambda i, j: (row_fn(i), 0, 0)),
        pl.BlockSpec((D_MODEL, D_MODEL), lambda i, j: (0, j)),
        pl.BlockSpec((1, D_MODEL), lambda i, j: (0, j)),
        pl.BlockSpec((None, D_MODEL, _LANES), lambda i, j: (feature_major_block(j), 0, 0)),
        pl.BlockSpec((D_MODEL, GATE_PAD), lambda i, j: (0, 0)),
        pl.BlockSpec((D_MODEL, GATE_PAD), lambda i, j: (0, 0)),
        pl.BlockSpec((1, GATE_PAD), lambda i, j: (0, 0)),
    ]
    args += [mod, lw["w_main"], lw["b_main"], lw["b_feat"], lw["wg_hi"], lw["wg_lo"], lw["b_gate"]]
    return pl.pallas_call(
        functools.partial(_inproj_kernel, has_pos),
        grid=(t // tm, n_steps),
        in_specs=in_specs,
        out_specs=[pl.BlockSpec((tm, D_MODEL), lambda i, j: (i, token_major_block(j))),
                   pl.BlockSpec((D_MODEL, tm), lambda i, j: (feature_major_block(j), i)),
                   pl.BlockSpec((tm, GATE_PAD), lambda i, j: (i, 0)),
                   pl.BlockSpec((GATE_PAD, tm), lambda i, j: (0, i))],
        out_shape=[jax.ShapeDtypeStruct((t, _N_TOKEN_MAJOR * D_MODEL), BF16),
                   jax.ShapeDtypeStruct((len(_FEATURE_MAJOR_STEPS) * D_MODEL, t), BF16),
                   jax.ShapeDtypeStruct((t, GATE_PAD), F32),
                   jax.ShapeDtypeStruct((GATE_PAD, t), F32)],
        scratch_shapes=[pltpu.VMEM((tm, D_MODEL), BF16)],
        compiler_params=_params(2),
        name="inproj",
    )(*args)


_SEQ_INPUTS = 10


def _mlstm_kernel(has_state, want_state, nc, n_par, *refs):
    refs = list(refs)
    seq_in = [refs[a * _SEQ_INPUTS:(a + 1) * _SEQ_INPUTS] for a in range(n_par)]
    pos = n_par * _SEQ_INPUTS
    if has_state:
        c0_ref, n0_ref, m0_ref = refs[pos:pos + 3]
        pos += 3
    hf_ref, hb_ref = refs[pos:pos + 2]
    pos += 2
    if want_state:
        co_ref, no_ref, mo_ref = refs[pos:pos + 3]
        pos += 3
    c_all, n_all, m_all = refs[pos:pos + 3]
    step = pl.program_id(1)

    @pl.when(step == 0)
    def _():
        if has_state:
            c_all[...] = c0_ref[...]
            n_all[...] = n0_ref[...]
            m_all[...] = m0_ref[...]
        else:
            c_all[...] = jnp.zeros_like(c_all)
            n_all[...] = jnp.zeros_like(n_all)
            m_all[...] = jnp.zeros_like(m_all)

    row = lax.broadcasted_iota(jnp.int32, (CHUNK, CHUNK), 0)
    col = lax.broadcasted_iota(jnp.int32, (CHUNK, CHUNK), 1)

    streams = []
    for a in range(n_par):
        qf, kf, vf, qb, kb, vb, gf, gb, gtf, gtb = seq_in[a]
        streams.append((a, 0, qf, kf, vf, gf, gtf, hf_ref))
        streams.append((a, 1, qb, kb, vb, gb, gtb, hb_ref))
    for a, d, qt_ref, k_ref, vt_ref, g_ref, gt_ref, h_ref in streams:
        c_scr, n_scr, m_scr = c_all.at[a], n_all.at[a], m_all.at[a]
        visible = (row <= col) if d == 0 else (row >= col)
        tri_t = jnp.where(visible, 1.0, 0.0).astype(BF16)
        tri = jnp.where((col <= row) if d == 0 else (col >= row), 1.0, 0.0).astype(BF16)
        g = g_ref[...]
        g_t = gt_ref[...]
        l1, l2, l3 = _split3(_log_sigmoid(g))
        b_col_all = _dot(tri, l1) + _dot(tri, l2) + _dot(tri, l3)
        t1, t2, t3 = _split3(_log_sigmoid(g_t))
        b_row_all = _dot(t1, tri_t) + _dot(t2, tri_t) + _dot(t3, tri_t)
        last = CHUNK - 1 if d == 0 else 0
        for h in range(N_HEADS):
            u = d * N_HEADS + h
            ci = d * 2 * N_HEADS + h
            cf = ci + N_HEADS
            b_row = b_row_all[cf:cf + 1, :]
            i_row = g_t[ci:ci + 1, :]
            src_col = g[:, ci:ci + 1] - b_col_all[:, cf:cf + 1]
            m = m_scr[u:u + 1, 0:1]
            hs = slice(h * HEAD_DIM, (h + 1) * HEAD_DIM)
            q_t = qt_ref[hs, :]
            k = k_ref[:, hs]
            v_t = vt_ref[hs, :]
            c_state = c_scr[u]
            n_state = n_scr[u:u + 1, :]

            dmat_t = jnp.where(visible, src_col + b_row, -jnp.inf)
            inter = b_row + m
            m_t = jnp.maximum(inter, jnp.max(dmat_t, axis=0, keepdims=True))
            w_st = jnp.exp(dmat_t - m_t)
            a_t = jnp.exp(inter - m_t)
            s_t = _dot(k, q_t) * w_st
            num_t = _dot(v_t, s_t.astype(BF16)) + a_t * _dot(c_state.astype(BF16), q_t)
            n_rows = jnp.broadcast_to(n_state, (_F32_ROWS, HEAD_DIM)).astype(BF16)
            den = jnp.sum(s_t, axis=0, keepdims=True) + a_t * _dot(n_rows, q_t)[0:1, :]
            h_ref[hs, a * CHUNK:(a + 1) * CHUNK] = num_t / jnp.maximum(jnp.abs(den), jnp.exp(-m_t))

            b_last = b_row[:, last:last + 1]
            dec_row = b_last - b_row + i_row
            m_new = jnp.maximum(b_last + m, jnp.max(dec_row, axis=-1, keepdims=True))
            w_row = jnp.exp(dec_row - m_new)
            a_c = jnp.exp(b_last + m - m_new)
            vw_t = (v_t.astype(F32) * w_row).astype(BF16)
            c_scr[u] = a_c * c_state + _dot(vw_t, k)
            w_rows = jnp.broadcast_to(w_row, (_F32_ROWS, CHUNK)).astype(BF16)
            n_scr[u:u + 1, :] = a_c * n_state + _dot(w_rows, k)[0:1, :]
            m_scr[u:u + 1, :] = jnp.broadcast_to(m_new, (1, GATE_PAD))

    if want_state:
        @pl.when(step == nc - 1)
        def _():
            co_ref[...] = c_all[...]
            no_ref[...] = n_all[...]
            mo_ref[...] = m_all[...]


def _mlstm(proj, projt, gates, gatest, n_seq, seq_len, state0, want_state):
    t = proj.shape[0]
    nc = seq_len // CHUNK
    units = 2 * N_HEADS
    has_state = state0 is not None

    n_par = _scan_group(n_seq)
    tile = (CHUNK, D_MODEL)
    tile_t = (D_MODEL, CHUNK)
    in_specs, args = [], []
    for a in range(n_par):
        def fwd(c, a=a):
            return lambda p, k: ((p * n_par + a) * nc + k, c)

        def bwd(c, a=a):
            return lambda p, k: ((p * n_par + a) * nc + nc - 1 - k, c)

        def fwd_t(r, a=a):
            return lambda p, k: (r, (p * n_par + a) * nc + k)

        def bwd_t(r, a=a):
            return lambda p, k: (r, (p * n_par + a) * nc + nc - 1 - k)

        in_specs += [pl.BlockSpec(tile_t, fwd_t(0)), pl.BlockSpec(tile, fwd(0)), pl.BlockSpec(tile_t, fwd_t(1)),
                     pl.BlockSpec(tile_t, bwd_t(0)), pl.BlockSpec(tile, bwd(0)), pl.BlockSpec(tile_t, bwd_t(1)),
                     pl.BlockSpec((CHUNK, GATE_PAD), fwd(0)), pl.BlockSpec((CHUNK, GATE_PAD), bwd(0)),
                     pl.BlockSpec((GATE_PAD, CHUNK), fwd_t(0)), pl.BlockSpec((GATE_PAD, CHUNK), bwd_t(0))]
        args += [projt, proj, projt, projt, proj, projt, gates, gates, gatest, gatest]
    state_specs = [pl.BlockSpec((n_par, units, HEAD_DIM, HEAD_DIM), lambda p, k: (p, 0, 0, 0)),
                   pl.BlockSpec((n_par, units, HEAD_DIM), lambda p, k: (p, 0, 0)),
                   pl.BlockSpec((n_par, units, GATE_PAD), lambda p, k: (p, 0, 0))]
    state_shapes = [jax.ShapeDtypeStruct((n_seq, units, HEAD_DIM, HEAD_DIM), F32),
                    jax.ShapeDtypeStruct((n_seq, units, HEAD_DIM), F32),
                    jax.ShapeDtypeStruct((n_seq, units, GATE_PAD), F32)]
    if has_state:
        in_specs += state_specs
        args += list(state0)
    group_tile = (D_MODEL, n_par * CHUNK)
    out_specs = [pl.BlockSpec(group_tile, lambda p, k: (0, p * nc + k)),
                 pl.BlockSpec(group_tile, lambda p, k: (0, p * nc + nc - 1 - k))]
    out_shape = [jax.ShapeDtypeStruct((D_MODEL, t), F32), jax.ShapeDtypeStruct((D_MODEL, t), F32)]
    if want_state:
        out_specs += state_specs
        out_shape += state_shapes
    return pl.pallas_call(
        functools.partial(_mlstm_kernel, has_state, want_state, nc, n_par),
        grid=(n_seq // n_par, nc),
        in_specs=in_specs,
        out_specs=out_specs,
        out_shape=out_shape,
        scratch_shapes=[pltpu.VMEM((n_par, units, HEAD_DIM, HEAD_DIM), F32),
                        pltpu.VMEM((n_par, units, HEAD_DIM), F32),
                        pltpu.VMEM((n_par, units, GATE_PAD), F32)],
        compiler_params=_params(2),
        name="mlstm",
    )(*args)


def _scan_group(n_seq):
    return 2 if n_seq % 2 == 0 else 1


def _scan_column(seq, chunk, nc, n_par):
    return ((seq // n_par) * nc + chunk) * n_par + seq % n_par


_CONV_ROWS = 64
_LANES = 128


def _tail_kernel(has_pos, tm, tiles_per_seq, alpha, *refs):
    refs = list(refs)
    n_ct = tm // CHUNK
    h_refs = refs[:2 * n_ct]
    so_ref, val_ref, sg_ref, sga_ref, sgb_ref, vp_ref, gp_ref, vn_ref, gn_ref, x_ref = refs[2 * n_ct:2 * n_ct + 10]
    pos = 2 * n_ct + 10
    if has_pos:
        pos_ref = refs[pos]
        pos += 1
    (mod_ref, normw_ref, wa_ref, wc_ref, wo_ref, dww_ref, dwb_ref, clnw_ref, clnb_ref, ln1w_ref, ln1b_ref, wq_ref,
     x1_ref, u2_ref, q_ref, xpad, conv_scr, shifted) = refs[pos:]
    i = pl.program_id(0)

    hsum = jnp.concatenate([h_refs[j][...] + h_refs[n_ct + j][...] for j in range(n_ct)], axis=1)
    parts = []
    for h in range(N_HEADS):
        hh = hsum[h * HEAD_DIM:(h + 1) * HEAD_DIM, :]
        mu = jnp.mean(hh, axis=0, keepdims=True)
        hc = hh - mu
        var = jnp.mean(hc * hc, axis=0, keepdims=True)
        parts.append(hc * lax.rsqrt(var + LN_EPS))
    hn = jnp.concatenate(parts, axis=0) * _lane_tile(normw_ref[...], tm)
    hg = (so_ref[...].astype(F32) * hn).T.astype(BF16)
    branch_a = _dot(hg, wa_ref[...])

    first = (i % tiles_per_seq) == 0
    last = (i % tiles_per_seq) == tiles_per_seq - 1
    keep_prev = jnp.where(first, 0.0, 1.0)
    keep_next = jnp.where(last, 0.0, 1.0)
    xpad[0:CONV_HALO, :] = vp_ref[...].astype(F32) * gp_ref[...].astype(F32) * keep_prev
    xpad[CONV_HALO:CONV_HALO + tm, :] = val_ref[...].astype(F32) * sg_ref[...].astype(F32)
    xpad[CONV_HALO + tm:2 * CONV_HALO + tm, :] = vn_ref[...].astype(F32) * gn_ref[...].astype(F32) * keep_next
    tap0 = CONV_HALO - CONV_WIDTH // 2

    n_shift_rows = tm + 2 * CONV_HALO - _F32_ROWS

    def col_body(c, carry):
        cs = pl.ds(pl.multiple_of(c * _LANES, _LANES), _LANES)
        for r in range(1, _F32_ROWS):
            shifted[r - 1, 0:n_shift_rows, :] = xpad[pl.ds(r, n_shift_rows), cs]
        for rb in range(tm // _CONV_ROWS):
            acc = jnp.broadcast_to(dwb_ref[:, cs], (_CONV_ROWS, _LANES))
            for k in range(CONV_WIDTH):
                tiles, r = divmod(tap0 + k, _F32_ROWS)
                start = tiles * _F32_ROWS + rb * _CONV_ROWS
                if r == 0:
                    src = xpad[pl.ds(start, _CONV_ROWS), cs]
                else:
                    src = shifted[r - 1, pl.ds(start, _CONV_ROWS), :]
                acc = acc + src * dww_ref[k:k + 1, cs]
            conv_scr[pl.ds(rb * _CONV_ROWS, _CONV_ROWS), cs] = acc
        return carry

    lax.fori_loop(0, D_MODEL // _LANES, col_body, 0)
    xc = _layer_norm(conv_scr[...], clnw_ref[...], clnb_ref[...])
    xc = (xc * _sigmoid(xc)).astype(BF16)
    branch_b = _dot(xc, wc_ref[...])

    merged = sga_ref[...].astype(F32) * branch_a + sgb_ref[...].astype(F32) * branch_b
    mix = _dot(merged.astype(BF16), wo_ref[...])

    mod = mod_ref[...]
    gate1 = mod[:, 2 * D_MODEL:3 * D_MODEL]
    shift2 = mod[:, 3 * D_MODEL:4 * D_MODEL]
    scale2 = mod[:, 4 * D_MODEL:5 * D_MODEL]
    x = x_ref[...]
    if has_pos:
        x = x + pos_ref[...]
    x1 = _layer_norm(alpha * x + gate1 * mix, ln1w_ref[...], ln1b_ref[...])
    x1_ref[...] = x1
    u2 = (x1 * (1.0 + scale2) + shift2).astype(BF16)
    u2_ref[...] = u2
    q_ref[...] = _dot(u2, wq_ref[...]).astype(BF16)


def _tail(hf, hb, proj, projt, x, pos, mod, row_fn, seq_len, tm, alpha, lw):
    t = x.shape[0]
    has_pos = pos is not None
    tiles_per_seq = seq_len // tm
    hb_per_tile = tm // CONV_HALO
    n_halo = t // CONV_HALO
    big = (tm, D_MODEL)
    halo = (CONV_HALO, D_MODEL)

    def colspec(c):
        return pl.BlockSpec(big, lambda i: (i, c))

    def prev(c):
        return pl.BlockSpec(halo, lambda i: (jnp.maximum(i * hb_per_tile - 1, 0), c))

    def nxt(c):
        return pl.BlockSpec(halo, lambda i: (jnp.minimum((i + 1) * hb_per_tile, n_halo - 1), c))

    def const(shape):
        return pl.BlockSpec(shape, lambda i: (0,) * len(shape), pipeline_mode=pl.Buffered(1))

    def rowspec(r):
        return pl.BlockSpec((D_MODEL, tm), lambda i: (r, i))

    nc = seq_len // CHUNK
    n_par = _scan_group(t // seq_len)
    n_ct = tm // CHUNK

    def scan_chunk(j):
        return pl.BlockSpec((D_MODEL, CHUNK), lambda i: (
            0, _scan_column(i // tiles_per_seq, (i % tiles_per_seq) * n_ct + j, nc, n_par)))

    in_specs = [scan_chunk(j) for j in range(n_ct)] * 2
    in_specs += [rowspec(2), colspec(1), colspec(2), colspec(3), colspec(4),
                 prev(1), prev(2), nxt(1), nxt(2), colspec(0)]
    args = [hf] * n_ct + [hb] * n_ct + [projt, proj, proj, proj, proj, proj, proj, proj, proj, x]
    if has_pos:
        in_specs.append(pl.BlockSpec(big, lambda i: (i % tiles_per_seq, 0)))
        args.append(pos)
    in_specs += [pl.BlockSpec((None, 1, mod.shape[2]), lambda i: (row_fn(i), 0, 0)),
                 const((D_MODEL, _LANES)), const((D_MODEL, D_MODEL)), const((D_MODEL, D_MODEL)),
                 const((D_MODEL, D_MODEL)),
                 const((32, D_MODEL)), const((1, D_MODEL)), const((1, D_MODEL)), const((1, D_MODEL)),
                 const((1, D_MODEL)), const((1, D_MODEL)), const(lw["wq"].shape)]
    args += [mod, lw["norm_w"], lw["w_a"], lw["w_cout"], lw["w_out"], lw["dw_w"], lw["dw_b"], lw["cln_w"],
             lw["cln_b"], lw["ln1_w"], lw["ln1_b"], lw["wq"]]
    n_q = lw["wq"].shape[1]
    return pl.pallas_call(
        functools.partial(_tail_kernel, has_pos, tm, tiles_per_seq, alpha),
        grid=(t // tm,),
        in_specs=in_specs,
        out_specs=[pl.BlockSpec(big, lambda i: (i, 0)), pl.BlockSpec(big, lambda i: (i, 0)),
                   pl.BlockSpec((tm, n_q), lambda i: (i, 0))],
        out_shape=[jax.ShapeDtypeStruct((t, D_MODEL), F32), jax.ShapeDtypeStruct((t, D_MODEL), BF16),
                   jax.ShapeDtypeStruct((t, n_q), BF16)],
        scratch_shapes=[pltpu.VMEM((tm + 2 * CONV_HALO, D_MODEL), F32), pltpu.VMEM((tm, D_MODEL), F32),
                        pltpu.VMEM((_F32_ROWS - 1, tm + 2 * CONV_HALO, _LANES), F32)],
        compiler_params=_params(1),
        name="mixer_tail",
    )(*args)


_TOK_BLOCKS = 8
_LANE = 128
_ROUTE_TM = _TOK_BLOCKS * _LANE


def _merge_exchange_pairs(n):
    pairs = []
    t = (n - 1).bit_length()
    p = 1 << (t - 1)
    while p > 0:
        q, r, d = 1 << (t - 1), 0, p
        while d > 0:
            pairs.extend((i, i + d) for i in range(n - d) if (i & p) == r)
            d, q, r = q - p, q >> 1, p
        p >>= 1
    return tuple(pairs)


_SORT16 = _merge_exchange_pairs(PEER_TOPK)


def _sort_desc(vals):
    vals = list(vals)
    for i, j in _SORT16:
        vals[i], vals[j] = jnp.maximum(vals[i], vals[j]), jnp.minimum(vals[i], vals[j])
    return vals


def _bitonic_desc(vals):
    vals = list(vals)
    d = len(vals) // 2
    while d > 0:
        for i in range(len(vals)):
            if (i & d) == 0:
                vals[i], vals[i + d] = jnp.maximum(vals[i], vals[i + d]), jnp.minimum(vals[i], vals[i + d])
        d //= 2
    return vals


def _top_merge(a, b):
    n = len(a)
    return _bitonic_desc([jnp.maximum(a[i], b[n - 1 - i]) for i in range(n)])


def _top16(keys):
    if len(keys) == PEER_TOPK:
        return _sort_desc(keys)
    half = len(keys) // 2
    return _top_merge(_top16(keys[:half]), _top16(keys[half:]))


def _prefix_count(test, v):
    t8 = test(v[7])
    t4 = test(jnp.where(t8, v[11], v[3]))
    t2 = test(jnp.where(t8, jnp.where(t4, v[13], v[9]), jnp.where(t4, v[5], v[1])))
    lo = jnp.where(t4, jnp.where(t2, v[6], v[4]), jnp.where(t2, v[2], v[0]))
    hi = jnp.where(t4, jnp.where(t2, v[14], v[12]), jnp.where(t2, v[10], v[8]))
    t1 = test(jnp.where(t8, hi, lo))
    cnt = (jnp.where(t8, 8.0, 0.0) + jnp.where(t4, 4.0, 0.0)) + (jnp.where(t2, 2.0, 0.0) + jnp.where(t1, 1.0, 0.0))
    return jnp.where(test(v[15]), 16.0, cnt)


def _joint_top16(a, b):
    q0 = [a[p] + b[0] for p in range(16)]
    q1 = [a[p] + b[1] for p in range(8)]
    p0 = [a[0] + b[q] for q in range(8, 16)]
    m1 = _bitonic_desc(q1 + p0[::-1])
    m2 = _sort_desc([a[p] + b[q] for q, n in ((2, 5), (3, 4), (4, 3), (5, 2), (6, 2)) for p in range(n)])
    q7 = [a[0] + b[7], a[1] + b[7]]
    t2 = _bitonic_desc(m2[:14] + [jnp.maximum(m2[14], q7[1]), jnp.maximum(m2[15], q7[0])])
    return _top_merge(_top_merge(q0, m1), t2)


def _route_kernel(q_ref, skx_ref, rank2_ref, e2_ref, r_ref, a1_ref, km_rank2, km_e2, km_r, km_a1):
    nb = _TOK_BLOCKS
    keys = []
    for p in range(2):
        qp = jnp.concatenate([q_ref[a * _LANE:(a + 1) * _LANE, p * PEER_NKEYS:(p + 1) * PEER_NKEYS]
                              for a in range(nb)], axis=1)
        sp = _dot_nt(skx_ref[p], qp)
        keys.append([sp[k * nb:(k + 1) * nb, :] for k in range(PEER_NKEYS)])
    s1, s2 = keys
    a = _top16(s1)
    b = _top16(s2)
    top = _joint_top16(a, b)
    tau = top[PEER_TOPK - 1]
    zsum = jnp.ones_like(tau)
    for c in top[1:]:
        zsum = zsum + jnp.exp(c - top[0])
    inv_z = 1.0 / zsum
    b_asc = b[::-1]
    for k in range(PEER_NKEYS):
        rows = slice(k * nb, (k + 1) * nb)
        km_r[rows, :] = _prefix_count(lambda bq, x=s1[k]: x + bq >= tau, b)
        km_a1[rows, :] = jnp.exp(s1[k] - a[0]) * inv_z
        km_rank2[rows, :] = float(PEER_TOPK) - _prefix_count(lambda bq, x=s2[k]: x >= bq, b_asc)
        km_e2[rows, :] = jnp.exp(s2[k] - b[0])
    for blk in range(nb):
        cols = slice(blk * _LANE, (blk + 1) * _LANE)
        rows = pl.ds(blk, PEER_NKEYS, stride=nb)
        rank2_ref[:, cols] = km_rank2[rows, :].astype(BF16)
        e2_ref[:, cols] = km_e2[rows, :].astype(BF16)
        r_ref[:, cols] = km_r[rows, :]
        a1_ref[:, cols] = km_a1[rows, :]


def _route(q, skx):
    t = q.shape[0]
    tm = _ROUTE_TM
    assert t % tm == 0, (t, tm)
    out_blk = pl.BlockSpec((None, PEER_NKEYS, tm), lambda h, i: (h, 0, i))
    shape = (PEER_HEADS, PEER_NKEYS, t)
    km = pltpu.VMEM((PEER_NKEYS * _TOK_BLOCKS, _LANE), F32)
    return pl.pallas_call(
        _route_kernel,
        grid=(PEER_HEADS, t // tm),
        in_specs=[pl.BlockSpec((tm, 2 * PEER_NKEYS), lambda h, i: (i, h)),
                  pl.BlockSpec((None, 2, PEER_NKEYS * _TOK_BLOCKS, _TOK_BLOCKS * PEER_NKEYS),
                               lambda h, i: (h, 0, 0, 0))],
        out_specs=[out_blk, out_blk, out_blk, out_blk],
        out_shape=[jax.ShapeDtypeStruct(shape, BF16), jax.ShapeDtypeStruct(shape, BF16),
                   jax.ShapeDtypeStruct(shape, F32), jax.ShapeDtypeStruct(shape, F32)],
        scratch_shapes=[km, km, km, km],
        compiler_params=_params(2),
        name="peer_route",
    )(q, skx)


_GELU_C = math.sqrt(2.0 / math.pi)
_BF16_ROWS = 16
_F32_ROWS = 8
_MXU_N = 256


def _gelu_tanh(x):
    neg2z = x * (x * x * (-2.0 * _GELU_C * 0.044715) + (-2.0 * _GELU_C))
    return x / (1.0 + jnp.exp(neg2z))


def _experts_kernel(ib, n_eblk, alpha, u_ref, ut_ref, vt_ref, rank2_ref, e2_ref, r_ref, a1_ref, x1_ref, mod_ref,
                    lnw_ref, lnb_ref, y_ref, acc_scr, at_scr, p_scr, uT_scr):
    s = pl.program_id(1)

    @pl.when(s == 0)
    def _():
        acc_scr[...] = jnp.zeros_like(acc_scr)
        uT_scr[...] = u_ref[...].astype(F32).T.astype(BF16)

    tm = u_ref.shape[0]
    key_tiles = PEER_NKEYS // _BF16_ROWS
    rows_per_chunk = _MXU_N // PEER_NKEYS
    n_chunks = ib // rows_per_chunk

    def tile_rows(ref, h, ii, cols):
        group = ref[h, pl.ds(pl.multiple_of(s * ib + (ii // _F32_ROWS) * _F32_ROWS, _F32_ROWS), _F32_ROWS), cols]
        row = group[ii % _F32_ROWS:ii % _F32_ROWS + 1, :]
        tile = jnp.broadcast_to(row, (_BF16_ROWS, _MXU_N)).astype(BF16)
        return jnp.concatenate([tile] * key_tiles, axis=0)

    def weights(ii):
        rows = slice(ii * PEER_NKEYS, (ii + 1) * PEER_NKEYS)
        for c in range(tm // _MXU_N):
            cols = slice(c * _MXU_N, (c + 1) * _MXU_N)
            act = _gelu_tanh(at_scr[rows, cols].astype(BF16))
            w = None
            for h in range(PEER_HEADS):
                r_b = tile_rows(r_ref, h, ii, cols)
                a_b = tile_rows(a1_ref, h, ii, cols)
                term = jnp.where(rank2_ref[h, :, cols] < r_b, e2_ref[h, :, cols] * a_b, jnp.zeros((), BF16))
                w = term if w is None else w + term
            p_scr[rows, cols] = w * act

    for ch in range(n_chunks):
        erows = slice(ch * _MXU_N, (ch + 1) * _MXU_N)
        at_scr[erows, :] = _dot(ut_ref[erows, :], uT_scr[...])
    for ch in range(n_chunks):
        erows = slice(ch * _MXU_N, (ch + 1) * _MXU_N)
        for ii in range(ch * rows_per_chunk, (ch + 1) * rows_per_chunk):
            weights(ii)
        acc_scr[...] += _dot(vt_ref[:, erows], p_scr[erows, :])

    @pl.when(s == n_eblk - 1)
    def _():
        mod = mod_ref[...]
        gate2 = mod[:, 5 * D_MODEL:6 * D_MODEL]
        y = alpha * x1_ref[...] + gate2 * acc_scr[...].T
        y_ref[...] = _layer_norm(y, lnw_ref[...], lnb_ref[...])


def _experts(u2, u_tab, vt_tab, rank2, e2, r, a1, x1, mod, row_fn, lnw, lnb, tm, eb, alpha):
    t = u2.shape[0]
    n_exp = u_tab.shape[0]
    ib = eb // PEER_NKEYS
    n_eblk = n_exp // eb
    full = pl.BlockSpec((PEER_HEADS, PEER_NKEYS, tm), lambda i, s: (0, 0, i))
    tok = pl.BlockSpec((tm, D_MODEL), lambda i, s: (i, 0))
    vec = pl.BlockSpec((1, D_MODEL), lambda i, s: (0, 0))
    return pl.pallas_call(
        functools.partial(_experts_kernel, ib, n_eblk, alpha),
        grid=(t // tm, n_eblk),
        in_specs=[tok,
                  pl.BlockSpec((eb, D_MODEL), lambda i, s: (s, 0)),
                  pl.BlockSpec((D_MODEL, eb), lambda i, s: (0, s)),
                  full, full, full, full, tok,
                  pl.BlockSpec((None, 1, mod.shape[2]), lambda i, s: (row_fn(i), 0, 0)),
                  vec, vec],
        out_specs=tok,
        out_shape=jax.ShapeDtypeStruct((t, D_MODEL), F32),
        scratch_shapes=[pltpu.VMEM((D_MODEL, tm), F32), pltpu.VMEM((eb, tm), F32),
                        pltpu.VMEM((eb, tm), BF16), pltpu.VMEM((D_MODEL, tm), BF16)],
        compiler_params=_params(2),
        name="peer_experts",
    )(u2, u_tab, vt_tab, rank2, e2, r, a1, x1, mod, lnw, lnb)


def _grid_pos_embed(n_tokens):
    rows = n_tokens // GRID_W
    quarter = D_MODEL // 4
    freqs = jnp.exp(-math.log(POS_BASE) * jnp.arange(quarter, dtype=F32) / quarter)
    r = jnp.arange(rows, dtype=F32)[:, None] * freqs
    cl = jnp.arange(GRID_W, dtype=F32)[:, None] * freqs
    er = jnp.concatenate([jnp.sin(r), jnp.cos(r)], -1)
    ec = jnp.concatenate([jnp.sin(cl), jnp.cos(cl)], -1)
    emb = jnp.concatenate([jnp.broadcast_to(er[:, None, :], (rows, GRID_W, D_MODEL // 2)),
                           jnp.broadcast_to(ec[None, :, :], (rows, GRID_W, D_MODEL // 2))], -1)
    return emb.reshape(rows * GRID_W, D_MODEL)


def _pick_tile(seq_len, n_tokens, target):
    tm = min(target, seq_len)
    while seq_len % tm or n_tokens % tm:
        tm //= 2
    return tm


def _block(x, pos, mod, n_seq, seq_len, tm_in, row_of_token_tile, state0, want_state, lw, alpha):
    t = x.shape[0]
    proj, projt, gates, gatest = _inproj(x, pos, mod, row_of_token_tile(tm_in), lw, tm_in)
    res = _mlstm(proj, projt, gates, gatest, n_seq, seq_len, state0, want_state)
    hf, hb = res[0], res[1]
    tm_tail = _pick_tile(seq_len, t, 256)
    x1, u2, q = _tail(hf, hb, proj, projt, x, pos, mod, row_of_token_tile(tm_tail), seq_len, tm_tail, alpha, lw)
    tm_p = _pick_tile(seq_len, t, 512)
    rank2, e2, r, a1 = _route(q, lw["skx"])
    y = _experts(u2, lw["peer_u"], lw["peer_vt"], rank2, e2, r, a1, x1, mod, row_of_token_tile(tm_p),
                 lw["ln2_w"], lw["ln2_b"], tm_p, 2048, alpha)
    return y, res[2:]


def kernel(x_prompt, x_sample, state_C, state_n, state_m, c, c_ctx, w_in, b_in, mlstm_norm_w, w_a, conv_dw_w,
           conv_dw_b, conv_ln_w, conv_ln_b, w_conv_out, w_out, w_mod, b_mod, ln1_w, ln1_b, ln2_w, ln2_b,
           peer_w_query, peer_subkeys, peer_u, peer_v):
    depth = w_in.shape[0]
    alpha = (2.0 * depth) ** 0.25
    bsz, seq, _ = x_prompt.shape
    dbsz, dseq, _ = x_sample.shape
    units = 2 * N_HEADS
    gate_off = 4 * D_MODEL

    ctx = x_prompt.reshape(bsz * seq, D_MODEL)
    lat = x_sample.reshape(dbsz * dseq, D_MODEL)
    pos = _grid_pos_embed(dseq)
    n_rows = 1 + dbsz
    pad_rows = (-n_rows) % 8
    cvec = jnp.concatenate([c_ctx[None, :], c, jnp.zeros((pad_rows, D_MODEL), F32)], axis=0)

    new_c, new_n, new_m = [], [], []
    for l in range(depth):
        vec = lambda a: a[l].reshape(1, -1)
        w_l = w_in[l]
        b_l = b_in[l]
        wg = jnp.pad(w_l[:, gate_off:gate_off + N_GATES], ((0, 0), (0, GATE_PAD - N_GATES)))
        wg_hi = wg.astype(BF16)
        wg_lo = (wg - wg_hi.astype(F32)).astype(BF16)
        b_gate = jnp.pad(b_l[gate_off:gate_off + N_GATES], (0, GATE_PAD - N_GATES))
        w_cols = jnp.concatenate([w_l[:, :gate_off], w_l[:, gate_off + N_GATES:]], axis=1)
        b_cols = jnp.concatenate([b_l[:gate_off], b_l[gate_off + N_GATES:]])
        n_steps = w_cols.shape[1] // D_MODEL
        blocks = [w_cols[:, s * D_MODEL:(s + 1) * D_MODEL] for s in range(n_steps)]
        lanes = lambda col: jnp.broadcast_to(col[:, None], (col.shape[0], _LANES))
        lw = {
            "w_main": jnp.concatenate([blk.T if s in _FEATURE_MAJOR_STEPS else blk for s, blk in enumerate(blocks)],
                                      axis=1).astype(BF16),
            "b_main": b_cols.reshape(1, -1),
            "b_feat": jnp.stack([lanes(b_cols[s * D_MODEL:(s + 1) * D_MODEL]) for s in _FEATURE_MAJOR_STEPS]),
            "wg_hi": wg_hi, "wg_lo": wg_lo, "b_gate": b_gate.reshape(1, -1),
            "norm_w": lanes(mlstm_norm_w[l]), "w_a": w_a[l].astype(BF16), "w_cout": w_conv_out[l].astype(BF16),
            "w_out": w_out[l].astype(BF16),
            "dw_w": jnp.pad(conv_dw_w[l], ((0, 32 - CONV_WIDTH), (0, 0))), "dw_b": vec(conv_dw_b),
            "cln_w": vec(conv_ln_w), "cln_b": vec(conv_ln_b), "ln1_w": vec(ln1_w), "ln1_b": vec(ln1_b),
            "ln2_w": vec(ln2_w), "ln2_b": vec(ln2_b),
            "wq": peer_w_query[l].astype(BF16),
            "skx": jnp.einsum("hpkd,ab->hpkabd", peer_subkeys[l].astype(BF16), jnp.eye(_TOK_BLOCKS, dtype=BF16))
            .reshape(PEER_HEADS, 2, PEER_NKEYS * _TOK_BLOCKS, _TOK_BLOCKS * PEER_NKEYS),
            "peer_u": peer_u[l].astype(BF16),
            "peer_vt": peer_v[l].T.astype(BF16),
        }
        mod = _modulation(cvec, w_mod[l], b_mod[l]).reshape(n_rows + pad_rows, 1, 6 * D_MODEL)

        ctx, (c_fin, n_fin, m_fin) = _block(ctx, None, mod, bsz, seq, _pick_tile(bsz * seq, bsz * seq, 1024),
                                             lambda tm: (lambda i: 0), None, True, lw, alpha)
        new_c.append(c_fin.reshape(bsz, 2, N_HEADS, HEAD_DIM, HEAD_DIM))
        new_n.append(n_fin.reshape(bsz, 2, N_HEADS, HEAD_DIM))
        new_m.append(m_fin[:, :, 0].reshape(bsz, 2, N_HEADS))

        state0 = (state_C[:, l].reshape(dbsz, units, HEAD_DIM, HEAD_DIM),
                  state_n[:, l].reshape(dbsz, units, HEAD_DIM),
                  jnp.broadcast_to(state_m[:, l].reshape(dbsz, units, 1), (dbsz, units, GATE_PAD)))
        lat, _ = _block(lat, pos, mod, dbsz, dseq, _pick_tile(dseq, dbsz * dseq, 1024),
                        lambda tm: (lambda i: 1 + (i * tm) // dseq), state0, False, lw, alpha)

    return (ctx.reshape(bsz, seq, D_MODEL), lat.reshape(dbsz, dseq, D_MODEL),
            jnp.stack(new_c, axis=1), jnp.stack(new_n, axis=1), jnp.stack(new_m, axis=1))
```

```python
import functools
import math

import jax
import jax.numpy as jnp
from jax import lax
from jax.experimental import pallas as pl
from jax.experimental.pallas import tpu as pltpu

F32 = jnp.float32
BF16 = jnp.bfloat16

D_MODEL = 1024
N_HEADS = 4
HEAD_DIM = 256
CHUNK = 256
CONV_WIDTH = 31
CONV_HALO = 16
N_GATES = 16
GATE_PAD = 128
GRID_W = 64
POS_BASE = 10000.0
LN_EPS = 1e-6
PEER_HEADS = 8
PEER_NKEYS = 128
PEER_TOPK = 16
V7X_VMEM_LIMIT = 56 * 1024 * 1024


def _params(n_axes):
    return pltpu.CompilerParams(dimension_semantics=("arbitrary",) * n_axes,
                                vmem_limit_bytes=V7X_VMEM_LIMIT)


def _sigmoid(x):
    return 0.5 + 0.5 * jnp.tanh(0.5 * x)


def _log_sigmoid(x):
    return jnp.minimum(x, 0.0) - jnp.log(1.0 + jnp.exp(-jnp.abs(x)))


def _dot(a, b):
    return jnp.dot(a, b, preferred_element_type=F32)


def _dot_nt(a, b):
    return lax.dot_general(a, b, (((1,), (1,)), ((), ())), preferred_element_type=F32)


def _split2(x):
    hi = x.astype(BF16)
    lo = (x - hi.astype(F32)).astype(BF16)
    return hi, lo


def _split3(x):
    a = x.astype(BF16)
    r = x - a.astype(F32)
    b = r.astype(BF16)
    c = (r - b.astype(F32)).astype(BF16)
    return a, b, c


def _layer_norm(x, w, b):
    mu = jnp.mean(x, axis=-1, keepdims=True)
    xc = x - mu
    var = jnp.mean(xc * xc, axis=-1, keepdims=True)
    return xc * lax.rsqrt(var + LN_EPS) * w + b


def _mod_kernel(c_ref, w_ref, b_ref, o_ref):
    c = c_ref[...]
    s = c * _sigmoid(c)
    sh, sl = _split2(s)
    wh, wl = _split2(w_ref[...])
    o_ref[...] = _dot(sh, wh) + _dot(sl, wh) + _dot(sh, wl) + b_ref[...]


def _modulation(cvec, w_mod, b_mod):
    rows = cvec.shape[0]
    tn = 1536
    n = w_mod.shape[1]
    return pl.pallas_call(
        _mod_kernel,
        grid=(n // tn,),
        in_specs=[pl.BlockSpec((rows, D_MODEL), lambda j: (0, 0)),
                  pl.BlockSpec((D_MODEL, tn), lambda j: (0, j)),
                  pl.BlockSpec((1, tn), lambda j: (0, j))],
        out_specs=pl.BlockSpec((rows, tn), lambda j: (0, j)),
        out_shape=jax.ShapeDtypeStruct((rows, n), F32),
        compiler_params=_params(1),
        name="modulation",
    )(cvec, w_mod, b_mod.reshape(1, n))


_STEP_K = 1
_FEATURE_MAJOR_STEPS = (0, 2, 3)
_N_TOKEN_MAJOR = 5


def _lane_tile(block, n_lanes):
    return jnp.concatenate([block] * (n_lanes // block.shape[1]), axis=1)


def _inproj_kernel(has_pos, *refs):
    if has_pos:
        (x_ref, pos_ref, mod_ref, w_ref, b_ref, bt_ref, wgh_ref, wgl_ref, bg_ref,
         proj_ref, projt_ref, gates_ref, gatest_ref, u_scr) = refs
    else:
        (x_ref, mod_ref, w_ref, b_ref, bt_ref, wgh_ref, wgl_ref, bg_ref,
         proj_ref, projt_ref, gates_ref, gatest_ref, u_scr) = refs
    j = pl.program_id(1)
    tm = u_scr.shape[0]

    @pl.when(j == 0)
    def _():
        x = x_ref[...]
        if has_pos:
            x = x + pos_ref[...]
        mod = mod_ref[...]
        u = x * (1.0 + mod[:, D_MODEL:2 * D_MODEL]) + mod[:, 0:D_MODEL]
        uh, ul = _split2(u)
        u_scr[...] = uh
        wgh = wgh_ref[...]
        gates = _dot(uh, wgh) + _dot(ul, wgh) + _dot(uh, wgl_ref[...]) + bg_ref[...]
        gates_ref[...] = gates
        gatest_ref[...] = gates.T

    is_t = jnp.logical_or(j == 0, jnp.logical_or(j == 2, j == 3))

    @pl.when(is_t)
    def _():
        acc = _dot_nt(w_ref[...], u_scr[...]) + _lane_tile(bt_ref[...], tm)

        @pl.when(j == 3)
        def _():
            projt_ref[...] = _sigmoid(acc.astype(BF16))

        @pl.when(j != 3)
        def _():
            projt_ref[...] = acc.astype(BF16)

    @pl.when(jnp.logical_not(is_t))
    def _():
        acc = _dot(u_scr[...], w_ref[...]) + b_ref[...]

        @pl.when(j == _STEP_K)
        def _():
            proj_ref[...] = (acc * (HEAD_DIM ** -0.5)).astype(BF16)

        @pl.when(j == 4)
        def _():
            proj_ref[...] = acc.astype(BF16)

        @pl.when(j >= 5)
        def _():
            proj_ref[...] = _sigmoid(acc.astype(BF16))


def _inproj(x, pos, mod, row_fn, lw, tm):
    t = x.shape[0]
    n_steps = lw["w_main"].shape[1] // D_MODEL
    has_pos = pos is not None
    in_specs = [pl.BlockSpec((tm, D_MODEL), lambda i, j: (i, 0))]
    args = [x]
    if has_pos:
        pos_blocks = pos.shape[0] // tm
        in_specs.append(pl.BlockSpec((tm, D_MODEL), lambda i, j: (i % pos_blocks, 0)))
        args.append(pos)

    def token_major_block(j):
        return jnp.maximum(j - 3, 0)

    def feature_major_block(j):
        return jnp.clip(j - 1, 0, 2)

    in_specs += [
        pl.BlockSpec((None, 1, mod.shape[2]), lambda i, j: (row_fn(i), 0, 0)),
        pl.BlockSpec((D_MODEL, D_MODEL), lambda i, j: (0, j)),
        pl.BlockSpec((1, D_MODEL), lambda i, j: (0, j)),
        pl.BlockSpec((None, D_MODEL, _LANES), lambda i, j: (feature_major_block(j), 0, 0)),
        pl.BlockSpec((D_MODEL, GATE_PAD), lambda i, j: (0, 0)),
        pl.BlockSpec((D_MODEL, GATE_PAD), lambda i, j: (0, 0)),
        pl.BlockSpec((1, GATE_PAD), lambda i, j: (0, 0)),
    ]
    args += [mod, lw["w_main"], lw["b_main"], lw["b_feat"], lw["wg_hi"], lw["wg_lo"], lw["b_gate"]]
    return pl.pallas_call(
        functools.partial(_inproj_kernel, has_pos),
        grid=(t // tm, n_steps),
        in_specs=in_specs,
        out_specs=[pl.BlockSpec((tm, D_MODEL), lambda i, j: (i, token_major_block(j))),
                   pl.BlockSpec((D_MODEL, tm), lambda i, j: (feature_major_block(j), i)),
                   pl.BlockSpec((tm, GATE_PAD), lambda i, j: (i, 0)),
                   pl.BlockSpec((GATE_PAD, tm), lambda i, j: (0, i))],
        out_shape=[jax.ShapeDtypeStruct((t, _N_TOKEN_MAJOR * D_MODEL), BF16),
                   jax.ShapeDtypeStruct((len(_FEATURE_MAJOR_STEPS) * D_MODEL, t), BF16),
                   jax.ShapeDtypeStruct((t, GATE_PAD), F32),
                   jax.ShapeDtypeStruct((GATE_PAD, t), F32)],
        scratch_shapes=[pltpu.VMEM((tm, D_MODEL), BF16)],
        compiler_params=_params(2),
        name="inproj",
    )(*args)


_SEQ_INPUTS = 10


def _mlstm_kernel(has_state, want_state, nc, n_par, *refs):
    refs = list(refs)
    seq_in = [refs[a * _SEQ_INPUTS:(a + 1) * _SEQ_INPUTS] for a in range(n_par)]
    pos = n_par * _SEQ_INPUTS
    if has_state:
        c0_ref, n0_ref, m0_ref = refs[pos:pos + 3]
        pos += 3
    hf_ref, hb_ref = refs[pos:pos + 2]
    pos += 2
    if want_state:
        co_ref, no_ref, mo_ref = refs[pos:pos + 3]
        pos += 3
    c_all, n_all, m_all = refs[pos:pos + 3]
    step = pl.program_id(1)

    @pl.when(step == 0)
    def _():
        if has_state:
            c_all[...] = c0_ref[...]
            n_all[...] = n0_ref[...]
            m_all[...] = m0_ref[...]
        else:
            c_all[...] = jnp.zeros_like(c_all)
            n_all[...] = jnp.zeros_like(n_all)
            m_all[...] = jnp.zeros_like(m_all)

    row = lax.broadcasted_iota(jnp.int32, (CHUNK, CHUNK), 0)
    col = lax.broadcasted_iota(jnp.int32, (CHUNK, CHUNK), 1)

    streams = []
    for a in range(n_par):
        qf, kf, vf, qb, kb, vb, gf, gb, gtf, gtb = seq_in[a]
        streams.append((a, 0, qf, kf, vf, gf, gtf, hf_ref))
        streams.append((a, 1, qb, kb, vb, gb, gtb, hb_ref))
    for a, d, qt_ref, k_ref, vt_ref, g_ref, gt_ref, h_ref in streams:
        c_scr, n_scr, m_scr = c_all.at[a], n_all.at[a], m_all.at[a]
        visible = (row <= col) if d == 0 else (row >= col)
        tri_t = jnp.where(visible, 1.0, 0.0).astype(BF16)
        tri = jnp.where((col <= row) if d == 0 else (col >= row), 1.0, 0.0).astype(BF16)
        g = g_ref[...]
        g_t = gt_ref[...]
        l1, l2, l3 = _split3(_log_sigmoid(g))
        b_col_all = _dot(tri, l1) + _dot(tri, l2) + _dot(tri, l3)
        t1, t2, t3 = _split3(_log_sigmoid(g_t))
        b_row_all = _dot(t1, tri_t) + _dot(t2, tri_t) + _dot(t3, tri_t)
        last = CHUNK - 1 if d == 0 else 0
        for h in range(N_HEADS):
            u = d * N_HEADS + h
            ci = d * 2 * N_HEADS + h
            cf = ci + N_HEADS
            b_row = b_row_all[cf:cf + 1, :]
            i_row = g_t[ci:ci + 1, :]
            src_col = g[:, ci:ci + 1] - b_col_all[:, cf:cf + 1]
            m = m_scr[u:u + 1, 0:1]
            hs = slice(h * HEAD_DIM, (h + 1) * HEAD_DIM)
            q_t = qt_ref[hs, :]
            k = k_ref[:, hs]
            v_t = vt_ref[hs, :]
            c_state = c_scr[u]
            n_state = n_scr[u:u + 1, :]

            dmat_t = jnp.where(visible, src_col + b_row, -jnp.inf)
            inter = b_row + m
            m_t = jnp.maximum(inter, jnp.max(dmat_t, axis=0, keepdims=True))
            w_st = jnp.exp(dmat_t - m_t)
            a_t = jnp.exp(inter - m_t)
            s_t = _dot(k, q_t) * w_st
            num_t = _dot(v_t, s_t.astype(BF16)) + a_t * _dot(c_state.astype(BF16), q_t)
            n_rows = jnp.broadcast_to(n_state, (_F32_ROWS, HEAD_DIM)).astype(BF16)
            den = jnp.sum(s_t, axis=0, keepdims=True) + a_t * _dot(n_rows, q_t)[0:1, :]
            h_ref[hs, a * CHUNK:(a + 1) * CHUNK] = num_t / jnp.maximum(jnp.abs(den), jnp.exp(-m_t))

            b_last = b_row[:, last:last + 1]
            dec_row = b_last - b_row + i_row
            m_new = jnp.maximum(b_last + m, jnp.max(dec_row, axis=-1, keepdims=True))
            w_row = jnp.exp(dec_row - m_new)
            a_c = jnp.exp(b_last + m - m_new)
            vw_t = (v_t.astype(F32) * w_row).astype(BF16)
            c_scr[u] = a_c * c_state + _dot(vw_t, k)
            w_rows = jnp.broadcast_to(w_row, (_F32_ROWS, CHUNK)).astype(BF16)
            n_scr[u:u + 1, :] = a_c * n_state + _dot(w_rows, k)[0:1, :]
            m_scr[u:u + 1, :] = jnp.broadcast_to(m_new, (1, GATE_PAD))

    if want_state:
        @pl.when(step == nc - 1)
        def _():
            co_ref[...] = c_all[...]
            no_ref[...] = n_all[...]
            mo_ref[...] = m_all[...]


def _mlstm(proj, projt, gates, gatest, n_seq, seq_len, state0, want_state):
    t = proj.shape[0]
    nc = seq_len // CHUNK
    units = 2 * N_HEADS
    has_state = state0 is not None

    n_par = _scan_group(n_seq)
    tile = (CHUNK, D_MODEL)
    tile_t = (D_MODEL, CHUNK)
    in_specs, args = [], []
    for a in range(n_par):
        def fwd(c, a=a):
            return lambda p, k: ((p * n_par + a) * nc + k, c)

        def bwd(c, a=a):
            return lambda p, k: ((p * n_par + a) * nc + nc - 1 - k, c)

        def fwd_t(r, a=a):
            return lambda p, k: (r, (p * n_par + a) * nc + k)

        def bwd_t(r, a=a):
            return lambda p, k: (r, (p * n_par + a) * nc + nc - 1 - k)

        in_specs += [pl.BlockSpec(tile_t, fwd_t(0)), pl.BlockSpec(tile, fwd(0)), pl.BlockSpec(tile_t, fwd_t(1)),
                     pl.BlockSpec(tile_t, bwd_t(0)), pl.BlockSpec(tile, bwd(0)), pl.BlockSpec(tile_t, bwd_t(1)),
                     pl.BlockSpec((CHUNK, GATE_PAD), fwd(0)), pl.BlockSpec((CHUNK, GATE_PAD), bwd(0)),
                     pl.BlockSpec((GATE_PAD, CHUNK), fwd_t(0)), pl.BlockSpec((GATE_PAD, CHUNK), bwd_t(0))]
        args += [projt, proj, projt, projt, proj, projt, gates, gates, gatest, gatest]
    state_specs = [pl.BlockSpec((n_par, units, HEAD_DIM, HEAD_DIM), lambda p, k: (p, 0, 0, 0)),
                   pl.BlockSpec((n_par, units, HEAD_DIM), lambda p, k: (p, 0, 0)),
                   pl.BlockSpec((n_par, units, GATE_PAD), lambda p, k: (p, 0, 0))]
    state_shapes = [jax.ShapeDtypeStruct((n_seq, units, HEAD_DIM, HEAD_DIM), F32),
                    jax.ShapeDtypeStruct((n_seq, units, HEAD_DIM), F32),
                    jax.ShapeDtypeStruct((n_seq, units, GATE_PAD), F32)]
    if has_state:
        in_specs += state_specs
        args += list(state0)
    group_tile = (D_MODEL, n_par * CHUNK)
    out_specs = [pl.BlockSpec(group_tile, lambda p, k: (0, p * nc + k)),
                 pl.BlockSpec(group_tile, lambda p, k: (0, p * nc + nc - 1 - k))]
    out_shape = [jax.ShapeDtypeStruct((D_MODEL, t), F32), jax.ShapeDtypeStruct((D_MODEL, t), F32)]
    if want_state:
        out_specs += state_specs
        out_shape += state_shapes
    return pl.pallas_call(
        functools.partial(_mlstm_kernel, has_state, want_state, nc, n_par),
        grid=(n_seq // n_par, nc),
        in_specs=in_specs,
        out_specs=out_specs,
        out_shape=out_shape,
        scratch_shapes=[pltpu.VMEM((n_par, units, HEAD_DIM, HEAD_DIM), F32),
                        pltpu.VMEM((n_par, units, HEAD_DIM), F32),
                        pltpu.VMEM((n_par, units, GATE_PAD), F32)],
        compiler_params=_params(2),
        name="mlstm",
    )(*args)


def _scan_group(n_seq):
    return 2 if n_seq % 2 == 0 else 1


def _scan_column(seq, chunk, nc, n_par):
    return ((seq // n_par) * nc + chunk) * n_par + seq % n_par


_CONV_ROWS = 64
_LANES = 128


def _tail_kernel(has_pos, tm, tiles_per_seq, alpha, *refs):
    refs = list(refs)
    n_ct = tm // CHUNK
    h_refs = refs[:2 * n_ct]
    so_ref, val_ref, sg_ref, sga_ref, sgb_ref, vp_ref, gp_ref, vn_ref, gn_ref, x_ref = refs[2 * n_ct:2 * n_ct + 10]
    pos = 2 * n_ct + 10
    if has_pos:
        pos_ref = refs[pos]
        pos += 1
    (mod_ref, normw_ref, wa_ref, wc_ref, wo_ref, dww_ref, dwb_ref, clnw_ref, clnb_ref, ln1w_ref, ln1b_ref, wq_ref,
     x1_ref, u2_ref, q_ref, xpad, conv_scr, shifted) = refs[pos:]
    i = pl.program_id(0)

    hsum = jnp.concatenate([h_refs[j][...] + h_refs[n_ct + j][...] for j in range(n_ct)], axis=1)
    parts = []
    for h in range(N_HEADS):
        hh = hsum[h * HEAD_DIM:(h + 1) * HEAD_DIM, :]
        mu = jnp.mean(hh, axis=0, keepdims=True)
        hc = hh - mu
        var = jnp.mean(hc * hc, axis=0, keepdims=True)
        parts.append(hc * lax.rsqrt(var + LN_EPS))
    hn = jnp.concatenate(parts, axis=0) * _lane_tile(normw_ref[...], tm)
    hg = (so_ref[...].astype(F32) * hn).T.astype(BF16)
    branch_a = _dot(hg, wa_ref[...])

    first = (i % tiles_per_seq) == 0
    last = (i % tiles_per_seq) == tiles_per_seq - 1
    keep_prev = jnp.where(first, 0.0, 1.0)
    keep_next = jnp.where(last, 0.0, 1.0)
    xpad[0:CONV_HALO, :] = vp_ref[...].astype(F32) * gp_ref[...].astype(F32) * keep_prev
    xpad[CONV_HALO:CONV_HALO + tm, :] = val_ref[...].astype(F32) * sg_ref[...].astype(F32)
    xpad[CONV_HALO + tm:2 * CONV_HALO + tm, :] = vn_ref[...].astype(F32) * gn_ref[...].astype(F32) * keep_next
    tap0 = CONV_HALO - CONV_WIDTH // 2

    n_shift_rows = tm + 2 * CONV_HALO - _F32_ROWS

    def col_body(c, carry):
        cs = pl.ds(pl.multiple_of(c * _LANES, _LANES), _LANES)
        for r in range(1, _F32_ROWS):
            shifted[r - 1, 0:n_shift_rows, :] = xpad[pl.ds(r, n_shift_rows), cs]
        for rb in range(tm // _CONV_ROWS):
            acc = jnp.broadcast_to(dwb_ref[:, cs], (_CONV_ROWS, _LANES))
            for k in range(CONV_WIDTH):
                tiles, r = divmod(tap0 + k, _F32_ROWS)
                start = tiles * _F32_ROWS + rb * _CONV_ROWS
                if r == 0:
                    src = xpad[pl.ds(start, _CONV_ROWS), cs]
                else:
                    src = shifted[r - 1, pl.ds(start, _CONV_ROWS), :]
                acc = acc + src * dww_ref[k:k + 1, cs]
            conv_scr[pl.ds(rb * _CONV_ROWS, _CONV_ROWS), cs] = acc
        return carry

    lax.fori_loop(0, D_MODEL // _LANES, col_body, 0)
    xc = _layer_norm(conv_scr[...], clnw_ref[...], clnb_ref[...])
    xc = (xc * _sigmoid(xc)).astype(BF16)
    branch_b = _dot(xc, wc_ref[...])

    merged = sga_ref[...].astype(F32) * branch_a + sgb_ref[...].astype(F32) * branch_b
    mix = _dot(merged.astype(BF16), wo_ref[...])

    mod = mod_ref[...]
    gate1 = mod[:, 2 * D_MODEL:3 * D_MODEL]
    shift2 = mod[:, 3 * D_MODEL:4 * D_MODEL]
    scale2 = mod[:, 4 * D_MODEL:5 * D_MODEL]
    x = x_ref[...]
    if has_pos:
        x = x + pos_ref[...]
    x1 = _layer_norm(alpha * x + gate1 * mix, ln1w_ref[...], ln1b_ref[...])
    x1_ref[...] = x1
    u2 = (x1 * (1.0 + scale2) + shift2).astype(BF16)
    u2_ref[...] = u2
    q_ref[...] = _dot(u2, wq_ref[...]).astype(BF16)


def _tail(hf, hb, proj, projt, x, pos, mod, row_fn, seq_len, tm, alpha, lw):
    t = x.shape[0]
    has_pos = pos is not None
    tiles_per_seq = seq_len // tm
    hb_per_tile = tm // CONV_HALO
    n_halo = t // CONV_HALO
    big = (tm, D_MODEL)
    halo = (CONV_HALO, D_MODEL)

    def colspec(c):
        return pl.BlockSpec(big, lambda i: (i, c))

    def prev(c):
        return pl.BlockSpec(halo, lambda i: (jnp.maximum(i * hb_per_tile - 1, 0), c))

    def nxt(c):
        return pl.BlockSpec(halo, lambda i: (jnp.minimum((i + 1) * hb_per_tile, n_halo - 1), c))

    def const(shape):
        return pl.BlockSpec(shape, lambda i: (0,) * len(shape), pipeline_mode=pl.Buffered(1))

    def rowspec(r):
        return pl.BlockSpec((D_MODEL, tm), lambda i: (r, i))

    nc = seq_len // CHUNK
    n_par = _scan_group(t // seq_len)
    n_ct = tm // CHUNK

    def scan_chunk(j):
        return pl.BlockSpec((D_MODEL, CHUNK), lambda i: (
            0, _scan_column(i // tiles_per_seq, (i % tiles_per_seq) * n_ct + j, nc, n_par)))

    in_specs = [scan_chunk(j) for j in range(n_ct)] * 2
    in_specs += [rowspec(2), colspec(1), colspec(2), colspec(3), colspec(4),
                 prev(1), prev(2), nxt(1), nxt(2), colspec(0)]
    args = [hf] * n_ct + [hb] * n_ct + [projt, proj, proj, proj, proj, proj, proj, proj, proj, x]
    if has_pos:
        in_specs.append(pl.BlockSpec(big, lambda i: (i % tiles_per_seq, 0)))
        args.append(pos)
    in_specs += [pl.BlockSpec((None, 1, mod.shape[2]), lambda i: (row_fn(i), 0, 0)),
                 const((D_MODEL, _LANES)), const((D_MODEL, D_MODEL)), const((D_MODEL, D_MODEL)),
                 const((D_MODEL, D_MODEL)),
                 const((32, D_MODEL)), const((1, D_MODEL)), const((1, D_MODEL)), const((1, D_MODEL)),
                 const((1, D_MODEL)), const((1, D_MODEL)), const(lw["wq"].shape)]
    args += [mod, lw["norm_w"], lw["w_a"], lw["w_cout"], lw["w_out"], lw["dw_w"], lw["dw_b"], lw["cln_w"],
             lw["cln_b"], lw["ln1_w"], lw["ln1_b"], lw["wq"]]
    n_q = lw["wq"].shape[1]
    return pl.pallas_call(
        functools.partial(_tail_kernel, has_pos, tm, tiles_per_seq, alpha),
        grid=(t // tm,),
        in_specs=in_specs,
        out_specs=[pl.BlockSpec(big, lambda i: (i, 0)), pl.BlockSpec(big, lambda i: (i, 0)),
                   pl.BlockSpec((tm, n_q), lambda i: (i, 0))],
        out_shape=[jax.ShapeDtypeStruct((t, D_MODEL), F32), jax.ShapeDtypeStruct((t, D_MODEL), BF16),
                   jax.ShapeDtypeStruct((t, n_q), BF16)],
        scratch_shapes=[pltpu.VMEM((tm + 2 * CONV_HALO, D_MODEL), F32), pltpu.VMEM((tm, D_MODEL), F32),
                        pltpu.VMEM((_F32_ROWS - 1, tm + 2 * CONV_HALO, _LANES), F32)],
        compiler_params=_params(1),
        name="mixer_tail",
    )(*args)


_TOK_BLOCKS = 8
_LANE = 128
_ROUTE_TM = _TOK_BLOCKS * _LANE


def _merge_exchange_pairs(n):
    pairs = []
    t = (n - 1).bit_length()
    p = 1 << (t - 1)
    while p > 0:
        q, r, d = 1 << (t - 1), 0, p
        while d > 0:
            pairs.extend((i, i + d) for i in range(n - d) if (i & p) == r)
            d, q, r = q - p, q >> 1, p
        p >>= 1
    return tuple(pairs)


_SORT16 = _merge_exchange_pairs(PEER_TOPK)


def _sort_desc(vals):
    vals = list(vals)
    for i, j in _SORT16:
        vals[i], vals[j] = jnp.maximum(vals[i], vals[j]), jnp.minimum(vals[i], vals[j])
    return vals


def _bitonic_desc(vals):
    vals = list(vals)
    d = len(vals) // 2
    while d > 0:
        for i in range(len(vals)):
            if (i & d) == 0:
                vals[i], vals[i + d] = jnp.maximum(vals[i], vals[i + d]), jnp.minimum(vals[i], vals[i + d])
        d //= 2
    return vals


def _top_merge(a, b):
    n = len(a)
    return _bitonic_desc([jnp.maximum(a[i], b[n - 1 - i]) for i in range(n)])


def _top16(keys):
    if len(keys) == PEER_TOPK:
        return _sort_desc(keys)
    half = len(keys) // 2
    return _top_merge(_top16(keys[:half]), _top16(keys[half:]))


def _prefix_count(test, v):
    t8 = test(v[7])
    t4 = test(jnp.where(t8, v[11], v[3]))
    t2 = test(jnp.where(t8, jnp.where(t4, v[13], v[9]), jnp.where(t4, v[5], v[1])))
    lo = jnp.where(t4, jnp.where(t2, v[6], v[4]), jnp.where(t2, v[2], v[0]))
    hi = jnp.where(t4, jnp.where(t2, v[14], v[12]), jnp.where(t2, v[10], v[8]))
    t1 = test(jnp.where(t8, hi, lo))
    cnt = (jnp.where(t8, 8.0, 0.0) + jnp.where(t4, 4.0, 0.0)) + (jnp.where(t2, 2.0, 0.0) + jnp.where(t1, 1.0, 0.0))
    return jnp.where(test(v[15]), 16.0, cnt)


def _joint_top16(a, b):
    q0 = [a[p] + b[0] for p in range(16)]
    q1 = [a[p] + b[1] for p in range(8)]
    p0 = [a[0] + b[q] for q in range(8, 16)]
    m1 = _bitonic_desc(q1 + p0[::-1])
    m2 = _sort_desc([a[p] + b[q] for q, n in ((2, 5), (3, 4), (4, 3), (5, 2), (6, 2)) for p in range(n)])
    q7 = [a[0] + b[7], a[1] + b[7]]
    t2 = _bitonic_desc(m2[:14] + [jnp.maximum(m2[14], q7[1]), jnp.maximum(m2[15], q7[0])])
    return _top_merge(_top_merge(q0, m1), t2)


def _route_kernel(q_ref, skx_ref, rank2_ref, e2_ref, r_ref, a1_ref, km_rank2, km_e2, km_r, km_a1):
    nb = _TOK_BLOCKS
    keys = []
    for p in range(2):
        qp = jnp.concatenate([q_ref[a * _LANE:(a + 1) * _LANE, p * PEER_NKEYS:(p + 1) * PEER_NKEYS]
                              for a in range(nb)], axis=1)
        sp = _dot_nt(skx_ref[p], qp)
        keys.append([sp[k * nb:(k + 1) * nb, :] for k in range(PEER_NKEYS)])
    s1, s2 = keys
    a = _top16(s1)
    b = _top16(s2)
    top = _joint_top16(a, b)
    tau = top[PEER_TOPK - 1]
    zsum = jnp.ones_like(tau)
    for c in top[1:]:
        zsum = zsum + jnp.exp(c - top[0])
    inv_z = 1.0 / zsum
    b_asc = b[::-1]
    for k in range(PEER_NKEYS):
        rows = slice(k * nb, (k + 1) * nb)
        km_r[rows, :] = _prefix_count(lambda bq, x=s1[k]: x + bq >= tau, b)
        km_a1[rows, :] = jnp.exp(s1[k] - a[0]) * inv_z
        km_rank2[rows, :] = float(PEER_TOPK) - _prefix_count(lambda bq, x=s2[k]: x >= bq, b_asc)
        km_e2[rows, :] = jnp.exp(s2[k] - b[0])
    for blk in range(nb):
        cols = slice(blk * _LANE, (blk + 1) * _LANE)
        rows = pl.ds(blk, PEER_NKEYS, stride=nb)
        rank2_ref[:, cols] = km_rank2[rows, :].astype(BF16)
        e2_ref[:, cols] = km_e2[rows, :].astype(BF16)
        r_ref[:, cols] = km_r[rows, :]
        a1_ref[:, cols] = km_a1[rows, :]


def _route(q, skx):
    t = q.shape[0]
    tm = _ROUTE_TM
    assert t % tm == 0, (t, tm)
    out_blk = pl.BlockSpec((None, PEER_NKEYS, tm), lambda h, i: (h, 0, i))
    shape = (PEER_HEADS, PEER_NKEYS, t)
    km = pltpu.VMEM((PEER_NKEYS * _TOK_BLOCKS, _LANE), F32)
    return pl.pallas_call(
        _route_kernel,
        grid=(PEER_HEADS, t // tm),
        in_specs=[pl.BlockSpec((tm, 2 * PEER_NKEYS), lambda h, i: (i, h)),
                  pl.BlockSpec((None, 2, PEER_NKEYS * _TOK_BLOCKS, _TOK_BLOCKS * PEER_NKEYS),
                               lambda h, i: (h, 0, 0, 0))],
        out_specs=[out_blk, out_blk, out_blk, out_blk],
        out_shape=[jax.ShapeDtypeStruct(shape, BF16), jax.ShapeDtypeStruct(shape, BF16),
                   jax.ShapeDtypeStruct(shape, F32), jax.ShapeDtypeStruct(shape, F32)],
        scratch_shapes=[km, km, km, km],
        compiler_params=_params(2),
        name="peer_route",
    )(q, skx)


_GELU_C = math.sqrt(2.0 / math.pi)
_BF16_ROWS = 16
_F32_ROWS = 8
_MXU_N = 256


def _gelu_tanh(x):
    neg2z = x * (x * x * (-2.0 * _GELU_C * 0.044715) + (-2.0 * _GELU_C))
    return x / (1.0 + jnp.exp(neg2z))


def _experts_kernel(ib, n_eblk, alpha, u_ref, ut_ref, vt_ref, rank2_ref, e2_ref, r_ref, a1_ref, x1_ref, mod_ref,
                    lnw_ref, lnb_ref, y_ref, acc_scr, at_scr, p_scr, uT_scr):
    s = pl.program_id(1)

    @pl.when(s == 0)
    def _():
        acc_scr[...] = jnp.zeros_like(acc_scr)
        uT_scr[...] = u_ref[...].astype(F32).T.astype(BF16)

    tm = u_ref.shape[0]
    key_tiles = PEER_NKEYS // _BF16_ROWS
    rows_per_chunk = _MXU_N // PEER_NKEYS
    n_chunks = ib // rows_per_chunk

    def tile_rows(ref, h, ii, cols):
        group = ref[h, pl.ds(pl.multiple_of(s * ib + (ii // _F32_ROWS) * _F32_ROWS, _F32_ROWS), _F32_ROWS), cols]
        row = group[ii % _F32_ROWS:ii % _F32_ROWS + 1, :]
        tile = jnp.broadcast_to(row, (_BF16_ROWS, _MXU_N)).astype(BF16)
        return jnp.concatenate([tile] * key_tiles, axis=0)

    def weights(ii):
        rows = slice(ii * PEER_NKEYS, (ii + 1) * PEER_NKEYS)
        for c in range(tm // _MXU_N):
            cols = slice(c * _MXU_N, (c + 1) * _MXU_N)
            act = _gelu_tanh(at_scr[rows, cols].astype(BF16))
            w = None
            for h in range(PEER_HEADS):
                r_b = tile_rows(r_ref, h, ii, cols)
                a_b = tile_rows(a1_ref, h, ii, cols)
                term = jnp.where(rank2_ref[h, :, cols] < r_b, e2_ref[h, :, cols] * a_b, jnp.zeros((), BF16))
                w = term if w is None else w + term
            p_scr[rows, cols] = w * act

    for ch in range(n_chunks):
        erows = slice(ch * _MXU_N, (ch + 1) * _MXU_N)
        at_scr[erows, :] = _dot(ut_ref[erows, :], uT_scr[...])
    for ch in range(n_chunks):
        erows = slice(ch * _MXU_N, (ch + 1) * _MXU_N)
        for ii in range(ch * rows_per_chunk, (ch + 1) * rows_per_chunk):
            weights(ii)
        acc_scr[...] += _dot(vt_ref[:, erows], p_scr[erows, :])

    @pl.when(s == n_eblk - 1)
    def _():
        mod = mod_ref[...]
        gate2 = mod[:, 5 * D_MODEL:6 * D_MODEL]
        y = alpha * x1_ref[...] + gate2 * acc_scr[...].T
        y_ref[...] = _layer_norm(y, lnw_ref[...], lnb_ref[...])


def _experts(u2, u_tab, vt_tab, rank2, e2, r, a1, x1, mod, row_fn, lnw, lnb, tm, eb, alpha):
    t = u2.shape[0]
    n_exp = u_tab.shape[0]
    ib = eb // PEER_NKEYS
    n_eblk = n_exp // eb
    full = pl.BlockSpec((PEER_HEADS, PEER_NKEYS, tm), lambda i, s: (0, 0, i))
    tok = pl.BlockSpec((tm, D_MODEL), lambda i, s: (i, 0))
    vec = pl.BlockSpec((1, D_MODEL), lambda i, s: (0, 0))
    return pl.pallas_call(
        functools.partial(_experts_kernel, ib, n_eblk, alpha),
        grid=(t // tm, n_eblk),
        in_specs=[tok,
                  pl.BlockSpec((eb, D_MODEL), lambda i, s: (s, 0)),
                  pl.BlockSpec((D_MODEL, eb), lambda i, s: (0, s)),
                  full, full, full, full, tok,
                  pl.BlockSpec((None, 1, mod.shape[2]), lambda i, s: (row_fn(i), 0, 0)),
                  vec, vec],
        out_specs=tok,
        out_shape=jax.ShapeDtypeStruct((t, D_MODEL), F32),
        scratch_shapes=[pltpu.VMEM((D_MODEL, tm), F32), pltpu.VMEM((eb, tm), F32),
                        pltpu.VMEM((eb, tm), BF16), pltpu.VMEM((D_MODEL, tm), BF16)],
        compiler_params=_params(2),
        name="peer_experts",
    )(u2, u_tab, vt_tab, rank2, e2, r, a1, x1, mod, lnw, lnb)


def _grid_pos_embed(n_tokens):
    rows = n_tokens // GRID_W
    quarter = D_MODEL // 4
    freqs = jnp.exp(-math.log(POS_BASE) * jnp.arange(quarter, dtype=F32) / quarter)
    r = jnp.arange(rows, dtype=F32)[:, None] * freqs
    cl = jnp.arange(GRID_W, dtype=F32)[:, None] * freqs
    er = jnp.concatenate([jnp.sin(r), jnp.cos(r)], -1)
    ec = jnp.concatenate([jnp.sin(cl), jnp.cos(cl)], -1)
    emb = jnp.concatenate([jnp.broadcast_to(er[:, None, :], (rows, GRID_W, D_MODEL // 2)),
                           jnp.broadcast_to(ec[None, :, :], (rows, GRID_W, D_MODEL // 2))], -1)
    return emb.reshape(rows * GRID_W, D_MODEL)


def _pick_tile(seq_len, n_tokens, target):
    tm = min(target, seq_len)
    while seq_len % tm or n_tokens % tm:
        tm //= 2
    return tm


def _block(x, pos, mod, n_seq, seq_len, tm_in, row_of_token_tile, state0, want_state, lw, alpha):
    t = x.shape[0]
    proj, projt, gates, gatest = _inproj(x, pos, mod, row_of_token_tile(tm_in), lw, tm_in)
    res = _mlstm(proj, projt, gates, gatest, n_seq, seq_len, state0, want_state)
    hf, hb = res[0], res[1]
    tm_tail = _pick_tile(seq_len, t, 256)
    x1, u2, q = _tail(hf, hb, proj, projt, x, pos, mod, row_of_token_tile(tm_tail), seq_len, tm_tail, alpha, lw)
    tm_p = _pick_tile(seq_len, t, 512)
    rank2, e2, r, a1 = _route(q, lw["skx"])
    y = _experts(u2, lw["peer_u"], lw["peer_vt"], rank2, e2, r, a1, x1, mod, row_of_token_tile(tm_p),
                 lw["ln2_w"], lw["ln2_b"], tm_p, 2048, alpha)
    return y, res[2:]


def kernel(x_prompt, x_sample, state_C, state_n, state_m, c, c_ctx, w_in, b_in, mlstm_norm_w, w_a, conv_dw_w,
           conv_dw_b, conv_ln_w, conv_ln_b, w_conv_out, w_out, w_mod, b_mod, ln1_w, ln1_b, ln2_w, ln2_b,
           peer_w_query, peer_subkeys, peer_u, peer_v):
    depth = w_in.shape[0]
    alpha = (2.0 * depth) ** 0.25
    bsz, seq, _ = x_prompt.shape
    dbsz, dseq, _ = x_sample.shape
    units = 2 * N_HEADS
    gate_off = 4 * D_MODEL

    ctx = x_prompt.reshape(bsz * seq, D_MODEL)
    lat = x_sample.reshape(dbsz * dseq, D_MODEL)
    pos = _grid_pos_embed(dseq)
    n_rows = 1 + dbsz
    pad_rows = (-n_rows) % 8
    cvec = jnp.concatenate([c_ctx[None, :], c, jnp.zeros((pad_rows, D_MODEL), F32)], axis=0)

    new_c, new_n, new_m = [], [], []
    for l in range(depth):
        vec = lambda a: a[l].reshape(1, -1)
        w_l = w_in[l]
        b_l = b_in[l]
        wg = jnp.pad(w_l[:, gate_off:gate_off + N_GATES], ((0, 0), (0, GATE_PAD - N_GATES)))
        wg_hi = wg.astype(BF16)
        wg_lo = (wg - wg_hi.astype(F32)).astype(BF16)
        b_gate = jnp.pad(b_l[gate_off:gate_off + N_GATES], (0, GATE_PAD - N_GATES))
        w_cols = jnp.concatenate([w_l[:, :gate_off], w_l[:, gate_off + N_GATES:]], axis=1)
        b_cols = jnp.concatenate([b_l[:gate_off], b_l[gate_off + N_GATES:]])
        n_steps = w_cols.shape[1] // D_MODEL
        blocks = [w_cols[:, s * D_MODEL:(s + 1) * D_MODEL] for s in range(n_steps)]
        lanes = lambda col: jnp.broadcast_to(col[:, None], (col.shape[0], _LANES))
        lw = {
            "w_main": jnp.concatenate([blk.T if s in _FEATURE_MAJOR_STEPS else blk for s, blk in enumerate(blocks)],
                                      axis=1).astype(BF16),
            "b_main": b_cols.reshape(1, -1),
            "b_feat": jnp.stack([lanes(b_cols[s * D_MODEL:(s + 1) * D_MODEL]) for s in _FEATURE_MAJOR_STEPS]),
            "wg_hi": wg_hi, "wg_lo": wg_lo, "b_gate": b_gate.reshape(1, -1),
            "norm_w": lanes(mlstm_norm_w[l]), "w_a": w_a[l].astype(BF16), "w_cout": w_conv_out[l].astype(BF16),
            "w_out": w_out[l].astype(BF16),
            "dw_w": jnp.pad(conv_dw_w[l], ((0, 32 - CONV_WIDTH), (0, 0))), "dw_b": vec(conv_dw_b),
            "cln_w": vec(conv_ln_w), "cln_b": vec(conv_ln_b), "ln1_w": vec(ln1_w), "ln1_b": vec(ln1_b),
            "ln2_w": vec(ln2_w), "ln2_b": vec(ln2_b),
            "wq": peer_w_query[l].astype(BF16),
            "skx": jnp.einsum("hpkd,ab->hpkabd", peer_subkeys[l].astype(BF16), jnp.eye(_TOK_BLOCKS, dtype=BF16))
            .reshape(PEER_HEADS, 2, PEER_NKEYS * _TOK_BLOCKS, _TOK_BLOCKS * PEER_NKEYS),
            "peer_u": peer_u[l].astype(BF16),
            "peer_vt": peer_v[l].T.astype(BF16),
        }
        mod = _modulation(cvec, w_mod[l], b_mod[l]).reshape(n_rows + pad_rows, 1, 6 * D_MODEL)

        ctx, (c_fin, n_fin, m_fin) = _block(ctx, None, mod, bsz, seq, _pick_tile(bsz * seq, bsz * seq, 1024),
                                             lambda tm: (lambda i: 0), None, True, lw, alpha)
        new_c.append(c_fin.reshape(bsz, 2, N_HEADS, HEAD_DIM, HEAD_DIM))
        new_n.append(n_fin.reshape(bsz, 2, N_HEADS, HEAD_DIM))
        new_m.append(m_fin[:, :, 0].reshape(bsz, 2, N_HEADS))

        state0 = (state_C[:, l].reshape(dbsz, units, HEAD_DIM, HEAD_DIM),
                  state_n[:, l].reshape(dbsz, units, HEAD_DIM),
                  jnp.broadcast_to(state_m[:, l].reshape(dbsz, units, 1), (dbsz, units, GATE_PAD)))
        lat, _ = _block(lat, pos, mod, dbsz, dseq, _pick_tile(dseq, dbsz * dseq, 1024),
                        lambda tm: (lambda i: 1 + (i * tm) // dseq), state0, False, lw, alpha)

    return (ctx.reshape(bsz, seq, D_MODEL), lat.reshape(dbsz, dseq, D_MODEL),
            jnp.stack(new_c, axis=1), jnp.stack(new_n, axis=1), jnp.stack(new_m, axis=1))
```

```python
import functools
import math

import jax
import jax.numpy as jnp
from jax import lax
from jax.experimental import pallas as pl
from jax.experimental.pallas import tpu as pltpu

F32 = jnp.float32
BF16 = jnp.bfloat16

D_MODEL = 1024
N_HEADS = 4
HEAD_DIM = 256
CHUNK = 256
CONV_WIDTH = 31
CONV_HALO = 16
N_GATES = 16
GATE_PAD = 128
GRID_W = 64
POS_BASE = 10000.0
LN_EPS = 1e-6
PEER_HEADS = 8
PEER_NKEYS = 128
PEER_TOPK = 16
V7X_VMEM_LIMIT = 56 * 1024 * 1024


def _params(n_axes):
    return pltpu.CompilerParams(dimension_semantics=("arbitrary",) * n_axes,
                                vmem_limit_bytes=V7X_VMEM_LIMIT)


def _sigmoid(x):
    return 0.5 + 0.5 * jnp.tanh(0.5 * x)


def _log_sigmoid(x):
    return jnp.minimum(x, 0.0) - jnp.log(1.0 + jnp.exp(-jnp.abs(x)))


def _dot(a, b):
    return jnp.dot(a, b, preferred_element_type=F32)


def _dot_nt(a, b):
    return lax.dot_general(a, b, (((1,), (1,)), ((), ())), preferred_element_type=F32)


def _split2(x):
    hi = x.astype(BF16)
    lo = (x - hi.astype(F32)).astype(BF16)
    return hi, lo


def _split3(x):
    a = x.astype(BF16)
    r = x - a.astype(F32)
    b = r.astype(BF16)
    c = (r - b.astype(F32)).astype(BF16)
    return a, b, c


def _layer_norm(x, w, b):
    mu = jnp.mean(x, axis=-1, keepdims=True)
    xc = x - mu
    var = jnp.mean(xc * xc, axis=-1, keepdims=True)
    return xc * lax.rsqrt(var + LN_EPS) * w + b


def _mod_kernel(c_ref, w_ref, b_ref, o_ref):
    c = c_ref[...]
    s = c * _sigmoid(c)
    sh, sl = _split2(s)
    wh, wl = _split2(w_ref[...])
    o_ref[...] = _dot(sh, wh) + _dot(sl, wh) + _dot(sh, wl) + b_ref[...]


def _modulation(cvec, w_mod, b_mod):
    rows = cvec.shape[0]
    tn = 1536
    n = w_mod.shape[1]
    return pl.pallas_call(
        _mod_kernel,
        grid=(n // tn,),
        in_specs=[pl.BlockSpec((rows, D_MODEL), lambda j: (0, 0)),
                  pl.BlockSpec((D_MODEL, tn), lambda j: (0, j)),
                  pl.BlockSpec((1, tn), lambda j: (0, j))],
        out_specs=pl.BlockSpec((rows, tn), lambda j: (0, j)),
        out_shape=jax.ShapeDtypeStruct((rows, n), F32),
        compiler_params=_params(1),
        name="modulation",
    )(cvec, w_mod, b_mod.reshape(1, n))


_STEP_K = 1
_FEATURE_MAJOR_STEPS = (0, 2, 3)
_N_TOKEN_MAJOR = 5


def _lane_tile(block, n_lanes):
    return jnp.concatenate([block] * (n_lanes // block.shape[1]), axis=1)


def _inproj_kernel(has_pos, *refs):
    if has_pos:
        (x_ref, pos_ref, mod_ref, w_ref, b_ref, bt_ref, wgh_ref, wgl_ref, bg_ref,
         proj_ref, projt_ref, gates_ref, gatest_ref, u_scr) = refs
    else:
        (x_ref, mod_ref, w_ref, b_ref, bt_ref, wgh_ref, wgl_ref, bg_ref,
         proj_ref, projt_ref, gates_ref, gatest_ref, u_scr) = refs
    j = pl.program_id(1)
    tm = u_scr.shape[0]

    @pl.when(j == 0)
    def _():
        x = x_ref[...]
        if has_pos:
            x = x + pos_ref[...]
        mod = mod_ref[...]
        u = x * (1.0 + mod[:, D_MODEL:2 * D_MODEL]) + mod[:, 0:D_MODEL]
        uh, ul = _split2(u)
        u_scr[...] = uh
        wgh = wgh_ref[...]
        gates = _dot(uh, wgh) + _dot(ul, wgh) + _dot(uh, wgl_ref[...]) + bg_ref[...]
        gates_ref[...] = gates
        gatest_ref[...] = gates.T

    is_t = jnp.logical_or(j == 0, jnp.logical_or(j == 2, j == 3))

    @pl.when(is_t)
    def _():
        acc = _dot_nt(w_ref[...], u_scr[...]) + _lane_tile(bt_ref[...], tm)

        @pl.when(j == 3)
        def _():
            projt_ref[...] = _sigmoid(acc.astype(BF16))

        @pl.when(j != 3)
        def _():
            projt_ref[...] = acc.astype(BF16)

    @pl.when(jnp.logical_not(is_t))
    def _():
        acc = _dot(u_scr[...], w_ref[...]) + b_ref[...]

        @pl.when(j == _STEP_K)
        def _():
            proj_ref[...] = (acc * (HEAD_DIM ** -0.5)).astype(BF16)

        @pl.when(j == 4)
        def _():
            proj_ref[...] = acc.astype(BF16)

        @pl.when(j >= 5)
        def _():
            proj_ref[...] = _sigmoid(acc.astype(BF16))


def _inproj(x, pos, mod, row_fn, lw, tm):
    t = x.shape[0]
    n_steps = lw["w_main"].shape[1] // D_MODEL
    has_pos = pos is not None
    in_specs = [pl.BlockSpec((tm, D_MODEL), lambda i, j: (i, 0))]
    args = [x]
    if has_pos:
        pos_blocks = pos.shape[0] // tm
        in_specs.append(pl.BlockSpec((tm, D_MODEL), lambda i, j: (i % pos_blocks, 0)))
        args.append(pos)

    def token_major_block(j):
        return jnp.maximum(j - 3, 0)

    def feature_major_block(j):
        return jnp.clip(j - 1, 0, 2)

    in_specs += [
        pl.BlockSpec((None, 1, mod.shape[2]), lambda i, j: (row_fn(i), 0, 0)),
        pl.BlockSpec((D_MODEL, D_MODEL), lambda i, j: (0, j)),
        pl.BlockSpec((1, D_MODEL), lambda i, j: (0, j)),
        pl.BlockSpec((None, D_MODEL, _LANES), lambda i, j: (feature_major_block(j), 0, 0)),
        pl.BlockSpec((D_MODEL, GATE_PAD), lambda i, j: (0, 0)),
        pl.BlockSpec((D_MODEL, GATE_PAD), lambda i, j: (0, 0)),
        pl.BlockSpec((1, GATE_PAD), lambda i, j: (0, 0)),
    ]
    args += [mod, lw["w_main"], lw["b_main"], lw["b_feat"], lw["wg_hi"], lw["wg_lo"], lw["b_gate"]]
    return pl.pallas_call(
        functools.partial(_inproj_kernel, has_pos),
        grid=(t // tm, n_steps),
        in_specs=in_specs,
        out_specs=[pl.BlockSpec((tm, D_MODEL), lambda i, j: (i, token_major_block(j))),
                   pl.BlockSpec((D_MODEL, tm), lambda i, j: (feature_major_block(j), i)),
                   pl.BlockSpec((tm, GATE_PAD), lambda i, j: (i, 0)),
                   pl.BlockSpec((GATE_PAD, tm), lambda i, j: (0, i))],
        out_shape=[jax.ShapeDtypeStruct((t, _N_TOKEN_MAJOR * D_MODEL), BF16),
                   jax.ShapeDtypeStruct((len(_FEATURE_MAJOR_STEPS) * D_MODEL, t), BF16),
                   jax.ShapeDtypeStruct((t, GATE_PAD), F32),
                   jax.ShapeDtypeStruct((GATE_PAD, t), F32)],
        scratch_shapes=[pltpu.VMEM((tm, D_MODEL), BF16)],
        compiler_params=_params(2),
        name="inproj",
    )(*args)


_SEQ_INPUTS = 10


def _mlstm_kernel(has_state, want_state, nc, n_par, *refs):
    refs = list(refs)
    seq_in = [refs[a * _SEQ_INPUTS:(a + 1) * _SEQ_INPUTS] for a in range(n_par)]
    pos = n_par * _SEQ_INPUTS
    if has_state:
        c0_ref, n0_ref, m0_ref = refs[pos:pos + 3]
        pos += 3
    hf_ref, hb_ref = refs[pos:pos + 2]
    pos += 2
    if want_state:
        co_ref, no_ref, mo_ref = refs[pos:pos + 3]
        pos += 3
    c_all, n_all, m_all = refs[pos:pos + 3]
    step = pl.program_id(1)

    @pl.when(step == 0)
    def _():
        if has_state:
            c_all[...] = c0_ref[...]
            n_all[...] = n0_ref[...]
            m_all[...] = m0_ref[...]
        else:
            c_all[...] = jnp.zeros_like(c_all)
            n_all[...] = jnp.zeros_like(n_all)
            m_all[...] = jnp.zeros_like(m_all)

    row = lax.broadcasted_iota(jnp.int32, (CHUNK, CHUNK), 0)
    col = lax.broadcasted_iota(jnp.int32, (CHUNK, CHUNK), 1)

    streams = []
    for a in range(n_par):
        qf, kf, vf, qb, kb, vb, gf, gb, gtf, gtb = seq_in[a]
        streams.append((a, 0, qf, kf, vf, gf, gtf, hf_ref))
        streams.append((a, 1, qb, kb, vb, gb, gtb, hb_ref))
    for a, d, qt_ref, k_ref, vt_ref, g_ref, gt_ref, h_ref in streams:
        c_scr, n_scr, m_scr = c_all.at[a], n_all.at[a], m_all.at[a]
        visible = (row <= col) if d == 0 else (row >= col)
        tri_t = jnp.where(visible, 1.0, 0.0).astype(BF16)
        tri = jnp.where((col <= row) if d == 0 else (col >= row), 1.0, 0.0).astype(BF16)
        g = g_ref[...]
        g_t = gt_ref[...]
        l1, l2, l3 = _split3(_log_sigmoid(g))
        b_col_all = _dot(tri, l1) + _dot(tri, l2) + _dot(tri, l3)
        t1, t2, t3 = _split3(_log_sigmoid(g_t))
        b_row_all = _dot(t1, tri_t) + _dot(t2, tri_t) + _dot(t3, tri_t)
        last = CHUNK - 1 if d == 0 else 0
        for h in range(N_HEADS):
            u = d * N_HEADS + h
            ci = d * 2 * N_HEADS + h
            cf = ci + N_HEADS
            b_row = b_row_all[cf:cf + 1, :]
            i_row = g_t[ci:ci + 1, :]
            src_col = g[:, ci:ci + 1] - b_col_all[:, cf:cf + 1]
            m = m_scr[u:u + 1, 0:1]
            hs = slice(h * HEAD_DIM, (h + 1) * HEAD_DIM)
            q_t = qt_ref[hs, :]
            k = k_ref[:, hs]
            v_t = vt_ref[hs, :]
            c_state = c_scr[u]
            n_state = n_scr[u:u + 1, :]

            dmat_t = jnp.where(visible, src_col + b_row, -jnp.inf)
            inter = b_row + m
            m_t = jnp.maximum(inter, jnp.max(dmat_t, axis=0, keepdims=True))
            w_st = jnp.exp(dmat_t - m_t)
            a_t = jnp.exp(inter - m_t)
            s_t = _dot(k, q_t) * w_st
            num_t = _dot(v_t, s_t.astype(BF16)) + a_t * _dot(c_state.astype(BF16), q_t)
            n_rows = jnp.broadcast_to(n_state, (_F32_ROWS, HEAD_DIM)).astype(BF16)
            den = jnp.sum(s_t, axis=0, keepdims=True) + a_t * _dot(n_rows, q_t)[0:1, :]
            h_ref[hs, a * CHUNK:(a + 1) * CHUNK] = num_t / jnp.maximum(jnp.abs(den), jnp.exp(-m_t))

            b_last = b_row[:, last:last + 1]
            dec_row = b_last - b_row + i_row
            m_new = jnp.maximum(b_last + m, jnp.max(dec_row, axis=-1, keepdims=True))
            w_row = jnp.exp(dec_row - m_new)
            a_c = jnp.exp(b_last + m - m_new)
            vw_t = (v_t.astype(F32) * w_row).astype(BF16)
            c_scr[u] = a_c * c_state + _dot(vw_t, k)
            w_rows = jnp.broadcast_to(w_row, (_F32_ROWS, CHUNK)).astype(BF16)
            n_scr[u:u + 1, :] = a_c * n_state + _dot(w_rows, k)[0:1, :]
            m_scr[u:u + 1, :] = jnp.broadcast_to(m_new, (1, GATE_PAD))

    if want_state:
        @pl.when(step == nc - 1)
        def _():
            co_ref[...] = c_all[...]
            no_ref[...] = n_all[...]
            mo_ref[...] = m_all[...]


def _mlstm(proj, projt, gates, gatest, n_seq, seq_len, state0, want_state):
    t = proj.shape[0]
    nc = seq_len // CHUNK
    units = 2 * N_HEADS
    has_state = state0 is not None

    n_par = _scan_group(n_seq)
    tile = (CHUNK, D_MODEL)
    tile_t = (D_MODEL, CHUNK)
    in_specs, args = [], []
    for a in range(n_par):
        def fwd(c, a=a):
            return lambda p, k: ((p * n_par + a) * nc + k, c)

        def bwd(c, a=a):
            return lambda p, k: ((p * n_par + a) * nc + nc - 1 - k, c)

        def fwd_t(r, a=a):
            return lambda p, k: (r, (p * n_par + a) * nc + k)

        def bwd_t(r, a=a):
            return lambda p, k: (r, (p * n_par + a) * nc + nc - 1 - k)

        in_specs += [pl.BlockSpec(tile_t, fwd_t(0)), pl.BlockSpec(tile, fwd(0)), pl.BlockSpec(tile_t, fwd_t(1)),
                     pl.BlockSpec(tile_t, bwd_t(0)), pl.BlockSpec(tile, bwd(0)), pl.BlockSpec(tile_t, bwd_t(1)),
                     pl.BlockSpec((CHUNK, GATE_PAD), fwd(0)), pl.BlockSpec((CHUNK, GATE_PAD), bwd(0)),
                     pl.BlockSpec((GATE_PAD, CHUNK), fwd_t(0)), pl.BlockSpec((GATE_PAD, CHUNK), bwd_t(0))]
        args += [projt, proj, projt, projt, proj, projt, gates, gates, gatest, gatest]
    state_specs = [pl.BlockSpec((n_par, units, HEAD_DIM, HEAD_DIM), lambda p, k: (p, 0, 0, 0)),
                   pl.BlockSpec((n_par, units, HEAD_DIM), lambda p, k: (p, 0, 0)),
                   pl.BlockSpec((n_par, units, GATE_PAD), lambda p, k: (p, 0, 0))]
    state_shapes = [jax.ShapeDtypeStruct((n_seq, units, HEAD_DIM, HEAD_DIM), F32),
                    jax.ShapeDtypeStruct((n_seq, units, HEAD_DIM), F32),
                    jax.ShapeDtypeStruct((n_seq, units, GATE_PAD), F32)]
    if has_state:
        in_specs += state_specs
        args += list(state0)
    group_tile = (D_MODEL, n_par * CHUNK)
    out_specs = [pl.BlockSpec(group_tile, lambda p, k: (0, p * nc + k)),
                 pl.BlockSpec(group_tile, lambda p, k: (0, p * nc + nc - 1 - k))]
    out_shape = [jax.ShapeDtypeStruct((D_MODEL, t), F32), jax.ShapeDtypeStruct((D_MODEL, t), F32)]
    if want_state:
        out_specs += state_specs
        out_shape += state_shapes
    return pl.pallas_call(
        functools.partial(_mlstm_kernel, has_state, want_state, nc, n_par),
        grid=(n_seq // n_par, nc),
        in_specs=in_specs,
        out_specs=out_specs,
        out_shape=out_shape,
        scratch_shapes=[pltpu.VMEM((n_par, units, HEAD_DIM, HEAD_DIM), F32),
                        pltpu.VMEM((n_par, units, HEAD_DIM), F32),
                        pltpu.VMEM((n_par, units, GATE_PAD), F32)],
        compiler_params=_params(2),
        name="mlstm",
    )(*args)


def _scan_group(n_seq):
    return 2 if n_seq % 2 == 0 else 1


def _scan_column(seq, chunk, nc, n_par):
    return ((seq // n_par) * nc + chunk) * n_par + seq % n_par


_CONV_ROWS = 64
_LANES = 128


def _tail_kernel(has_pos, tm, tiles_per_seq, alpha, *refs):
    refs = list(refs)
    n_ct = tm // CHUNK
    h_refs = refs[:2 * n_ct]
    so_ref, val_ref, sg_ref, sga_ref, sgb_ref, vp_ref, gp_ref, vn_ref, gn_ref, x_ref = refs[2 * n_ct:2 * n_ct + 10]
    pos = 2 * n_ct + 10
    if has_pos:
        pos_ref = refs[pos]
        pos += 1
    (mod_ref, normw_ref, wa_ref, wc_ref, wo_ref, dww_ref, dwb_ref, clnw_ref, clnb_ref, ln1w_ref, ln1b_ref, wq_ref,
     x1_ref, u2_ref, q_ref, xpad, conv_scr, shifted) = refs[pos:]
    i = pl.program_id(0)

    hsum = jnp.concatenate([h_refs[j][...] + h_refs[n_ct + j][...] for j in range(n_ct)], axis=1)
    parts = []
    for h in range(N_HEADS):
        hh = hsum[h * HEAD_DIM:(h + 1) * HEAD_DIM, :]
        mu = jnp.mean(hh, axis=0, keepdims=True)
        hc = hh - mu
        var = jnp.mean(hc * hc, axis=0, keepdims=True)
        parts.append(hc * lax.rsqrt(var + LN_EPS))
    hn = jnp.concatenate(parts, axis=0) * _lane_tile(normw_ref[...], tm)
    hg = (so_ref[...].astype(F32) * hn).T.astype(BF16)
    branch_a = _dot(hg, wa_ref[...])

    first = (i % tiles_per_seq) == 0
    last = (i % tiles_per_seq) == tiles_per_seq - 1
    keep_prev = jnp.where(first, 0.0, 1.0)
    keep_next = jnp.where(last, 0.0, 1.0)
    xpad[0:CONV_HALO, :] = vp_ref[...].astype(F32) * gp_ref[...].astype(F32) * keep_prev
    xpad[CONV_HALO:CONV_HALO + tm, :] = val_ref[...].astype(F32) * sg_ref[...].astype(F32)
    xpad[CONV_HALO + tm:2 * CONV_HALO + tm, :] = vn_ref[...].astype(F32) * gn_ref[...].astype(F32) * keep_next
    tap0 = CONV_HALO - CONV_WIDTH // 2

    n_shift_rows = tm + 2 * CONV_HALO - _F32_ROWS

    def col_body(c, carry):
        cs = pl.ds(pl.multiple_of(c * _LANES, _LANES), _LANES)
        for r in range(1, _F32_ROWS):
            shifted[r - 1, 0:n_shift_rows, :] = xpad[pl.ds(r, n_shift_rows), cs]
        for rb in range(tm // _CONV_ROWS):
            acc = jnp.broadcast_to(dwb_ref[:, cs], (_CONV_ROWS, _LANES))
            for k in range(CONV_WIDTH):
                tiles, r = divmod(tap0 + k, _F32_ROWS)
                start = tiles * _F32_ROWS + rb * _CONV_ROWS
                if r == 0:
                    src = xpad[pl.ds(start, _CONV_ROWS), cs]
                else:
                    src = shifted[r - 1, pl.ds(start, _CONV_ROWS), :]
                acc = acc + src * dww_ref[k:k + 1, cs]
            conv_scr[pl.ds(rb * _CONV_ROWS, _CONV_ROWS), cs] = acc
        return carry

    lax.fori_loop(0, D_MODEL // _LANES, col_body, 0)
    xc = _layer_norm(conv_scr[...], clnw_ref[...], clnb_ref[...])
    xc = (xc * _sigmoid(xc)).astype(BF16)
    branch_b = _dot(xc, wc_ref[...])

    merged = sga_ref[...].astype(F32) * branch_a + sgb_ref[...].astype(F32) * branch_b
    mix = _dot(merged.astype(BF16), wo_ref[...])

    mod = mod_ref[...]
    gate1 = mod[:, 2 * D_MODEL:3 * D_MODEL]
    shift2 = mod[:, 3 * D_MODEL:4 * D_MODEL]
    scale2 = mod[:, 4 * D_MODEL:5 * D_MODEL]
    x = x_ref[...]
    if has_pos:
        x = x + pos_ref[...]
    x1 = _layer_norm(alpha * x + gate1 * mix, ln1w_ref[...], ln1b_ref[...])
    x1_ref[...] = x1
    u2 = (x1 * (1.0 + scale2) + shift2).astype(BF16)
    u2_ref[...] = u2
    q_ref[...] = _dot(u2, wq_ref[...]).astype(BF16)


def _tail(hf, hb, proj, projt, x, pos, mod, row_fn, seq_len, tm, alpha, lw):
    t = x.shape[0]
    has_pos = pos is not None
    tiles_per_seq = seq_len // tm
    hb_per_tile = tm // CONV_HALO
    n_halo = t // CONV_HALO
    big = (tm, D_MODEL)
    halo = (CONV_HALO, D_MODEL)

    def colspec(c):
        return pl.BlockSpec(big, lambda i: (i, c))

    def prev(c):
        return pl.BlockSpec(halo, lambda i: (jnp.maximum(i * hb_per_tile - 1, 0), c))

    def nxt(c):
        return pl.BlockSpec(halo, lambda i: (jnp.minimum((i + 1) * hb_per_tile, n_halo - 1), c))

    def const(shape):
        return pl.BlockSpec(shape, lambda i: (0,) * len(shape), pipeline_mode=pl.Buffered(1))

    def rowspec(r):
        return pl.BlockSpec((D_MODEL, tm), lambda i: (r, i))

    nc = seq_len // CHUNK
    n_par = _scan_group(t // seq_len)
    n_ct = tm // CHUNK

    def scan_chunk(j):
        return pl.BlockSpec((D_MODEL, CHUNK), lambda i: (
            0, _scan_column(i // tiles_per_seq, (i % tiles_per_seq) * n_ct + j, nc, n_par)))

    in_specs = [scan_chunk(j) for j in range(n_ct)] * 2
    in_specs += [rowspec(2), colspec(1), colspec(2), colspec(3), colspec(4),
                 prev(1), prev(2), nxt(1), nxt(2), colspec(0)]
    args = [hf] * n_ct + [hb] * n_ct + [projt, proj, proj, proj, proj, proj, proj, proj, proj, x]
    if has_pos:
        in_specs.append(pl.BlockSpec(big, lambda i: (i % tiles_per_seq, 0)))
        args.append(pos)
    in_specs += [pl.BlockSpec((None, 1, mod.shape[2]), lambda i: (row_fn(i), 0, 0)),
                 const((D_MODEL, _LANES)), const((D_MODEL, D_MODEL)), const((D_MODEL, D_MODEL)),
                 const((D_MODEL, D_MODEL)),
                 const((32, D_MODEL)), const((1, D_MODEL)), const((1, D_MODEL)), const((1, D_MODEL)),
                 const((1, D_MODEL)), const((1, D_MODEL)), const(lw["wq"].shape)]
    args += [mod, lw["norm_w"], lw["w_a"], lw["w_cout"], lw["w_out"], lw["dw_w"], lw["dw_b"], lw["cln_w"],
             lw["cln_b"], lw["ln1_w"], lw["ln1_b"], lw["wq"]]
    n_q = lw["wq"].shape[1]
    return pl.pallas_call(
        functools.partial(_tail_kernel, has_pos, tm, tiles_per_seq, alpha),
        grid=(t // tm,),
        in_specs=in_specs,
        out_specs=[pl.BlockSpec(big, lambda i: (i, 0)), pl.BlockSpec(big, lambda i: (i, 0)),
                   pl.BlockSpec((tm, n_q), lambda i: (i, 0))],
        out_shape=[jax.ShapeDtypeStruct((t, D_MODEL), F32), jax.ShapeDtypeStruct((t, D_MODEL), BF16),
                   jax.ShapeDtypeStruct((t, n_q), BF16)],
        scratch_shapes=[pltpu.VMEM((tm + 2 * CONV_HALO, D_MODEL), F32), pltpu.VMEM((tm, D_MODEL), F32),
                        pltpu.VMEM((_F32_ROWS - 1, tm + 2 * CONV_HALO, _LANES), F32)],
        compiler_params=_params(1),
        name="mixer_tail",
    )(*args)


_TOK_BLOCKS = 8
_LANE = 128
_ROUTE_TM = _TOK_BLOCKS * _LANE


def _merge_exchange_pairs(n):
    pairs = []
    t = (n - 1).bit_length()
    p = 1 << (t - 1)
    while p > 0:
        q, r, d = 1 << (t - 1), 0, p
        while d > 0:
            pairs.extend((i, i + d) for i in range(n - d) if (i & p) == r)
            d, q, r = q - p, q >> 1, p
        p >>= 1
    return tuple(pairs)


_SORT16 = _merge_exchange_pairs(PEER_TOPK)


def _sort_desc(vals):
    vals = list(vals)
    for i, j in _SORT16:
        vals[i], vals[j] = jnp.maximum(vals[i], vals[j]), jnp.minimum(vals[i], vals[j])
    return vals


def _bitonic_desc(vals):
    vals = list(vals)
    d = len(vals) // 2
    while d > 0:
        for i in range(len(vals)):
            if (i & d) == 0:
                vals[i], vals[i + d] = jnp.maximum(vals[i], vals[i + d]), jnp.minimum(vals[i], vals[i + d])
        d //= 2
    return vals


def _top_merge(a, b):
    n = len(a)
    return _bitonic_desc([jnp.maximum(a[i], b[n - 1 - i]) for i in range(n)])


def _top16(keys):
    if len(keys) == PEER_TOPK:
        return _sort_desc(keys)
    half = len(keys) // 2
    return _top_merge(_top16(keys[:half]), _top16(keys[half:]))


def _prefix_count(test, v):
    t8 = test(v[7])
    t4 = test(jnp.where(t8, v[11], v[3]))
    t2 = test(jnp.where(t8, jnp.where(t4, v[13], v[9]), jnp.where(t4, v[5], v[1])))
    lo = jnp.where(t4, jnp.where(t2, v[6], v[4]), jnp.where(t2, v[2], v[0]))
    hi = jnp.where(t4, jnp.where(t2, v[14], v[12]), jnp.where(t2, v[10], v[8]))
    t1 = test(jnp.where(t8, hi, lo))
    cnt = (jnp.where(t8, 8.0, 0.0) + jnp.where(t4, 4.0, 0.0)) + (jnp.where(t2, 2.0, 0.0) + jnp.where(t1, 1.0, 0.0))
    return jnp.where(test(v[15]), 16.0, cnt)


def _joint_top16(a, b):
    q0 = [a[p] + b[0] for p in range(16)]
    q1 = [a[p] + b[1] for p in range(8)]
    p0 = [a[0] + b[q] for q in range(8, 16)]
    m1 = _bitonic_desc(q1 + p0[::-1])
    m2 = _sort_desc([a[p] + b[q] for q, n in ((2, 5), (3, 4), (4, 3), (5, 2), (6, 2)) for p in range(n)])
    q7 = [a[0] + b[7], a[1] + b[7]]
    t2 = _bitonic_desc(m2[:14] + [jnp.maximum(m2[14], q7[1]), jnp.maximum(m2[15], q7[0])])
    return _top_merge(_top_merge(q0, m1), t2)


def _route_kernel(q_ref, skx_ref, rank2_ref, e2_ref, r_ref, a1_ref, km_rank2, km_e2, km_r, km_a1):
    nb = _TOK_BLOCKS
    keys = []
    for p in range(2):
        qp = jnp.concatenate([q_ref[a * _LANE:(a + 1) * _LANE, p * PEER_NKEYS:(p + 1) * PEER_NKEYS]
                              for a in range(nb)], axis=1)
        sp = _dot_nt(skx_ref[p], qp)
        keys.append([sp[k * nb:(k + 1) * nb, :] for k in range(PEER_NKEYS)])
    s1, s2 = keys
    a = _top16(s1)
    b = _top16(s2)
    top = _joint_top16(a, b)
    tau = top[PEER_TOPK - 1]
    zsum = jnp.ones_like(tau)
    for c in top[1:]:
        zsum = zsum + jnp.exp(c - top[0])
    inv_z = 1.0 / zsum
    b_asc = b[::-1]
    for k in range(PEER_NKEYS):
        rows = slice(k * nb, (k + 1) * nb)
        km_r[rows, :] = _prefix_count(lambda bq, x=s1[k]: x + bq >= tau, b)
        km_a1[rows, :] = jnp.exp(s1[k] - a[0]) * inv_z
        km_rank2[rows, :] = float(PEER_TOPK) - _prefix_count(lambda bq, x=s2[k]: x >= bq, b_asc)
        km_e2[rows, :] = jnp.exp(s2[k] - b[0])
    for blk in range(nb):
        cols = slice(blk * _LANE, (blk + 1) * _LANE)
        rows = pl.ds(blk, PEER_NKEYS, stride=nb)
        rank2_ref[:, cols] = km_rank2[rows, :].astype(BF16)
        e2_ref[:, cols] = km_e2[rows, :].astype(BF16)
        r_ref[:, cols] = km_r[rows, :]
        a1_ref[:, cols] = km_a1[rows, :]


def _route(q, skx):
    t = q.shape[0]
    tm = _ROUTE_TM
    assert t % tm == 0, (t, tm)
    out_blk = pl.BlockSpec((None, PEER_NKEYS, tm), lambda h, i: (h, 0, i))
    shape = (PEER_HEADS, PEER_NKEYS, t)
    km = pltpu.VMEM((PEER_NKEYS * _TOK_BLOCKS, _LANE), F32)
    return pl.pallas_call(
        _route_kernel,
        grid=(PEER_HEADS, t // tm),
        in_specs=[pl.BlockSpec((tm, 2 * PEER_NKEYS), lambda h, i: (i, h)),
                  pl.BlockSpec((None, 2, PEER_NKEYS * _TOK_BLOCKS, _TOK_BLOCKS * PEER_NKEYS),
                               lambda h, i: (h, 0, 0, 0))],
        out_specs=[out_blk, out_blk, out_blk, out_blk],
        out_shape=[jax.ShapeDtypeStruct(shape, BF16), jax.ShapeDtypeStruct(shape, BF16),
                   jax.ShapeDtypeStruct(shape, F32), jax.ShapeDtypeStruct(shape, F32)],
        scratch_shapes=[km, km, km, km],
        compiler_params=_params(2),
        name="peer_route",
    )(q, skx)


_GELU_C = math.sqrt(2.0 / math.pi)
_BF16_ROWS = 16
_F32_ROWS = 8
_MXU_N = 256


def _gelu_tanh(x):
    neg2z = x * (x * x * (-2.0 * _GELU_C * 0.044715) + (-2.0 * _GELU_C))
    return x / (1.0 + jnp.exp(neg2z))


def _experts_kernel(ib, n_eblk, alpha, u_ref, ut_ref, vt_ref, rank2_ref, e2_ref, r_ref, a1_ref, x1_ref, mod_ref,
                    lnw_ref, lnb_ref, y_ref, acc_scr, at_scr, p_scr, uT_scr):
    s = pl.program_id(1)

    @pl.when(s == 0)
    def _():
        acc_scr[...] = jnp.zeros_like(acc_scr)
        uT_scr[...] = u_ref[...].astype(F32).T.astype(BF16)

    tm = u_ref.shape[0]
    key_tiles = PEER_NKEYS // _BF16_ROWS
    rows_per_chunk = _MXU_N // PEER_NKEYS
    n_chunks = ib // rows_per_chunk

    def tile_rows(ref, h, ii, cols):
        group = ref[h, pl.ds(pl.multiple_of(s * ib + (ii // _F32_ROWS) * _F32_ROWS, _F32_ROWS), _F32_ROWS), cols]
        row = group[ii % _F32_ROWS:ii % _F32_ROWS + 1, :]
        tile = jnp.broadcast_to(row, (_BF16_ROWS, _MXU_N)).astype(BF16)
        return jnp.concatenate([tile] * key_tiles, axis=0)

    def weights(ii):
        rows = slice(ii * PEER_NKEYS, (ii + 1) * PEER_NKEYS)
        for c in range(tm // _MXU_N):
            cols = slice(c * _MXU_N, (c + 1) * _MXU_N)
            act = _gelu_tanh(at_scr[rows, cols].astype(BF16))
            w = None
            for h in range(PEER_HEADS):
                r_b = tile_rows(r_ref, h, ii, cols)
                a_b = tile_rows(a1_ref, h, ii, cols)
                term = jnp.where(rank2_ref[h, :, cols] < r_b, e2_ref[h, :, cols] * a_b, jnp.zeros((), BF16))
                w = term if w is None else w + term
            p_scr[rows, cols] = w * act

    for ch in range(n_chunks):
        erows = slice(ch * _MXU_N, (ch + 1) * _MXU_N)
        at_scr[erows, :] = _dot(ut_ref[erows, :], uT_scr[...])
    for ch in range(n_chunks):
        erows = slice(ch * _MXU_N, (ch + 1) * _MXU_N)
        for ii in range(ch * rows_per_chunk, (ch + 1) * rows_per_chunk):
            weights(ii)
        acc_scr[...] += _dot(vt_ref[:, erows], p_scr[erows, :])

    @pl.when(s == n_eblk - 1)
    def _():
        mod = mod_ref[...]
        gate2 = mod[:, 5 * D_MODEL:6 * D_MODEL]
        y = alpha * x1_ref[...] + gate2 * acc_scr[...].T
        y_ref[...] = _layer_norm(y, lnw_ref[...], lnb_ref[...])


def _experts(u2, u_tab, vt_tab, rank2, e2, r, a1, x1, mod, row_fn, lnw, lnb, tm, eb, alpha):
    t = u2.shape[0]
    n_exp = u_tab.shape[0]
    ib = eb // PEER_NKEYS
    n_eblk = n_exp // eb
    full = pl.BlockSpec((PEER_HEADS, PEER_NKEYS, tm), lambda i, s: (0, 0, i))
    tok = pl.BlockSpec((tm, D_MODEL), lambda i, s: (i, 0))
    vec = pl.BlockSpec((1, D_MODEL), lambda i, s: (0, 0))
    return pl.pallas_call(
        functools.partial(_experts_kernel, ib, n_eblk, alpha),
        grid=(t // tm, n_eblk),
        in_specs=[tok,
                  pl.BlockSpec((eb, D_MODEL), lambda i, s: (s, 0)),
                  pl.BlockSpec((D_MODEL, eb), lambda i, s: (0, s)),
                  full, full, full, full, tok,
                  pl.BlockSpec((None, 1, mod.shape[2]), lambda i, s: (row_fn(i), 0, 0)),
                  vec, vec],
        out_specs=tok,
        out_shape=jax.ShapeDtypeStruct((t, D_MODEL), F32),
        scratch_shapes=[pltpu.VMEM((D_MODEL, tm), F32), pltpu.VMEM((eb, tm), F32),
                        pltpu.VMEM((eb, tm), BF16), pltpu.VMEM((D_MODEL, tm), BF16)],
        compiler_params=_params(2),
        name="peer_experts",
    )(u2, u_tab, vt_tab, rank2, e2, r, a1, x1, mod, lnw, lnb)


def _grid_pos_embed(n_tokens):
    rows = n_tokens // GRID_W
    quarter = D_MODEL // 4
    freqs = jnp.exp(-math.log(POS_BASE) * jnp.arange(quarter, dtype=F32) / quarter)
    r = jnp.arange(rows, dtype=F32)[:, None] * freqs
    cl = jnp.arange(GRID_W, dtype=F32)[:, None] * freqs
    er = jnp.concatenate([jnp.sin(r), jnp.cos(r)], -1)
    ec = jnp.concatenate([jnp.sin(cl), jnp.cos(cl)], -1)
    emb = jnp.concatenate([jnp.broadcast_to(er[:, None, :], (rows, GRID_W, D_MODEL // 2)),
                           jnp.broadcast_to(ec[None, :, :], (rows, GRID_W, D_MODEL // 2))], -1)
    return emb.reshape(rows * GRID_W, D_MODEL)


def _pick_tile(seq_len, n_tokens, target):
    tm = min(target, seq_len)
    while seq_len % tm or n_tokens % tm:
        tm //= 2
    return tm


def _block(x, pos, mod, n_seq, seq_len, tm_in, row_of_token_tile, state0, want_state, lw, alpha):
    t = x.shape[0]
    proj, projt, gates, gatest = _inproj(x, pos, mod, row_of_token_tile(tm_in), lw, tm_in)
    res = _mlstm(proj, projt, gates, gatest, n_seq, seq_len, state0, want_state)
    hf, hb = res[0], res[1]
    tm_tail = _pick_tile(seq_len, t, 256)
    x1, u2, q = _tail(hf, hb, proj, projt, x, pos, mod, row_of_token_tile(tm_tail), seq_len, tm_tail, alpha, lw)
    tm_p = _pick_tile(seq_len, t, 512)
    rank2, e2, r, a1 = _route(q, lw["skx"])
    y = _experts(u2, lw["peer_u"], lw["peer_vt"], rank2, e2, r, a1, x1, mod, row_of_token_tile(tm_p),
                 lw["ln2_w"], lw["ln2_b"], tm_p, 2048, alpha)
    return y, res[2:]


def kernel(x_prompt, x_sample, state_C, state_n, state_m, c, c_ctx, w_in, b_in, mlstm_norm_w, w_a, conv_dw_w,
           conv_dw_b, conv_ln_w, conv_ln_b, w_conv_out, w_out, w_mod, b_mod, ln1_w, ln1_b, ln2_w, ln2_b,
           peer_w_query, peer_subkeys, peer_u, peer_v):
    depth = w_in.shape[0]
    alpha = (2.0 * depth) ** 0.25
    bsz, seq, _ = x_prompt.shape
    dbsz, dseq, _ = x_sample.shape
    units = 2 * N_HEADS
    gate_off = 4 * D_MODEL

    ctx = x_prompt.reshape(bsz * seq, D_MODEL)
    lat = x_sample.reshape(dbsz * dseq, D_MODEL)
    pos = _grid_pos_embed(dseq)
    n_rows = 1 + dbsz
    pad_rows = (-n_rows) % 8
    cvec = jnp.concatenate([c_ctx[None, :], c, jnp.zeros((pad_rows, D_MODEL), F32)], axis=0)

    new_c, new_n, new_m = [], [], []
    for l in range(depth):
        vec = lambda a: a[l].reshape(1, -1)
        w_l = w_in[l]
        b_l = b_in[l]
        wg = jnp.pad(w_l[:, gate_off:gate_off + N_GATES], ((0, 0), (0, GATE_PAD - N_GATES)))
        wg_hi = wg.astype(BF16)
        wg_lo = (wg - wg_hi.astype(F32)).astype(BF16)
        b_gate = jnp.pad(b_l[gate_off:gate_off + N_GATES], (0, GATE_PAD - N_GATES))
        w_cols = jnp.concatenate([w_l[:, :gate_off], w_l[:, gate_off + N_GATES:]], axis=1).astype(BF16)
        b_cols = jnp.concatenate([b_l[:gate_off], b_l[gate_off + N_GATES:]])
        n_steps = w_cols.shape[1] // D_MODEL
        blocks = [w_cols[:, s * D_MODEL:(s + 1) * D_MODEL] for s in range(n_steps)]
        lanes = lambda col: jnp.broadcast_to(col[:, None], (col.shape[0], _LANES))
        lw = {
            "w_main": jnp.concatenate([blk.T if s in _FEATURE_MAJOR_STEPS else blk for s, blk in enumerate(blocks)],
                                      axis=1),
            "b_main": b_cols.reshape(1, -1),
            "b_feat": jnp.stack([lanes(b_cols[s * D_MODEL:(s + 1) * D_MODEL]) for s in _FEATURE_MAJOR_STEPS]),
            "wg_hi": wg_hi, "wg_lo": wg_lo, "b_gate": b_gate.reshape(1, -1),
            "norm_w": lanes(mlstm_norm_w[l]), "w_a": w_a[l].astype(BF16), "w_cout": w_conv_out[l].astype(BF16),
            "w_out": w_out[l].astype(BF16),
            "dw_w": jnp.pad(conv_dw_w[l], ((0, 32 - CONV_WIDTH), (0, 0))), "dw_b": vec(conv_dw_b),
            "cln_w": vec(conv_ln_w), "cln_b": vec(conv_ln_b), "ln1_w": vec(ln1_w), "ln1_b": vec(ln1_b),
            "ln2_w": vec(ln2_w), "ln2_b": vec(ln2_b),
            "wq": peer_w_query[l].astype(BF16),
            "skx": jnp.einsum("hpkd,ab->hpkabd", peer_subkeys[l].astype(BF16), jnp.eye(_TOK_BLOCKS, dtype=BF16))
            .reshape(PEER_HEADS, 2, PEER_NKEYS * _TOK_BLOCKS, _TOK_BLOCKS * PEER_NKEYS),
            "peer_u": peer_u[l].astype(BF16),
            "peer_vt": peer_v[l].astype(BF16).T,
        }
        mod = _modulation(cvec, w_mod[l], b_mod[l]).reshape(n_rows + pad_rows, 1, 6 * D_MODEL)

        ctx, (c_fin, n_fin, m_fin) = _block(ctx, None, mod, bsz, seq, _pick_tile(bsz * seq, bsz * seq, 1024),
                                             lambda tm: (lambda i: 0), None, True, lw, alpha)
        new_c.append(c_fin.reshape(bsz, 2, N_HEADS, HEAD_DIM, HEAD_DIM))
        new_n.append(n_fin.reshape(bsz, 2, N_HEADS, HEAD_DIM))
        new_m.append(m_fin[:, :, 0].reshape(bsz, 2, N_HEADS))

        state0 = (state_C[:, l].reshape(dbsz, units, HEAD_DIM, HEAD_DIM),
                  state_n[:, l].reshape(dbsz, units, HEAD_DIM),
                  jnp.broadcast_to(state_m[:, l].reshape(dbsz, units, 1), (dbsz, units, GATE_PAD)))
        lat, _ = _block(lat, pos, mod, dbsz, dseq, _pick_tile(dseq, dbsz * dseq, 1024),
                        lambda tm: (lambda i: 1 + (i * tm) // dseq), state0, False, lw, alpha)

    return (ctx.reshape(bsz, seq, D_MODEL), lat.reshape(dbsz, dseq, D_MODEL),
            jnp.stack(new_c, axis=1), jnp.stack(new_n, axis=1), jnp.stack(new_m, axis=1))
```

```python
import functools
import math

import jax
import jax.numpy as jnp
from jax import lax
from jax.experimental import pallas as pl
from jax.experimental.pallas import tpu as pltpu

F32 = jnp.float32
BF16 = jnp.bfloat16

D_MODEL = 1024
N_HEADS = 4
HEAD_DIM = 256
CHUNK = 256
CONV_WIDTH = 31
CONV_HALO = 16
N_GATES = 16
GATE_PAD = 128
GRID_W = 64
POS_BASE = 10000.0
LN_EPS = 1e-6
PEER_HEADS = 8
PEER_NKEYS = 128
PEER_TOPK = 16
V7X_VMEM_LIMIT = 56 * 1024 * 1024


def _params(n_axes):
    return pltpu.CompilerParams(dimension_semantics=("arbitrary",) * n_axes,
                                vmem_limit_bytes=V7X_VMEM_LIMIT)


def _sigmoid(x):
    return 0.5 + 0.5 * jnp.tanh(0.5 * x)


def _log_sigmoid(x):
    return jnp.minimum(x, 0.0) - jnp.log(1.0 + jnp.exp(-jnp.abs(x)))


def _dot(a, b):
    return jnp.dot(a, b, preferred_element_type=F32)


def _dot_nt(a, b):
    return lax.dot_general(a, b, (((1,), (1,)), ((), ())), preferred_element_type=F32)


def _split2(x):
    hi = x.astype(BF16)
    lo = (x - hi.astype(F32)).astype(BF16)
    return hi, lo


def _split3(x):
    a = x.astype(BF16)
    r = x - a.astype(F32)
    b = r.astype(BF16)
    c = (r - b.astype(F32)).astype(BF16)
    return a, b, c


def _layer_norm(x, w, b):
    mu = jnp.mean(x, axis=-1, keepdims=True)
    xc = x - mu
    var = jnp.mean(xc * xc, axis=-1, keepdims=True)
    return xc * lax.rsqrt(var + LN_EPS) * w + b


def _mod_kernel(c_ref, w_ref, b_ref, o_ref):
    c = c_ref[...]
    s = c * _sigmoid(c)
    sh, sl = _split2(s)
    wh, wl = _split2(w_ref[...])
    o_ref[...] = _dot(sh, wh) + _dot(sl, wh) + _dot(sh, wl) + b_ref[...]


def _modulation(cvec, w_mod, b_mod):
    rows = cvec.shape[0]
    tn = 1536
    n = w_mod.shape[1]
    return pl.pallas_call(
        _mod_kernel,
        grid=(n // tn,),
        in_specs=[pl.BlockSpec((rows, D_MODEL), lambda j: (0, 0)),
                  pl.BlockSpec((D_MODEL, tn), lambda j: (0, j)),
                  pl.BlockSpec((1, tn), lambda j: (0, j))],
        out_specs=pl.BlockSpec((rows, tn), lambda j: (0, j)),
        out_shape=jax.ShapeDtypeStruct((rows, n), F32),
        compiler_params=_params(1),
        name="modulation",
    )(cvec, w_mod, b_mod.reshape(1, n))


_STEP_K = 1
_FEATURE_MAJOR_STEPS = (0, 2, 3)
_N_TOKEN_MAJOR = 5


def _lane_tile(block, n_lanes):
    return jnp.concatenate([block] * (n_lanes // block.shape[1]), axis=1)


def _inproj_kernel(has_pos, *refs):
    if has_pos:
        (x_ref, pos_ref, mod_ref, w_ref, b_ref, bt_ref, wgh_ref, wgl_ref, bg_ref,
         proj_ref, projt_ref, gates_ref, gatest_ref, u_scr) = refs
    else:
        (x_ref, mod_ref, w_ref, b_ref, bt_ref, wgh_ref, wgl_ref, bg_ref,
         proj_ref, projt_ref, gates_ref, gatest_ref, u_scr) = refs
    j = pl.program_id(1)
    tm = u_scr.shape[0]

    @pl.when(j == 0)
    def _():
        x = x_ref[...]
        if has_pos:
            x = x + pos_ref[...]
        mod = mod_ref[...]
        u = x * (1.0 + mod[:, D_MODEL:2 * D_MODEL]) + mod[:, 0:D_MODEL]
        uh, ul = _split2(u)
        u_scr[...] = uh
        wgh = wgh_ref[...]
        gates = _dot(uh, wgh) + _dot(ul, wgh) + _dot(uh, wgl_ref[...]) + bg_ref[...]
        gates_ref[...] = gates
        gatest_ref[...] = gates.T

    is_t = jnp.logical_or(j == 0, jnp.logical_or(j == 2, j == 3))

    @pl.when(is_t)
    def _():
        acc = _dot_nt(w_ref[...], u_scr[...]) + _lane_tile(bt_ref[...], tm)

        @pl.when(j == 3)
        def _():
            projt_ref[...] = _sigmoid(acc.astype(BF16))

        @pl.when(j != 3)
        def _():
            projt_ref[...] = acc.astype(BF16)

    @pl.when(jnp.logical_not(is_t))
    def _():
        acc = _dot(u_scr[...], w_ref[...]) + b_ref[...]

        @pl.when(j == _STEP_K)
        def _():
            proj_ref[...] = (acc * (HEAD_DIM ** -0.5)).astype(BF16)

        @pl.when(j == 4)
        def _():
            proj_ref[...] = acc.astype(BF16)

        @pl.when(j >= 5)
        def _():
            proj_ref[...] = _sigmoid(acc.astype(BF16))


def _inproj(x, pos, mod, row_fn, lw, tm):
    t = x.shape[0]
    n_steps = lw["w_main"].shape[1] // D_MODEL
    has_pos = pos is not None
    in_specs = [pl.BlockSpec((tm, D_MODEL), lambda i, j: (i, 0))]
    args = [x]
    if has_pos:
        pos_blocks = pos.shape[0] // tm
        in_specs.append(pl.BlockSpec((tm, D_MODEL), lambda i, j: (i % pos_blocks, 0)))
        args.append(pos)

    def token_major_block(j):
        return jnp.maximum(j - 3, 0)

    def feature_major_block(j):
        return jnp.clip(j - 1, 0, 2)

    in_specs += [
        pl.BlockSpec((None, 1, mod.shape[2]), lambda i, j: (row_fn(i), 0, 0)),
        pl.BlockSpec((D_MODEL, D_MODEL), lambda i, j: (0, j)),
        pl.BlockSpec((1, D_MODEL), lambda i, j: (0, j)),
        pl.BlockSpec((None, D_MODEL, _LANES), lambda i, j: (feature_major_block(j), 0, 0)),
        pl.BlockSpec((D_MODEL, GATE_PAD), lambda i, j: (0, 0)),
        pl.BlockSpec((D_MODEL, GATE_PAD), lambda i, j: (0, 0)),
        pl.BlockSpec((1, GATE_PAD), lambda i, j: (0, 0)),
    ]
    args += [mod, lw["w_main"], lw["b_main"], lw["b_feat"], lw["wg_hi"], lw["wg_lo"], lw["b_gate"]]
    return pl.pallas_call(
        functools.partial(_inproj_kernel, has_pos),
        grid=(t // tm, n_steps),
        in_specs=in_specs,
        out_specs=[pl.BlockSpec((tm, D_MODEL), lambda i, j: (i, token_major_block(j))),
                   pl.BlockSpec((D_MODEL, tm), lambda i, j: (feature_major_block(j), i)),
                   pl.BlockSpec((tm, GATE_PAD), lambda i, j: (i, 0)),
                   pl.BlockSpec((GATE_PAD, tm), lambda i, j: (0, i))],
        out_shape=[jax.ShapeDtypeStruct((t, _N_TOKEN_MAJOR * D_MODEL), BF16),
                   jax.ShapeDtypeStruct((len(_FEATURE_MAJOR_STEPS) * D_MODEL, t), BF16),
                   jax.ShapeDtypeStruct((t, GATE_PAD), F32),
                   jax.ShapeDtypeStruct((GATE_PAD, t), F32)],
        scratch_shapes=[pltpu.VMEM((tm, D_MODEL), BF16)],
        compiler_params=_params(2),
        name="inproj",
    )(*args)


_SEQ_INPUTS = 10


def _mlstm_kernel(has_state, want_state, nc, n_par, *refs):
    refs = list(refs)
    seq_in = [refs[a * _SEQ_INPUTS:(a + 1) * _SEQ_INPUTS] for a in range(n_par)]
    pos = n_par * _SEQ_INPUTS
    if has_state:
        c0_ref, n0_ref, m0_ref = refs[pos:pos + 3]
        pos += 3
    hf_ref, hb_ref = refs[pos:pos + 2]
    pos += 2
    if want_state:
        co_ref, no_ref, mo_ref = refs[pos:pos + 3]
        pos += 3
    c_all, n_all, m_all = refs[pos:pos + 3]
    step = pl.program_id(1)

    @pl.when(step == 0)
    def _():
        if has_state:
            c_all[...] = c0_ref[...]
            n_all[...] = n0_ref[...]
            m_all[...] = m0_ref[...]
        else:
            c_all[...] = jnp.zeros_like(c_all)
            n_all[...] = jnp.zeros_like(n_all)
            m_all[...] = jnp.zeros_like(m_all)

    row = lax.broadcasted_iota(jnp.int32, (CHUNK, CHUNK), 0)
    col = lax.broadcasted_iota(jnp.int32, (CHUNK, CHUNK), 1)

    streams = []
    for a in range(n_par):
        qf, kf, vf, qb, kb, vb, gf, gb, gtf, gtb = seq_in[a]
        streams.append((a, 0, qf, kf, vf, gf, gtf, hf_ref))
        streams.append((a, 1, qb, kb, vb, gb, gtb, hb_ref))
    for a, d, qt_ref, k_ref, vt_ref, g_ref, gt_ref, h_ref in streams:
        c_scr, n_scr, m_scr = c_all.at[a], n_all.at[a], m_all.at[a]
        visible = (row <= col) if d == 0 else (row >= col)
        tri_t = jnp.where(visible, 1.0, 0.0).astype(BF16)
        tri = jnp.where((col <= row) if d == 0 else (col >= row), 1.0, 0.0).astype(BF16)
        g = g_ref[...]
        g_t = gt_ref[...]
        l1, l2, l3 = _split3(_log_sigmoid(g))
        b_col_all = _dot(tri, l1) + _dot(tri, l2) + _dot(tri, l3)
        t1, t2, t3 = _split3(_log_sigmoid(g_t))
        b_row_all = _dot(t1, tri_t) + _dot(t2, tri_t) + _dot(t3, tri_t)
        last = CHUNK - 1 if d == 0 else 0
        for h in range(N_HEADS):
            u = d * N_HEADS + h
            ci = d * 2 * N_HEADS + h
            cf = ci + N_HEADS
            b_row = b_row_all[cf:cf + 1, :]
            i_row = g_t[ci:ci + 1, :]
            src_col = g[:, ci:ci + 1] - b_col_all[:, cf:cf + 1]
            m = m_scr[u:u + 1, 0:1]
            hs = slice(h * HEAD_DIM, (h + 1) * HEAD_DIM)
            q_t = qt_ref[hs, :]
            k = k_ref[:, hs]
            v_t = vt_ref[hs, :]
            c_state = c_scr[u]
            n_state = n_scr[u:u + 1, :]

            dmat_t = jnp.where(visible, src_col + b_row, -jnp.inf)
            inter = b_row + m
            m_t = jnp.maximum(inter, jnp.max(dmat_t, axis=0, keepdims=True))
            w_st = jnp.exp(dmat_t - m_t)
            a_t = jnp.exp(inter - m_t)
            s_t = _dot(k, q_t) * w_st
            num_t = _dot(v_t, s_t.astype(BF16)) + a_t * _dot(c_state.astype(BF16), q_t)
            n_rows = jnp.broadcast_to(n_state, (_F32_ROWS, HEAD_DIM)).astype(BF16)
            den = jnp.sum(s_t, axis=0, keepdims=True) + a_t * _dot(n_rows, q_t)[0:1, :]
            h_ref[hs, a * CHUNK:(a + 1) * CHUNK] = num_t / jnp.maximum(jnp.abs(den), jnp.exp(-m_t))

            b_last = b_row[:, last:last + 1]
            dec_row = b_last - b_row + i_row
            m_new = jnp.maximum(b_last + m, jnp.max(dec_row, axis=-1, keepdims=True))
            w_row = jnp.exp(dec_row - m_new)
            a_c = jnp.exp(b_last + m - m_new)
            vw_t = (v_t.astype(F32) * w_row).astype(BF16)
            c_scr[u] = a_c * c_state + _dot(vw_t, k)
            w_rows = jnp.broadcast_to(w_row, (_F32_ROWS, CHUNK)).astype(BF16)
            n_scr[u:u + 1, :] = a_c * n_state + _dot(w_rows, k)[0:1, :]
            m_scr[u:u + 1, :] = jnp.broadcast_to(m_new, (1, GATE_PAD))

    if want_state:
        @pl.when(step == nc - 1)
        def _():
            co_ref[...] = c_all[...]
            no_ref[...] = n_all[...]
            mo_ref[...] = m_all[...]


def _mlstm(proj, projt, gates, gatest, n_seq, seq_len, state0, want_state):
    t = proj.shape[0]
    nc = seq_len // CHUNK
    units = 2 * N_HEADS
    has_state = state0 is not None

    n_par = _scan_group(n_seq)
    tile = (CHUNK, D_MODEL)
    tile_t = (D_MODEL, CHUNK)
    in_specs, args = [], []
    for a in range(n_par):
        def fwd(c, a=a):
            return lambda p, k: ((p * n_par + a) * nc + k, c)

        def bwd(c, a=a):
            return lambda p, k: ((p * n_par + a) * nc + nc - 1 - k, c)

        def fwd_t(r, a=a):
            return lambda p, k: (r, (p * n_par + a) * nc + k)

        def bwd_t(r, a=a):
            return lambda p, k: (r, (p * n_par + a) * nc + nc - 1 - k)

        in_specs += [pl.BlockSpec(tile_t, fwd_t(0)), pl.BlockSpec(tile, fwd(0)), pl.BlockSpec(tile_t, fwd_t(1)),
                     pl.BlockSpec(tile_t, bwd_t(0)), pl.BlockSpec(tile, bwd(0)), pl.BlockSpec(tile_t, bwd_t(1)),
                     pl.BlockSpec((CHUNK, GATE_PAD), fwd(0)), pl.BlockSpec((CHUNK, GATE_PAD), bwd(0)),
                     pl.BlockSpec((GATE_PAD, CHUNK), fwd_t(0)), pl.BlockSpec((GATE_PAD, CHUNK), bwd_t(0))]
        args += [projt, proj, projt, projt, proj, projt, gates, gates, gatest, gatest]
    state_specs = [pl.BlockSpec((n_par, units, HEAD_DIM, HEAD_DIM), lambda p, k: (p, 0, 0, 0)),
                   pl.BlockSpec((n_par, units, HEAD_DIM), lambda p, k: (p, 0, 0)),
                   pl.BlockSpec((n_par, units, GATE_PAD), lambda p, k: (p, 0, 0))]
    state_shapes = [jax.ShapeDtypeStruct((n_seq, units, HEAD_DIM, HEAD_DIM), F32),
                    jax.ShapeDtypeStruct((n_seq, units, HEAD_DIM), F32),
                    jax.ShapeDtypeStruct((n_seq, units, GATE_PAD), F32)]
    if has_state:
        in_specs += state_specs
        args += list(state0)
    group_tile = (D_MODEL, n_par * CHUNK)
    out_specs = [pl.BlockSpec(group_tile, lambda p, k: (0, p * nc + k)),
                 pl.BlockSpec(group_tile, lambda p, k: (0, p * nc + nc - 1 - k))]
    out_shape = [jax.ShapeDtypeStruct((D_MODEL, t), F32), jax.ShapeDtypeStruct((D_MODEL, t), F32)]
    if want_state:
        out_specs += state_specs
        out_shape += state_shapes
    return pl.pallas_call(
        functools.partial(_mlstm_kernel, has_state, want_state, nc, n_par),
        grid=(n_seq // n_par, nc),
        in_specs=in_specs,
        out_specs=out_specs,
        out_shape=out_shape,
        scratch_shapes=[pltpu.VMEM((n_par, units, HEAD_DIM, HEAD_DIM), F32),
                        pltpu.VMEM((n_par, units, HEAD_DIM), F32),
                        pltpu.VMEM((n_par, units, GATE_PAD), F32)],
        compiler_params=_params(2),
        name="mlstm",
    )(*args)


def _scan_group(n_seq):
    return 2 if n_seq % 2 == 0 else 1


def _scan_column(seq, chunk, nc, n_par):
    return ((seq // n_par) * nc + chunk) * n_par + seq % n_par


_CONV_ROWS = 64
_LANES = 128


def _tail_kernel(has_pos, tm, tiles_per_seq, alpha, *refs):
    refs = list(refs)
    n_ct = tm // CHUNK
    h_refs = refs[:2 * n_ct]
    so_ref, val_ref, sg_ref, sga_ref, sgb_ref, vp_ref, gp_ref, vn_ref, gn_ref, x_ref = refs[2 * n_ct:2 * n_ct + 10]
    pos = 2 * n_ct + 10
    if has_pos:
        pos_ref = refs[pos]
        pos += 1
    (mod_ref, normw_ref, wa_ref, wc_ref, wo_ref, dww_ref, dwb_ref, clnw_ref, clnb_ref, ln1w_ref, ln1b_ref, wq_ref,
     x1_ref, u2_ref, q_ref, xpad, conv_scr, shifted) = refs[pos:]
    i = pl.program_id(0)

    hsum = jnp.concatenate([h_refs[j][...] + h_refs[n_ct + j][...] for j in range(n_ct)], axis=1)
    parts = []
    for h in range(N_HEADS):
        hh = hsum[h * HEAD_DIM:(h + 1) * HEAD_DIM, :]
        mu = jnp.mean(hh, axis=0, keepdims=True)
        hc = hh - mu
        var = jnp.mean(hc * hc, axis=0, keepdims=True)
        parts.append(hc * lax.rsqrt(var + LN_EPS))
    hn = jnp.concatenate(parts, axis=0) * _lane_tile(normw_ref[...], tm)
    hg = (so_ref[...].astype(F32) * hn).T.astype(BF16)
    branch_a = _dot(hg, wa_ref[...])

    first = (i % tiles_per_seq) == 0
    last = (i % tiles_per_seq) == tiles_per_seq - 1
    keep_prev = jnp.where(first, 0.0, 1.0)
    keep_next = jnp.where(last, 0.0, 1.0)
    xpad[0:CONV_HALO, :] = vp_ref[...].astype(F32) * gp_ref[...].astype(F32) * keep_prev
    xpad[CONV_HALO:CONV_HALO + tm, :] = val_ref[...].astype(F32) * sg_ref[...].astype(F32)
    xpad[CONV_HALO + tm:2 * CONV_HALO + tm, :] = vn_ref[...].astype(F32) * gn_ref[...].astype(F32) * keep_next
    tap0 = CONV_HALO - CONV_WIDTH // 2

    n_shift_rows = tm + 2 * CONV_HALO - _F32_ROWS

    def col_body(c, carry):
        cs = pl.ds(pl.multiple_of(c * _LANES, _LANES), _LANES)
        for r in range(1, _F32_ROWS):
            shifted[r - 1, 0:n_shift_rows, :] = xpad[pl.ds(r, n_shift_rows), cs]
        for rb in range(tm // _CONV_ROWS):
            acc = jnp.broadcast_to(dwb_ref[:, cs], (_CONV_ROWS, _LANES))
            for k in range(CONV_WIDTH):
                tiles, r = divmod(tap0 + k, _F32_ROWS)
                start = tiles * _F32_ROWS + rb * _CONV_ROWS
                if r == 0:
                    src = xpad[pl.ds(start, _CONV_ROWS), cs]
                else:
                    src = shifted[r - 1, pl.ds(start, _CONV_ROWS), :]
                acc = acc + src * dww_ref[k:k + 1, cs]
            conv_scr[pl.ds(rb * _CONV_ROWS, _CONV_ROWS), cs] = acc
        return carry

    lax.fori_loop(0, D_MODEL // _LANES, col_body, 0)
    xc = _layer_norm(conv_scr[...], clnw_ref[...], clnb_ref[...])
    xc = (xc * _sigmoid(xc)).astype(BF16)
    branch_b = _dot(xc, wc_ref[...])

    merged = sga_ref[...].astype(F32) * branch_a + sgb_ref[...].astype(F32) * branch_b
    mix = _dot(merged.astype(BF16), wo_ref[...])

    mod = mod_ref[...]
    gate1 = mod[:, 2 * D_MODEL:3 * D_MODEL]
    shift2 = mod[:, 3 * D_MODEL:4 * D_MODEL]
    scale2 = mod[:, 4 * D_MODEL:5 * D_MODEL]
    x = x_ref[...]
    if has_pos:
        x = x + pos_ref[...]
    x1 = _layer_norm(alpha * x + gate1 * mix, ln1w_ref[...], ln1b_ref[...])
    x1_ref[...] = x1
    u2 = (x1 * (1.0 + scale2) + shift2).astype(BF16)
    u2_ref[...] = u2
    q_ref[...] = _dot(u2, wq_ref[...]).astype(BF16)


def _tail(hf, hb, proj, projt, x, pos, mod, row_fn, seq_len, tm, alpha, lw):
    t = x.shape[0]
    has_pos = pos is not None
    tiles_per_seq = seq_len // tm
    hb_per_tile = tm // CONV_HALO
    n_halo = t // CONV_HALO
    big = (tm, D_MODEL)
    halo = (CONV_HALO, D_MODEL)

    def colspec(c):
        return pl.BlockSpec(big, lambda i: (i, c))

    def prev(c):
        return pl.BlockSpec(halo, lambda i: (jnp.maximum(i * hb_per_tile - 1, 0), c))

    def nxt(c):
        return pl.BlockSpec(halo, lambda i: (jnp.minimum((i + 1) * hb_per_tile, n_halo - 1), c))

    def const(shape):
        return pl.BlockSpec(shape, lambda i: (0,) * len(shape), pipeline_mode=pl.Buffered(1))

    def rowspec(r):
        return pl.BlockSpec((D_MODEL, tm), lambda i: (r, i))

    nc = seq_len // CHUNK
    n_par = _scan_group(t // seq_len)
    n_ct = tm // CHUNK

    def scan_chunk(j):
        return pl.BlockSpec((D_MODEL, CHUNK), lambda i: (
            0, _scan_column(i // tiles_per_seq, (i % tiles_per_seq) * n_ct + j, nc, n_par)))

    in_specs = [scan_chunk(j) for j in range(n_ct)] * 2
    in_specs += [rowspec(2), colspec(1), colspec(2), colspec(3), colspec(4),
                 prev(1), prev(2), nxt(1), nxt(2), colspec(0)]
    args = [hf] * n_ct + [hb] * n_ct + [projt, proj, proj, proj, proj, proj, proj, proj, proj, x]
    if has_pos:
        in_specs.append(pl.BlockSpec(big, lambda i: (i % tiles_per_seq, 0)))
        args.append(pos)
    in_specs += [pl.BlockSpec((None, 1, mod.shape[2]), lambda i: (row_fn(i), 0, 0)),
                 const((D_MODEL, _LANES)), const((D_MODEL, D_MODEL)), const((D_MODEL, D_MODEL)),
                 const((D_MODEL, D_MODEL)),
                 const((32, D_MODEL)), const((1, D_MODEL)), const((1, D_MODEL)), const((1, D_MODEL)),
                 const((1, D_MODEL)), const((1, D_MODEL)), const(lw["wq"].shape)]
    args += [mod, lw["norm_w"], lw["w_a"], lw["w_cout"], lw["w_out"], lw["dw_w"], lw["dw_b"], lw["cln_w"],
             lw["cln_b"], lw["ln1_w"], lw["ln1_b"], lw["wq"]]
    n_q = lw["wq"].shape[1]
    return pl.pallas_call(
        functools.partial(_tail_kernel, has_pos, tm, tiles_per_seq, alpha),
        grid=(t // tm,),
        in_specs=in_specs,
        out_specs=[pl.BlockSpec(big, lambda i: (i, 0)), pl.BlockSpec(big, lambda i: (i, 0)),
                   pl.BlockSpec((tm, n_q), lambda i: (i, 0))],
        out_shape=[jax.ShapeDtypeStruct((t, D_MODEL), F32), jax.ShapeDtypeStruct((t, D_MODEL), BF16),
                   jax.ShapeDtypeStruct((t, n_q), BF16)],
        scratch_shapes=[pltpu.VMEM((tm + 2 * CONV_HALO, D_MODEL), F32), pltpu.VMEM((tm, D_MODEL), F32),
                        pltpu.VMEM((_F32_ROWS - 1, tm + 2 * CONV_HALO, _LANES), F32)],
        compiler_params=_params(1),
        name="mixer_tail",
    )(*args)


_TOK_BLOCKS = 8
_LANE = 128
_ROUTE_TM = _TOK_BLOCKS * _LANE


def _merge_exchange_pairs(n):
    pairs = []
    t = (n - 1).bit_length()
    p = 1 << (t - 1)
    while p > 0:
        q, r, d = 1 << (t - 1), 0, p
        while d > 0:
            pairs.extend((i, i + d) for i in range(n - d) if (i & p) == r)
            d, q, r = q - p, q >> 1, p
        p >>= 1
    return tuple(pairs)


_SORT16 = _merge_exchange_pairs(PEER_TOPK)


def _sort_desc(vals):
    vals = list(vals)
    for i, j in _SORT16:
        vals[i], vals[j] = jnp.maximum(vals[i], vals[j]), jnp.minimum(vals[i], vals[j])
    return vals


def _bitonic_desc(vals):
    vals = list(vals)
    d = len(vals) // 2
    while d > 0:
        for i in range(len(vals)):
            if (i & d) == 0:
                vals[i], vals[i + d] = jnp.maximum(vals[i], vals[i + d]), jnp.minimum(vals[i], vals[i + d])
        d //= 2
    return vals


def _top_merge(a, b):
    n = len(a)
    return _bitonic_desc([jnp.maximum(a[i], b[n - 1 - i]) for i in range(n)])


def _top16(keys):
    if len(keys) == PEER_TOPK:
        return _sort_desc(keys)
    half = len(keys) // 2
    return _top_merge(_top16(keys[:half]), _top16(keys[half:]))


def _prefix_count(test, v):
    t8 = test(v[7])
    t4 = test(jnp.where(t8, v[11], v[3]))
    t2 = test(jnp.where(t8, jnp.where(t4, v[13], v[9]), jnp.where(t4, v[5], v[1])))
    lo = jnp.where(t4, jnp.where(t2, v[6], v[4]), jnp.where(t2, v[2], v[0]))
    hi = jnp.where(t4, jnp.where(t2, v[14], v[12]), jnp.where(t2, v[10], v[8]))
    t1 = test(jnp.where(t8, hi, lo))
    cnt = (jnp.where(t8, 8.0, 0.0) + jnp.where(t4, 4.0, 0.0)) + (jnp.where(t2, 2.0, 0.0) + jnp.where(t1, 1.0, 0.0))
    return jnp.where(test(v[15]), 16.0, cnt)


def _joint_top16(a, b):
    q0 = [a[p] + b[0] for p in range(16)]
    q1 = [a[p] + b[1] for p in range(8)]
    p0 = [a[0] + b[q] for q in range(8, 16)]
    m1 = _bitonic_desc(q1 + p0[::-1])
    m2 = _sort_desc([a[p] + b[q] for q, n in ((2, 5), (3, 4), (4, 3), (5, 2), (6, 2)) for p in range(n)])
    q7 = [a[0] + b[7], a[1] + b[7]]
    t2 = _bitonic_desc(m2[:14] + [jnp.maximum(m2[14], q7[1]), jnp.maximum(m2[15], q7[0])])
    return _top_merge(_top_merge(q0, m1), t2)


def _route_kernel(q_ref, skx_ref, rank2_ref, e2_ref, r_ref, a1_ref, km_rank2, km_e2, km_r, km_a1):
    nb = _TOK_BLOCKS
    keys = []
    for p in range(2):
        qp = jnp.concatenate([q_ref[a * _LANE:(a + 1) * _LANE, p * PEER_NKEYS:(p + 1) * PEER_NKEYS]
                              for a in range(nb)], axis=1)
        sp = _dot_nt(skx_ref[p], qp)
        keys.append([sp[k * nb:(k + 1) * nb, :] for k in range(PEER_NKEYS)])
    s1, s2 = keys
    a = _top16(s1)
    b = _top16(s2)
    top = _joint_top16(a, b)
    tau = top[PEER_TOPK - 1]
    zsum = jnp.ones_like(tau)
    for c in top[1:]:
        zsum = zsum + jnp.exp(c - top[0])
    inv_z = 1.0 / zsum
    b_asc = b[::-1]
    for k in range(PEER_NKEYS):
        rows = slice(k * nb, (k + 1) * nb)
        km_r[rows, :] = _prefix_count(lambda bq, x=s1[k]: x + bq >= tau, b)
        km_a1[rows, :] = jnp.exp(s1[k] - a[0]) * inv_z
        km_rank2[rows, :] = float(PEER_TOPK) - _prefix_count(lambda bq, x=s2[k]: x >= bq, b_asc)
        km_e2[rows, :] = jnp.exp(s2[k] - b[0])
    for blk in range(nb):
        cols = slice(blk * _LANE, (blk + 1) * _LANE)
        rows = pl.ds(blk, PEER_NKEYS, stride=nb)
        rank2_ref[:, cols] = km_rank2[rows, :].astype(BF16)
        e2_ref[:, cols] = km_e2[rows, :].astype(BF16)
        r_ref[:, cols] = km_r[rows, :]
        a1_ref[:, cols] = km_a1[rows, :]


def _route(q, skx):
    t = q.shape[0]
    tm = _ROUTE_TM
    assert t % tm == 0, (t, tm)
    out_blk = pl.BlockSpec((None, PEER_NKEYS, tm), lambda h, i: (h, 0, i))
    shape = (PEER_HEADS, PEER_NKEYS, t)
    km = pltpu.VMEM((PEER_NKEYS * _TOK_BLOCKS, _LANE), F32)
    return pl.pallas_call(
        _route_kernel,
        grid=(PEER_HEADS, t // tm),
        in_specs=[pl.BlockSpec((tm, 2 * PEER_NKEYS), lambda h, i: (i, h)),
                  pl.BlockSpec((None, 2, PEER_NKEYS * _TOK_BLOCKS, _TOK_BLOCKS * PEER_NKEYS),
                               lambda h, i: (h, 0, 0, 0))],
        out_specs=[out_blk, out_blk, out_blk, out_blk],
        out_shape=[jax.ShapeDtypeStruct(shape, BF16), jax.ShapeDtypeStruct(shape, BF16),
                   jax.ShapeDtypeStruct(shape, F32), jax.ShapeDtypeStruct(shape, F32)],
        scratch_shapes=[km, km, km, km],
        compiler_params=_params(2),
        name="peer_route",
    )(q, skx)


_GELU_C = math.sqrt(2.0 / math.pi)
_BF16_ROWS = 16
_F32_ROWS = 8
_MXU_N = 256


def _gelu_tanh(x):
    neg2z = x * (x * x * (-2.0 * _GELU_C * 0.044715) + (-2.0 * _GELU_C))
    return x / (1.0 + jnp.exp(neg2z))


def _experts_kernel(ib, n_eblk, alpha, u_ref, ut_ref, vt_ref, rank2_ref, e2_ref, r_ref, a1_ref, x1_ref, mod_ref,
                    lnw_ref, lnb_ref, y_ref, acc_scr, at_scr, p_scr, uT_scr):
    s = pl.program_id(1)

    @pl.when(s == 0)
    def _():
        acc_scr[...] = jnp.zeros_like(acc_scr)
        uT_scr[...] = u_ref[...].astype(F32).T.astype(BF16)

    tm = u_ref.shape[0]
    key_tiles = PEER_NKEYS // _BF16_ROWS
    rows_per_chunk = _MXU_N // PEER_NKEYS
    n_chunks = ib // rows_per_chunk

    def tile_rows(ref, h, ii, cols):
        group = ref[h, pl.ds(pl.multiple_of(s * ib + (ii // _F32_ROWS) * _F32_ROWS, _F32_ROWS), _F32_ROWS), cols]
        row = group[ii % _F32_ROWS:ii % _F32_ROWS + 1, :]
        tile = jnp.broadcast_to(row, (_BF16_ROWS, _MXU_N)).astype(BF16)
        return jnp.concatenate([tile] * key_tiles, axis=0)

    def weights(ii):
        rows = slice(ii * PEER_NKEYS, (ii + 1) * PEER_NKEYS)
        for c in range(tm // _MXU_N):
            cols = slice(c * _MXU_N, (c + 1) * _MXU_N)
            act = _gelu_tanh(at_scr[rows, cols].astype(BF16))
            w = None
            for h in range(PEER_HEADS):
                r_b = tile_rows(r_ref, h, ii, cols)
                a_b = tile_rows(a1_ref, h, ii, cols)
                term = jnp.where(rank2_ref[h, :, cols] < r_b, e2_ref[h, :, cols] * a_b, jnp.zeros((), BF16))
                w = term if w is None else w + term
            p_scr[rows, cols] = w * act

    for ch in range(n_chunks):
        erows = slice(ch * _MXU_N, (ch + 1) * _MXU_N)
        at_scr[erows, :] = _dot(ut_ref[erows, :], uT_scr[...])
    for ch in range(n_chunks):
        erows = slice(ch * _MXU_N, (ch + 1) * _MXU_N)
        for ii in range(ch * rows_per_chunk, (ch + 1) * rows_per_chunk):
            weights(ii)
        acc_scr[...] += _dot(vt_ref[:, erows], p_scr[erows, :])

    @pl.when(s == n_eblk - 1)
    def _():
        mod = mod_ref[...]
        gate2 = mod[:, 5 * D_MODEL:6 * D_MODEL]
        y = alpha * x1_ref[...] + gate2 * acc_scr[...].T
        y_ref[...] = _layer_norm(y, lnw_ref[...], lnb_ref[...])


def _experts(u2, u_tab, vt_tab, rank2, e2, r, a1, x1, mod, row_fn, lnw, lnb, tm, eb, alpha):
    t = u2.shape[0]
    n_exp = u_tab.shape[0]
    ib = eb // PEER_NKEYS
    n_eblk = n_exp // eb
    full = pl.BlockSpec((PEER_HEADS, PEER_NKEYS, tm), lambda i, s: (0, 0, i))
    tok = pl.BlockSpec((tm, D_MODEL), lambda i, s: (i, 0))
    vec = pl.BlockSpec((1, D_MODEL), lambda i, s: (0, 0))
    return pl.pallas_call(
        functools.partial(_experts_kernel, ib, n_eblk, alpha),
        grid=(t // tm, n_eblk),
        in_specs=[tok,
                  pl.BlockSpec((eb, D_MODEL), lambda i, s: (s, 0)),
                  pl.BlockSpec((D_MODEL, eb), lambda i, s: (0, s)),
                  full, full, full, full, tok,
                  pl.BlockSpec((None, 1, mod.shape[2]), lambda i, s: (row_fn(i), 0, 0)),
                  vec, vec],
        out_specs=tok,
        out_shape=jax.ShapeDtypeStruct((t, D_MODEL), F32),
        scratch_shapes=[pltpu.VMEM((D_MODEL, tm), F32), pltpu.VMEM((eb, tm), F32),
                        pltpu.VMEM((eb, tm), BF16), pltpu.VMEM((D_MODEL, tm), BF16)],
        compiler_params=_params(2),
        name="peer_experts",
    )(u2, u_tab, vt_tab, rank2, e2, r, a1, x1, mod, lnw, lnb)


def _grid_pos_embed(n_tokens):
    rows = n_tokens // GRID_W
    quarter = D_MODEL // 4
    freqs = jnp.exp(-math.log(POS_BASE) * jnp.arange(quarter, dtype=F32) / quarter)
    r = jnp.arange(rows, dtype=F32)[:, None] * freqs
    cl = jnp.arange(GRID_W, dtype=F32)[:, None] * freqs
    er = jnp.concatenate([jnp.sin(r), jnp.cos(r)], -1)
    ec = jnp.concatenate([jnp.sin(cl), jnp.cos(cl)], -1)
    emb = jnp.concatenate([jnp.broadcast_to(er[:, None, :], (rows, GRID_W, D_MODEL // 2)),
                           jnp.broadcast_to(ec[None, :, :], (rows, GRID_W, D_MODEL // 2))], -1)
    return emb.reshape(rows * GRID_W, D_MODEL)


def _pick_tile(seq_len, n_tokens, target):
    tm = min(target, seq_len)
    while seq_len % tm or n_tokens % tm:
        tm //= 2
    return tm


def _block(x, pos, mod, n_seq, seq_len, tm_in, row_of_token_tile, state0, want_state, lw, alpha):
    t = x.shape[0]
    proj, projt, gates, gatest = _inproj(x, pos, mod, row_of_token_tile(tm_in), lw, tm_in)
    res = _mlstm(proj, projt, gates, gatest, n_seq, seq_len, state0, want_state)
    hf, hb = res[0], res[1]
    tm_tail = _pick_tile(seq_len, t, 256)
    x1, u2, q = _tail(hf, hb, proj, projt, x, pos, mod, row_of_token_tile(tm_tail), seq_len, tm_tail, alpha, lw)
    tm_p = _pick_tile(seq_len, t, 512)
    rank2, e2, r, a1 = _route(q, lw["skx"])
    y = _experts(u2, lw["peer_u"], lw["peer_vt"], rank2, e2, r, a1, x1, mod, row_of_token_tile(tm_p),
                 lw["ln2_w"], lw["ln2_b"], tm_p, 2048, alpha)
    return y, res[2:]


def kernel(x_prompt, x_sample, state_C, state_n, state_m, c, c_ctx, w_in, b_in, mlstm_norm_w, w_a, conv_dw_w,
           conv_dw_b, conv_ln_w, conv_ln_b, w_conv_out, w_out, w_mod, b_mod, ln1_w, ln1_b, ln2_w, ln2_b,
           peer_w_query, peer_subkeys, peer_u, peer_v):
    depth = w_in.shape[0]
    alpha = (2.0 * depth) ** 0.25
    bsz, seq, _ = x_prompt.shape
    dbsz, dseq, _ = x_sample.shape
    units = 2 * N_HEADS
    gate_off = 4 * D_MODEL

    ctx = x_prompt.reshape(bsz * seq, D_MODEL)
    lat = x_sample.reshape(dbsz * dseq, D_MODEL)
    pos = _grid_pos_embed(dseq)
    n_rows = 1 + dbsz
    pad_rows = (-n_rows) % 8
    cvec = jnp.concatenate([c_ctx[None, :], c, jnp.zeros((pad_rows, D_MODEL), F32)], axis=0)

    new_c, new_n, new_m = [], [], []
    for l in range(depth):
        vec = lambda a: a[l].reshape(1, -1)
        w_l = w_in[l]
        b_l = b_in[l]
        wg = jnp.pad(w_l[:, gate_off:gate_off + N_GATES], ((0, 0), (0, GATE_PAD - N_GATES)))
        wg_hi = wg.astype(BF16)
        wg_lo = (wg - wg_hi.astype(F32)).astype(BF16)
        b_gate = jnp.pad(b_l[gate_off:gate_off + N_GATES], (0, GATE_PAD - N_GATES))
        w_cols = jnp.concatenate([w_l[:, :gate_off], w_l[:, gate_off + N_GATES:]], axis=1).astype(BF16)
        b_cols = jnp.concatenate([b_l[:gate_off], b_l[gate_off + N_GATES:]])
        n_steps = w_cols.shape[1] // D_MODEL
        blocks = [w_cols[:, s * D_MODEL:(s + 1) * D_MODEL] for s in range(n_steps)]
        lanes = lambda col: jnp.broadcast_to(col[:, None], (col.shape[0], _LANES))
        lw = {
            "w_main": jnp.concatenate([blk.T if s in _FEATURE_MAJOR_STEPS else blk for s, blk in enumerate(blocks)],
                                      axis=1),
            "b_main": b_cols.reshape(1, -1),
            "b_feat": jnp.stack([lanes(b_cols[s * D_MODEL:(s + 1) * D_MODEL]) for s in _FEATURE_MAJOR_STEPS]),
            "wg_hi": wg_hi, "wg_lo": wg_lo, "b_gate": b_gate.reshape(1, -1),
            "norm_w": lanes(mlstm_norm_w[l]), "w_a": w_a[l].astype(BF16), "w_cout": w_conv_out[l].astype(BF16),
            "w_out": w_out[l].astype(BF16),
            "dw_w": jnp.pad(conv_dw_w[l], ((0, 32 - CONV_WIDTH), (0, 0))), "dw_b": vec(conv_dw_b),
            "cln_w": vec(conv_ln_w), "cln_b": vec(conv_ln_b), "ln1_w": vec(ln1_w), "ln1_b": vec(ln1_b),
            "ln2_w": vec(ln2_w), "ln2_b": vec(ln2_b),
            "wq": peer_w_query[l].astype(BF16),
            "skx": jnp.einsum("hpkd,ab->hpkabd", peer_subkeys[l].astype(BF16), jnp.eye(_TOK_BLOCKS, dtype=BF16))
            .reshape(PEER_HEADS, 2, PEER_NKEYS * _TOK_BLOCKS, _TOK_BLOCKS * PEER_NKEYS),
            "peer_u": peer_u[l].astype(BF16),
            "peer_vt": peer_v[l].astype(BF16).T,
        }
        mod = _modulation(cvec, w_mod[l], b_mod[l]).reshape(n_rows + pad_rows, 1, 6 * D_MODEL)

        ctx, (c_fin, n_fin, m_fin) = _block(ctx, None, mod, bsz, seq, _pick_tile(bsz * seq, bsz * seq, 1024),
                                             lambda tm: (lambda i: 0), None, True, lw, alpha)
        new_c.append(c_fin.reshape(bsz, 2, N_HEADS, HEAD_DIM, HEAD_DIM))
        new_n.append(n_fin.reshape(bsz, 2, N_HEADS, HEAD_DIM))
        new_m.append(m_fin[:, :, 0].reshape(bsz, 2, N_HEADS))

        state0 = (state_C[:, l].reshape(dbsz, units, HEAD_DIM, HEAD_DIM),
                  state_n[:, l].reshape(dbsz, units, HEAD_DIM),
                  jnp.broadcast_to(state_m[:, l].reshape(dbsz, units, 1), (dbsz, units, GATE_PAD)))
        lat, _ = _block(lat, pos, mod, dbsz, dseq, _pick_tile(dseq, dbsz * dseq, 1024),
                        lambda tm: (lambda i: 1 + (i * tm) // dseq), state0, False, lw, alpha)

    def per_layer(parts):
        return parts[0][:, None] if depth == 1 else jnp.stack(parts, axis=1)

    return (ctx.reshape(bsz, seq, D_MODEL), lat.reshape(dbsz, dseq, D_MODEL),
            per_layer(new_c), per_layer(new_n), per_layer(new_m))
```

```python
import functools
import math

import jax
import jax.numpy as jnp
from jax import lax
from jax.experimental import pallas as pl
from jax.experimental.pallas import tpu as pltpu

F32 = jnp.float32
BF16 = jnp.bfloat16

D_MODEL = 1024
N_HEADS = 4
HEAD_DIM = 256
CHUNK = 256
CONV_WIDTH = 31
CONV_HALO = 16
N_GATES = 16
GATE_PAD = 128
GRID_W = 64
POS_BASE = 10000.0
LN_EPS = 1e-6
PEER_HEADS = 8
PEER_NKEYS = 128
PEER_TOPK = 16
V7X_VMEM_LIMIT = 56 * 1024 * 1024


def _params(n_axes):
    return pltpu.CompilerParams(dimension_semantics=("arbitrary",) * n_axes,
                                vmem_limit_bytes=V7X_VMEM_LIMIT)


def _sigmoid(x):
    return 0.5 + 0.5 * jnp.tanh(0.5 * x)


def _log_sigmoid(x):
    return jnp.minimum(x, 0.0) - jnp.log(1.0 + jnp.exp(-jnp.abs(x)))


def _dot(a, b):
    return jnp.dot(a, b, preferred_element_type=F32)


def _dot_nt(a, b):
    return lax.dot_general(a, b, (((1,), (1,)), ((), ())), preferred_element_type=F32)


def _split2(x):
    hi = x.astype(BF16)
    lo = (x - hi.astype(F32)).astype(BF16)
    return hi, lo


def _split3(x):
    a = x.astype(BF16)
    r = x - a.astype(F32)
    b = r.astype(BF16)
    c = (r - b.astype(F32)).astype(BF16)
    return a, b, c


def _layer_norm(x, w, b):
    mu = jnp.mean(x, axis=-1, keepdims=True)
    xc = x - mu
    var = jnp.mean(xc * xc, axis=-1, keepdims=True)
    return xc * lax.rsqrt(var + LN_EPS) * w + b


def _mod_kernel(c_ref, w_ref, b_ref, o_ref):
    c = c_ref[...]
    s = c * _sigmoid(c)
    sh, sl = _split2(s)
    wh, wl = _split2(w_ref[...])
    o_ref[...] = _dot(sh, wh) + _dot(sl, wh) + _dot(sh, wl) + b_ref[...]


def _modulation(cvec, w_mod, b_mod):
    rows = cvec.shape[0]
    tn = 1536
    n = w_mod.shape[1]
    return pl.pallas_call(
        _mod_kernel,
        grid=(n // tn,),
        in_specs=[pl.BlockSpec((rows, D_MODEL), lambda j: (0, 0)),
                  pl.BlockSpec((D_MODEL, tn), lambda j: (0, j)),
                  pl.BlockSpec((1, tn), lambda j: (0, j))],
        out_specs=pl.BlockSpec((rows, tn), lambda j: (0, j)),
        out_shape=jax.ShapeDtypeStruct((rows, n), F32),
        compiler_params=_params(1),
        name="modulation",
    )(cvec, w_mod, b_mod.reshape(1, n))


_STEP_K = 1
_FEATURE_MAJOR_STEPS = (0, 2, 3)
_N_TOKEN_MAJOR = 5


def _lane_tile(block, n_lanes):
    return jnp.concatenate([block] * (n_lanes // block.shape[1]), axis=1)


def _inproj_kernel(has_pos, *refs):
    if has_pos:
        (x_ref, pos_ref, mod_ref, w_ref, b_ref, bt_ref, wgh_ref, wgl_ref, bg_ref,
         proj_ref, projt_ref, gates_ref, gatest_ref, u_scr) = refs
    else:
        (x_ref, mod_ref, w_ref, b_ref, bt_ref, wgh_ref, wgl_ref, bg_ref,
         proj_ref, projt_ref, gates_ref, gatest_ref, u_scr) = refs
    j = pl.program_id(1)
    tm = u_scr.shape[0]

    @pl.when(j == 0)
    def _():
        x = x_ref[...]
        if has_pos:
            x = x + pos_ref[...]
        mod = mod_ref[...]
        u = x * (1.0 + mod[:, D_MODEL:2 * D_MODEL]) + mod[:, 0:D_MODEL]
        uh, ul = _split2(u)
        u_scr[...] = uh
        wgh = wgh_ref[...]
        gates = _dot(uh, wgh) + _dot(ul, wgh) + _dot(uh, wgl_ref[...]) + bg_ref[...]
        gates_ref[...] = gates
        gatest_ref[...] = gates.T

    is_t = jnp.logical_or(j == 0, jnp.logical_or(j == 2, j == 3))

    @pl.when(is_t)
    def _():
        acc = _dot_nt(w_ref[...], u_scr[...]) + _lane_tile(bt_ref[...], tm)

        @pl.when(j == 3)
        def _():
            projt_ref[...] = _sigmoid(acc.astype(BF16))

        @pl.when(j != 3)
        def _():
            projt_ref[...] = acc.astype(BF16)

    @pl.when(jnp.logical_not(is_t))
    def _():
        acc = _dot(u_scr[...], w_ref[...]) + b_ref[...]

        @pl.when(j == _STEP_K)
        def _():
            proj_ref[...] = (acc * (HEAD_DIM ** -0.5)).astype(BF16)

        @pl.when(j == 4)
        def _():
            proj_ref[...] = acc.astype(BF16)

        @pl.when(j >= 5)
        def _():
            proj_ref[...] = _sigmoid(acc.astype(BF16))


def _inproj(x, pos, mod, row_fn, lw, tm):
    t = x.shape[0]
    n_steps = lw["w_main"].shape[1] // D_MODEL
    has_pos = pos is not None
    in_specs = [pl.BlockSpec((tm, D_MODEL), lambda i, j: (i, 0))]
    args = [x]
    if has_pos:
        pos_blocks = pos.shape[0] // tm
        in_specs.append(pl.BlockSpec((tm, D_MODEL), lambda i, j: (i % pos_blocks, 0)))
        args.append(pos)

    def token_major_block(j):
        return jnp.maximum(j - 3, 0)

    def feature_major_block(j):
        return jnp.clip(j - 1, 0, 2)

    in_specs += [
        pl.BlockSpec((None, 1, mod.shape[2]), lambda i, j: (row_fn(i), 0, 0)),
        pl.BlockSpec((D_MODEL, D_MODEL), lambda i, j: (0, j)),
        pl.BlockSpec((1, D_MODEL), lambda i, j: (0, j)),
        pl.BlockSpec((None, D_MODEL, _LANES), lambda i, j: (feature_major_block(j), 0, 0)),
        pl.BlockSpec((D_MODEL, GATE_PAD), lambda i, j: (0, 0)),
        pl.BlockSpec((D_MODEL, GATE_PAD), lambda i, j: (0, 0)),
        pl.BlockSpec((1, GATE_PAD), lambda i, j: (0, 0)),
    ]
    args += [mod, lw["w_main"], lw["b_main"], lw["b_feat"], lw["wg_hi"], lw["wg_lo"], lw["b_gate"]]
    return pl.pallas_call(
        functools.partial(_inproj_kernel, has_pos),
        grid=(t // tm, n_steps),
        in_specs=in_specs,
        out_specs=[pl.BlockSpec((tm, D_MODEL), lambda i, j: (i, token_major_block(j))),
                   pl.BlockSpec((D_MODEL, tm), lambda i, j: (feature_major_block(j), i)),
                   pl.BlockSpec((tm, GATE_PAD), lambda i, j: (i, 0)),
                   pl.BlockSpec((GATE_PAD, tm), lambda i, j: (0, i))],
        out_shape=[jax.ShapeDtypeStruct((t, _N_TOKEN_MAJOR * D_MODEL), BF16),
                   jax.ShapeDtypeStruct((len(_FEATURE_MAJOR_STEPS) * D_MODEL, t), BF16),
                   jax.ShapeDtypeStruct((t, GATE_PAD), F32),
                   jax.ShapeDtypeStruct((GATE_PAD, t), F32)],
        scratch_shapes=[pltpu.VMEM((tm, D_MODEL), BF16)],
        compiler_params=_params(2),
        name="inproj",
    )(*args)


_SEQ_INPUTS = 10


def _mlstm_kernel(has_state, want_state, nc, n_par, *refs):
    refs = list(refs)
    seq_in = [refs[a * _SEQ_INPUTS:(a + 1) * _SEQ_INPUTS] for a in range(n_par)]
    pos = n_par * _SEQ_INPUTS
    if has_state:
        c0_ref, n0_ref, m0_ref = refs[pos:pos + 3]
        pos += 3
    hf_ref, hb_ref = refs[pos:pos + 2]
    pos += 2
    if want_state:
        co_ref, no_ref, mo_ref = refs[pos:pos + 3]
        pos += 3
    c_all, n_all, m_all = refs[pos:pos + 3]
    step = pl.program_id(1)

    @pl.when(step == 0)
    def _():
        if has_state:
            c_all[...] = c0_ref[...]
            n_all[...] = n0_ref[...]
            m_all[...] = m0_ref[...]
        else:
            c_all[...] = jnp.zeros_like(c_all)
            n_all[...] = jnp.zeros_like(n_all)
            m_all[...] = jnp.zeros_like(m_all)

    row = lax.broadcasted_iota(jnp.int32, (CHUNK, CHUNK), 0)
    col = lax.broadcasted_iota(jnp.int32, (CHUNK, CHUNK), 1)

    streams = []
    for a in range(n_par):
        qf, kf, vf, qb, kb, vb, gf, gb, gtf, gtb = seq_in[a]
        streams.append((a, 0, qf, kf, vf, gf, gtf, hf_ref))
        streams.append((a, 1, qb, kb, vb, gb, gtb, hb_ref))
    for a, d, qt_ref, k_ref, vt_ref, g_ref, gt_ref, h_ref in streams:
        c_scr, n_scr, m_scr = c_all.at[a], n_all.at[a], m_all.at[a]
        visible = (row <= col) if d == 0 else (row >= col)
        tri_t = jnp.where(visible, 1.0, 0.0).astype(BF16)
        tri = jnp.where((col <= row) if d == 0 else (col >= row), 1.0, 0.0).astype(BF16)
        g = g_ref[...]
        g_t = gt_ref[...]
        l1, l2, l3 = _split3(_log_sigmoid(g))
        b_col_all = _dot(tri, l1) + _dot(tri, l2) + _dot(tri, l3)
        t1, t2, t3 = _split3(_log_sigmoid(g_t))
        b_row_all = _dot(t1, tri_t) + _dot(t2, tri_t) + _dot(t3, tri_t)
        last = CHUNK - 1 if d == 0 else 0
        for h in range(N_HEADS):
            u = d * N_HEADS + h
            ci = d * 2 * N_HEADS + h
            cf = ci + N_HEADS
            b_row = b_row_all[cf:cf + 1, :]
            i_row = g_t[ci:ci + 1, :]
            src_col = g[:, ci:ci + 1] - b_col_all[:, cf:cf + 1]
            m = m_scr[u:u + 1, 0:1]
            hs = slice(h * HEAD_DIM, (h + 1) * HEAD_DIM)
            q_t = qt_ref[hs, :]
            k = k_ref[:, hs]
            v_t = vt_ref[hs, :]
            c_state = c_scr[u]
            n_state = n_scr[u:u + 1, :]

            dmat_t = jnp.where(visible, src_col + b_row, -jnp.inf)
            inter = b_row + m
            m_t = jnp.maximum(inter, jnp.max(dmat_t, axis=0, keepdims=True))
            w_st = jnp.exp(dmat_t - m_t)
            a_t = jnp.exp(inter - m_t)
            s_t = _dot(k, q_t) * w_st
            num_t = _dot(v_t, s_t.astype(BF16)) + a_t * _dot(c_state.astype(BF16), q_t)
            n_rows = jnp.broadcast_to(n_state, (_F32_ROWS, HEAD_DIM)).astype(BF16)
            den = jnp.sum(s_t, axis=0, keepdims=True) + a_t * _dot(n_rows, q_t)[0:1, :]
            h_ref[hs, a * CHUNK:(a + 1) * CHUNK] = num_t / jnp.maximum(jnp.abs(den), jnp.exp(-m_t))

            b_last = b_row[:, last:last + 1]
            dec_row = b_last - b_row + i_row
            m_new = jnp.maximum(b_last + m, jnp.max(dec_row, axis=-1, keepdims=True))
            w_row = jnp.exp(dec_row - m_new)
            a_c = jnp.exp(b_last + m - m_new)
            vw_t = (v_t.astype(F32) * w_row).astype(BF16)
            c_scr[u] = a_c * c_state + _dot(vw_t, k)
            w_rows = jnp.broadcast_to(w_row, (_F32_ROWS, CHUNK)).astype(BF16)
            n_scr[u:u + 1, :] = a_c * n_state + _dot(w_rows, k)[0:1, :]
            m_scr[u:u + 1, :] = jnp.broadcast_to(m_new, (1, GATE_PAD))

    if want_state:
        @pl.when(step == nc - 1)
        def _():
            co_ref[...] = c_all[...]
            no_ref[...] = n_all[...]
            mo_ref[...] = m_all[...]


def _mlstm(proj, projt, gates, gatest, n_seq, seq_len, state0, want_state):
    t = proj.shape[0]
    nc = seq_len // CHUNK
    units = 2 * N_HEADS
    has_state = state0 is not None

    n_par = _scan_group(n_seq)
    tile = (CHUNK, D_MODEL)
    tile_t = (D_MODEL, CHUNK)
    in_specs, args = [], []
    for a in range(n_par):
        def fwd(c, a=a):
            return lambda p, k: ((p * n_par + a) * nc + k, c)

        def bwd(c, a=a):
            return lambda p, k: ((p * n_par + a) * nc + nc - 1 - k, c)

        def fwd_t(r, a=a):
            return lambda p, k: (r, (p * n_par + a) * nc + k)

        def bwd_t(r, a=a):
            return lambda p, k: (r, (p * n_par + a) * nc + nc - 1 - k)

        in_specs += [pl.BlockSpec(tile_t, fwd_t(0)), pl.BlockSpec(tile, fwd(0)), pl.BlockSpec(tile_t, fwd_t(1)),
                     pl.BlockSpec(tile_t, bwd_t(0)), pl.BlockSpec(tile, bwd(0)), pl.BlockSpec(tile_t, bwd_t(1)),
                     pl.BlockSpec((CHUNK, GATE_PAD), fwd(0)), pl.BlockSpec((CHUNK, GATE_PAD), bwd(0)),
                     pl.BlockSpec((GATE_PAD, CHUNK), fwd_t(0)), pl.BlockSpec((GATE_PAD, CHUNK), bwd_t(0))]
        args += [projt, proj, projt, projt, proj, projt, gates, gates, gatest, gatest]
    state_specs = [pl.BlockSpec((n_par, units, HEAD_DIM, HEAD_DIM), lambda p, k: (p, 0, 0, 0)),
                   pl.BlockSpec((n_par, units, HEAD_DIM), lambda p, k: (p, 0, 0)),
                   pl.BlockSpec((n_par, units, GATE_PAD), lambda p, k: (p, 0, 0))]
    state_shapes = [jax.ShapeDtypeStruct((n_seq, units, HEAD_DIM, HEAD_DIM), F32),
                    jax.ShapeDtypeStruct((n_seq, units, HEAD_DIM), F32),
                    jax.ShapeDtypeStruct((n_seq, units, GATE_PAD), F32)]
    if has_state:
        in_specs += state_specs
        args += list(state0)
    group_tile = (D_MODEL, n_par * CHUNK)
    out_specs = [pl.BlockSpec(group_tile, lambda p, k: (0, p * nc + k)),
                 pl.BlockSpec(group_tile, lambda p, k: (0, p * nc + nc - 1 - k))]
    out_shape = [jax.ShapeDtypeStruct((D_MODEL, t), F32), jax.ShapeDtypeStruct((D_MODEL, t), F32)]
    if want_state:
        out_specs += state_specs
        out_shape += state_shapes
    return pl.pallas_call(
        functools.partial(_mlstm_kernel, has_state, want_state, nc, n_par),
        grid=(n_seq // n_par, nc),
        in_specs=in_specs,
        out_specs=out_specs,
        out_shape=out_shape,
        scratch_shapes=[pltpu.VMEM((n_par, units, HEAD_DIM, HEAD_DIM), F32),
                        pltpu.VMEM((n_par, units, HEAD_DIM), F32),
                        pltpu.VMEM((n_par, units, GATE_PAD), F32)],
        compiler_params=_params(2),
        name="mlstm",
    )(*args)


def _scan_group(n_seq):
    return 2 if n_seq % 2 == 0 else 1


def _scan_column(seq, chunk, nc, n_par):
    return ((seq // n_par) * nc + chunk) * n_par + seq % n_par


_CONV_ROWS = 64
_LANES = 128


def _tail_kernel(has_pos, tm, tiles_per_seq, alpha, *refs):
    refs = list(refs)
    n_ct = tm // CHUNK
    h_refs = refs[:2 * n_ct]
    so_ref, val_ref, sg_ref, sga_ref, sgb_ref, vp_ref, gp_ref, vn_ref, gn_ref, x_ref = refs[2 * n_ct:2 * n_ct + 10]
    pos = 2 * n_ct + 10
    if has_pos:
        pos_ref = refs[pos]
        pos += 1
    (mod_ref, normw_ref, wa_ref, wc_ref, wo_ref, dww_ref, dwb_ref, clnw_ref, clnb_ref, ln1w_ref, ln1b_ref, wq_ref,
     x1_ref, u2_ref, q_ref, xpad, conv_scr, shifted) = refs[pos:]
    i = pl.program_id(0)

    hsum = jnp.concatenate([h_refs[j][...] + h_refs[n_ct + j][...] for j in range(n_ct)], axis=1)
    parts = []
    for h in range(N_HEADS):
        hh = hsum[h * HEAD_DIM:(h + 1) * HEAD_DIM, :]
        mu = jnp.mean(hh, axis=0, keepdims=True)
        hc = hh - mu
        var = jnp.mean(hc * hc, axis=0, keepdims=True)
        parts.append(hc * lax.rsqrt(var + LN_EPS))
    hn = jnp.concatenate(parts, axis=0) * _lane_tile(normw_ref[...], tm)
    hg = (so_ref[...].astype(F32) * hn).T.astype(BF16)
    branch_a = _dot(hg, wa_ref[...])

    first = (i % tiles_per_seq) == 0
    last = (i % tiles_per_seq) == tiles_per_seq - 1
    keep_prev = jnp.where(first, 0.0, 1.0)
    keep_next = jnp.where(last, 0.0, 1.0)
    xpad[0:CONV_HALO, :] = vp_ref[...].astype(F32) * gp_ref[...].astype(F32) * keep_prev
    xpad[CONV_HALO:CONV_HALO + tm, :] = val_ref[...].astype(F32) * sg_ref[...].astype(F32)
    xpad[CONV_HALO + tm:2 * CONV_HALO + tm, :] = vn_ref[...].astype(F32) * gn_ref[...].astype(F32) * keep_next
    tap0 = CONV_HALO - CONV_WIDTH // 2

    n_shift_rows = tm + 2 * CONV_HALO - _F32_ROWS

    def col_body(c, carry):
        cs = pl.ds(pl.multiple_of(c * _LANES, _LANES), _LANES)
        for r in range(1, _F32_ROWS):
            shifted[r - 1, 0:n_shift_rows, :] = xpad[pl.ds(r, n_shift_rows), cs]
        for rb in range(tm // _CONV_ROWS):
            acc = jnp.broadcast_to(dwb_ref[:, cs], (_CONV_ROWS, _LANES))
            for k in range(CONV_WIDTH):
                tiles, r = divmod(tap0 + k, _F32_ROWS)
                start = tiles * _F32_ROWS + rb * _CONV_ROWS
                if r == 0:
                    src = xpad[pl.ds(start, _CONV_ROWS), cs]
                else:
                    src = shifted[r - 1, pl.ds(start, _CONV_ROWS), :]
                acc = acc + src * dww_ref[k:k + 1, cs]
            conv_scr[pl.ds(rb * _CONV_ROWS, _CONV_ROWS), cs] = acc
        return carry

    lax.fori_loop(0, D_MODEL // _LANES, col_body, 0)
    xc = _layer_norm(conv_scr[...], clnw_ref[...], clnb_ref[...])
    xc = (xc * _sigmoid(xc)).astype(BF16)
    branch_b = _dot(xc, wc_ref[...])

    merged = sga_ref[...].astype(F32) * branch_a + sgb_ref[...].astype(F32) * branch_b
    mix = _dot(merged.astype(BF16), wo_ref[...])

    mod = mod_ref[...]
    gate1 = mod[:, 2 * D_MODEL:3 * D_MODEL]
    shift2 = mod[:, 3 * D_MODEL:4 * D_MODEL]
    scale2 = mod[:, 4 * D_MODEL:5 * D_MODEL]
    x = x_ref[...]
    if has_pos:
        x = x + pos_ref[...]
    x1 = _layer_norm(alpha * x + gate1 * mix, ln1w_ref[...], ln1b_ref[...])
    x1_ref[...] = x1
    u2 = (x1 * (1.0 + scale2) + shift2).astype(BF16)
    u2_ref[...] = u2
    q_ref[...] = _dot(u2, wq_ref[...]).astype(BF16)


def _tail(hf, hb, proj, projt, x, pos, mod, row_fn, seq_len, tm, alpha, lw):
    t = x.shape[0]
    has_pos = pos is not None
    tiles_per_seq = seq_len // tm
    hb_per_tile = tm // CONV_HALO
    n_halo = t // CONV_HALO
    big = (tm, D_MODEL)
    halo = (CONV_HALO, D_MODEL)

    def colspec(c):
        return pl.BlockSpec(big, lambda i: (i, c))

    def prev(c):
        return pl.BlockSpec(halo, lambda i: (jnp.maximum(i * hb_per_tile - 1, 0), c))

    def nxt(c):
        return pl.BlockSpec(halo, lambda i: (jnp.minimum((i + 1) * hb_per_tile, n_halo - 1), c))

    def const(shape):
        return pl.BlockSpec(shape, lambda i: (0,) * len(shape), pipeline_mode=pl.Buffered(1))

    def rowspec(r):
        return pl.BlockSpec((D_MODEL, tm), lambda i: (r, i))

    nc = seq_len // CHUNK
    n_par = _scan_group(t // seq_len)
    n_ct = tm // CHUNK

    def scan_chunk(j):
        return pl.BlockSpec((D_MODEL, CHUNK), lambda i: (
            0, _scan_column(i // tiles_per_seq, (i % tiles_per_seq) * n_ct + j, nc, n_par)))

    in_specs = [scan_chunk(j) for j in range(n_ct)] * 2
    in_specs += [rowspec(2), colspec(1), colspec(2), colspec(3), colspec(4),
                 prev(1), prev(2), nxt(1), nxt(2), colspec(0)]
    args = [hf] * n_ct + [hb] * n_ct + [projt, proj, proj, proj, proj, proj, proj, proj, proj, x]
    if has_pos:
        in_specs.append(pl.BlockSpec(big, lambda i: (i % tiles_per_seq, 0)))
        args.append(pos)
    in_specs += [pl.BlockSpec((None, 1, mod.shape[2]), lambda i: (row_fn(i), 0, 0)),
                 const((D_MODEL, _LANES)), const((D_MODEL, D_MODEL)), const((D_MODEL, D_MODEL)),
                 const((D_MODEL, D_MODEL)),
                 const((32, D_MODEL)), const((1, D_MODEL)), const((1, D_MODEL)), const((1, D_MODEL)),
                 const((1, D_MODEL)), const((1, D_MODEL)), const(lw["wq"].shape)]
    args += [mod, lw["norm_w"], lw["w_a"], lw["w_cout"], lw["w_out"], lw["dw_w"], lw["dw_b"], lw["cln_w"],
             lw["cln_b"], lw["ln1_w"], lw["ln1_b"], lw["wq"]]
    n_q = lw["wq"].shape[1]
    return pl.pallas_call(
        functools.partial(_tail_kernel, has_pos, tm, tiles_per_seq, alpha),
        grid=(t // tm,),
        in_specs=in_specs,
        out_specs=[pl.BlockSpec(big, lambda i: (i, 0)), pl.BlockSpec(big, lambda i: (i, 0)),
                   pl.BlockSpec((tm, n_q), lambda i: (i, 0))],
        out_shape=[jax.ShapeDtypeStruct((t, D_MODEL), F32), jax.ShapeDtypeStruct((t, D_MODEL), BF16),
                   jax.ShapeDtypeStruct((t, n_q), BF16)],
        scratch_shapes=[pltpu.VMEM((tm + 2 * CONV_HALO, D_MODEL), F32), pltpu.VMEM((tm, D_MODEL), F32),
                        pltpu.VMEM((_F32_ROWS - 1, tm + 2 * CONV_HALO, _LANES), F32)],
        compiler_params=_params(1),
        name="mixer_tail",
    )(*args)


_TOK_BLOCKS = 8
_LANE = 128
_ROUTE_TM = _TOK_BLOCKS * _LANE


def _merge_exchange_pairs(n):
    pairs = []
    t = (n - 1).bit_length()
    p = 1 << (t - 1)
    while p > 0:
        q, r, d = 1 << (t - 1), 0, p
        while d > 0:
            pairs.extend((i, i + d) for i in range(n - d) if (i & p) == r)
            d, q, r = q - p, q >> 1, p
        p >>= 1
    return tuple(pairs)


_SORT16 = _merge_exchange_pairs(PEER_TOPK)


def _sort_desc(vals):
    vals = list(vals)
    for i, j in _SORT16:
        vals[i], vals[j] = jnp.maximum(vals[i], vals[j]), jnp.minimum(vals[i], vals[j])
    return vals


def _bitonic_desc(vals):
    vals = list(vals)
    d = len(vals) // 2
    while d > 0:
        for i in range(len(vals)):
            if (i & d) == 0:
                vals[i], vals[i + d] = jnp.maximum(vals[i], vals[i + d]), jnp.minimum(vals[i], vals[i + d])
        d //= 2
    return vals


def _top_merge(a, b):
    n = len(a)
    return _bitonic_desc([jnp.maximum(a[i], b[n - 1 - i]) for i in range(n)])


def _top16(keys):
    if len(keys) == PEER_TOPK:
        return _sort_desc(keys)
    half = len(keys) // 2
    return _top_merge(_top16(keys[:half]), _top16(keys[half:]))


def _prefix_count(test, v):
    t8 = test(v[7])
    t4 = test(jnp.where(t8, v[11], v[3]))
    t2 = test(jnp.where(t8, jnp.where(t4, v[13], v[9]), jnp.where(t4, v[5], v[1])))
    lo = jnp.where(t4, jnp.where(t2, v[6], v[4]), jnp.where(t2, v[2], v[0]))
    hi = jnp.where(t4, jnp.where(t2, v[14], v[12]), jnp.where(t2, v[10], v[8]))
    t1 = test(jnp.where(t8, hi, lo))
    cnt = (jnp.where(t8, 8.0, 0.0) + jnp.where(t4, 4.0, 0.0)) + (jnp.where(t2, 2.0, 0.0) + jnp.where(t1, 1.0, 0.0))
    return jnp.where(test(v[15]), 16.0, cnt)


def _joint_top16(a, b):
    q0 = [a[p] + b[0] for p in range(16)]
    q1 = [a[p] + b[1] for p in range(8)]
    p0 = [a[0] + b[q] for q in range(8, 16)]
    m1 = _bitonic_desc(q1 + p0[::-1])
    m2 = _sort_desc([a[p] + b[q] for q, n in ((2, 5), (3, 4), (4, 3), (5, 2), (6, 2)) for p in range(n)])
    q7 = [a[0] + b[7], a[1] + b[7]]
    t2 = _bitonic_desc(m2[:14] + [jnp.maximum(m2[14], q7[1]), jnp.maximum(m2[15], q7[0])])
    return _top_merge(_top_merge(q0, m1), t2)


def _route_kernel(q_ref, skx_ref, rank2_ref, e2_ref, r_ref, a1_ref, km_rank2, km_e2, km_r, km_a1):
    nb = _TOK_BLOCKS
    keys = []
    for p in range(2):
        qp = jnp.concatenate([q_ref[a * _LANE:(a + 1) * _LANE, p * PEER_NKEYS:(p + 1) * PEER_NKEYS]
                              for a in range(nb)], axis=1)
        sp = _dot_nt(skx_ref[p], qp)
        keys.append([sp[k * nb:(k + 1) * nb, :] for k in range(PEER_NKEYS)])
    s1, s2 = keys
    a = _top16(s1)
    b = _top16(s2)
    top = _joint_top16(a, b)
    tau = top[PEER_TOPK - 1]
    zsum = jnp.ones_like(tau)
    for c in top[1:]:
        zsum = zsum + jnp.exp(c - top[0])
    inv_z = 1.0 / zsum
    b_asc = b[::-1]
    for k in range(PEER_NKEYS):
        rows = slice(k * nb, (k + 1) * nb)
        km_r[rows, :] = _prefix_count(lambda bq, x=s1[k]: x + bq >= tau, b)
        km_a1[rows, :] = jnp.exp(s1[k] - a[0]) * inv_z
        km_rank2[rows, :] = float(PEER_TOPK) - _prefix_count(lambda bq, x=s2[k]: x >= bq, b_asc)
        km_e2[rows, :] = jnp.exp(s2[k] - b[0])
    for blk in range(nb):
        cols = slice(blk * _LANE, (blk + 1) * _LANE)
        rows = pl.ds(blk, PEER_NKEYS, stride=nb)
        rank2_ref[:, cols] = km_rank2[rows, :].astype(BF16)
        e2_ref[:, cols] = km_e2[rows, :].astype(BF16)
        r_ref[:, cols] = km_r[rows, :]
        a1_ref[:, cols] = km_a1[rows, :]


def _route(q, skx):
    t = q.shape[0]
    tm = _ROUTE_TM
    assert t % tm == 0, (t, tm)
    out_blk = pl.BlockSpec((None, PEER_NKEYS, tm), lambda h, i: (h, 0, i))
    shape = (PEER_HEADS, PEER_NKEYS, t)
    km = pltpu.VMEM((PEER_NKEYS * _TOK_BLOCKS, _LANE), F32)
    return pl.pallas_call(
        _route_kernel,
        grid=(PEER_HEADS, t // tm),
        in_specs=[pl.BlockSpec((tm, 2 * PEER_NKEYS), lambda h, i: (i, h)),
                  pl.BlockSpec((None, 2, PEER_NKEYS * _TOK_BLOCKS, _TOK_BLOCKS * PEER_NKEYS),
                               lambda h, i: (h, 0, 0, 0))],
        out_specs=[out_blk, out_blk, out_blk, out_blk],
        out_shape=[jax.ShapeDtypeStruct(shape, BF16), jax.ShapeDtypeStruct(shape, BF16),
                   jax.ShapeDtypeStruct(shape, F32), jax.ShapeDtypeStruct(shape, F32)],
        scratch_shapes=[km, km, km, km],
        compiler_params=_params(2),
        name="peer_route",
    )(q, skx)


_GELU_C = math.sqrt(2.0 / math.pi)
_BF16_ROWS = 16
_F32_ROWS = 8
_MXU_N = 256


def _gelu_tanh(x):
    scale = -2.0 * _GELU_C * math.log2(math.e)
    return x / (1.0 + jnp.exp2(x * (x * x * (scale * 0.044715) + scale)))


def _experts_kernel(ib, n_eblk, alpha, u_ref, ut_ref, vt_ref, rank2_ref, e2_ref, r_ref, a1_ref, x1_ref, mod_ref,
                    lnw_ref, lnb_ref, y_ref, acc_scr, at_scr, p_scr, uT_scr):
    s = pl.program_id(1)

    @pl.when(s == 0)
    def _():
        acc_scr[...] = jnp.zeros_like(acc_scr)
        uT_scr[...] = u_ref[...].astype(F32).T.astype(BF16)

    tm = u_ref.shape[0]
    key_tiles = PEER_NKEYS // _BF16_ROWS
    rows_per_chunk = _MXU_N // PEER_NKEYS
    n_chunks = ib // rows_per_chunk

    def tile_rows(ref, h, ii, cols):
        group = ref[h, pl.ds(pl.multiple_of(s * ib + (ii // _F32_ROWS) * _F32_ROWS, _F32_ROWS), _F32_ROWS), cols]
        row = group[ii % _F32_ROWS:ii % _F32_ROWS + 1, :]
        tile = jnp.broadcast_to(row, (_BF16_ROWS, _MXU_N)).astype(BF16)
        return jnp.concatenate([tile] * key_tiles, axis=0)

    def weights(ii):
        rows = slice(ii * PEER_NKEYS, (ii + 1) * PEER_NKEYS)
        for c in range(tm // _MXU_N):
            cols = slice(c * _MXU_N, (c + 1) * _MXU_N)
            act = _gelu_tanh(at_scr[rows, cols].astype(BF16))
            w = None
            for h in range(PEER_HEADS):
                r_b = tile_rows(r_ref, h, ii, cols)
                a_b = tile_rows(a1_ref, h, ii, cols)
                term = jnp.where(rank2_ref[h, :, cols] < r_b, e2_ref[h, :, cols] * a_b, jnp.zeros((), BF16))
                w = term if w is None else w + term
            p_scr[rows, cols] = w * act

    for ch in range(n_chunks):
        erows = slice(ch * _MXU_N, (ch + 1) * _MXU_N)
        at_scr[erows, :] = _dot(ut_ref[erows, :], uT_scr[...])
    for ch in range(n_chunks):
        erows = slice(ch * _MXU_N, (ch + 1) * _MXU_N)
        for ii in range(ch * rows_per_chunk, (ch + 1) * rows_per_chunk):
            weights(ii)
        acc_scr[...] += _dot(vt_ref[:, erows], p_scr[erows, :])

    @pl.when(s == n_eblk - 1)
    def _():
        mod = mod_ref[...]
        gate2 = mod[:, 5 * D_MODEL:6 * D_MODEL]
        y = alpha * x1_ref[...] + gate2 * acc_scr[...].T
        y_ref[...] = _layer_norm(y, lnw_ref[...], lnb_ref[...])


def _experts(u2, u_tab, vt_tab, rank2, e2, r, a1, x1, mod, row_fn, lnw, lnb, tm, eb, alpha):
    t = u2.shape[0]
    n_exp = u_tab.shape[0]
    ib = eb // PEER_NKEYS
    n_eblk = n_exp // eb
    full = pl.BlockSpec((PEER_HEADS, PEER_NKEYS, tm), lambda i, s: (0, 0, i))
    tok = pl.BlockSpec((tm, D_MODEL), lambda i, s: (i, 0))
    vec = pl.BlockSpec((1, D_MODEL), lambda i, s: (0, 0))
    return pl.pallas_call(
        functools.partial(_experts_kernel, ib, n_eblk, alpha),
        grid=(t // tm, n_eblk),
        in_specs=[tok,
                  pl.BlockSpec((eb, D_MODEL), lambda i, s: (s, 0)),
                  pl.BlockSpec((D_MODEL, eb), lambda i, s: (0, s)),
                  full, full, full, full, tok,
                  pl.BlockSpec((None, 1, mod.shape[2]), lambda i, s: (row_fn(i), 0, 0)),
                  vec, vec],
        out_specs=tok,
        out_shape=jax.ShapeDtypeStruct((t, D_MODEL), F32),
        scratch_shapes=[pltpu.VMEM((D_MODEL, tm), F32), pltpu.VMEM((eb, tm), F32),
                        pltpu.VMEM((eb, tm), BF16), pltpu.VMEM((D_MODEL, tm), BF16)],
        compiler_params=_params(2),
        name="peer_experts",
    )(u2, u_tab, vt_tab, rank2, e2, r, a1, x1, mod, lnw, lnb)


def _grid_pos_embed(n_tokens):
    rows = n_tokens // GRID_W
    quarter = D_MODEL // 4
    freqs = jnp.exp(-math.log(POS_BASE) * jnp.arange(quarter, dtype=F32) / quarter)
    r = jnp.arange(rows, dtype=F32)[:, None] * freqs
    cl = jnp.arange(GRID_W, dtype=F32)[:, None] * freqs
    er = jnp.concatenate([jnp.sin(r), jnp.cos(r)], -1)
    ec = jnp.concatenate([jnp.sin(cl), jnp.cos(cl)], -1)
    emb = jnp.concatenate([jnp.broadcast_to(er[:, None, :], (rows, GRID_W, D_MODEL // 2)),
                           jnp.broadcast_to(ec[None, :, :], (rows, GRID_W, D_MODEL // 2))], -1)
    return emb.reshape(rows * GRID_W, D_MODEL)


def _pick_tile(seq_len, n_tokens, target):
    tm = min(target, seq_len)
    while seq_len % tm or n_tokens % tm:
        tm //= 2
    return tm


def _block(x, pos, mod, n_seq, seq_len, tm_in, row_of_token_tile, state0, want_state, lw, alpha):
    t = x.shape[0]
    proj, projt, gates, gatest = _inproj(x, pos, mod, row_of_token_tile(tm_in), lw, tm_in)
    res = _mlstm(proj, projt, gates, gatest, n_seq, seq_len, state0, want_state)
    hf, hb = res[0], res[1]
    tm_tail = _pick_tile(seq_len, t, 256)
    x1, u2, q = _tail(hf, hb, proj, projt, x, pos, mod, row_of_token_tile(tm_tail), seq_len, tm_tail, alpha, lw)
    tm_p = _pick_tile(seq_len, t, 512)
    rank2, e2, r, a1 = _route(q, lw["skx"])
    y = _experts(u2, lw["peer_u"], lw["peer_vt"], rank2, e2, r, a1, x1, mod, row_of_token_tile(tm_p),
                 lw["ln2_w"], lw["ln2_b"], tm_p, 2048, alpha)
    return y, res[2:]


def kernel(x_prompt, x_sample, state_C, state_n, state_m, c, c_ctx, w_in, b_in, mlstm_norm_w, w_a, conv_dw_w,
           conv_dw_b, conv_ln_w, conv_ln_b, w_conv_out, w_out, w_mod, b_mod, ln1_w, ln1_b, ln2_w, ln2_b,
           peer_w_query, peer_subkeys, peer_u, peer_v):
    depth = w_in.shape[0]
    alpha = (2.0 * depth) ** 0.25
    bsz, seq, _ = x_prompt.shape
    dbsz, dseq, _ = x_sample.shape
    units = 2 * N_HEADS
    gate_off = 4 * D_MODEL

    ctx = x_prompt.reshape(bsz * seq, D_MODEL)
    lat = x_sample.reshape(dbsz * dseq, D_MODEL)
    pos = _grid_pos_embed(dseq)
    n_rows = 1 + dbsz
    pad_rows = (-n_rows) % 8
    cvec = jnp.concatenate([c_ctx[None, :], c, jnp.zeros((pad_rows, D_MODEL), F32)], axis=0)

    new_c, new_n, new_m = [], [], []
    for l in range(depth):
        vec = lambda a: a[l].reshape(1, -1)
        w_l = w_in[l]
        b_l = b_in[l]
        wg = jnp.pad(w_l[:, gate_off:gate_off + N_GATES], ((0, 0), (0, GATE_PAD - N_GATES)))
        wg_hi = wg.astype(BF16)
        wg_lo = (wg - wg_hi.astype(F32)).astype(BF16)
        b_gate = jnp.pad(b_l[gate_off:gate_off + N_GATES], (0, GATE_PAD - N_GATES))
        w_cols = jnp.concatenate([w_l[:, :gate_off], w_l[:, gate_off + N_GATES:]], axis=1).astype(BF16)
        b_cols = jnp.concatenate([b_l[:gate_off], b_l[gate_off + N_GATES:]])
        n_steps = w_cols.shape[1] // D_MODEL
        blocks = [w_cols[:, s * D_MODEL:(s + 1) * D_MODEL] for s in range(n_steps)]
        lanes = lambda col: jnp.broadcast_to(col[:, None], (col.shape[0], _LANES))
        lw = {
            "w_main": jnp.concatenate([blk.T if s in _FEATURE_MAJOR_STEPS else blk for s, blk in enumerate(blocks)],
                                      axis=1),
            "b_main": b_cols.reshape(1, -1),
            "b_feat": jnp.stack([lanes(b_cols[s * D_MODEL:(s + 1) * D_MODEL]) for s in _FEATURE_MAJOR_STEPS]),
            "wg_hi": wg_hi, "wg_lo": wg_lo, "b_gate": b_gate.reshape(1, -1),
            "norm_w": lanes(mlstm_norm_w[l]), "w_a": w_a[l].astype(BF16), "w_cout": w_conv_out[l].astype(BF16),
            "w_out": w_out[l].astype(BF16),
            "dw_w": jnp.pad(conv_dw_w[l], ((0, 32 - CONV_WIDTH), (0, 0))), "dw_b": vec(conv_dw_b),
            "cln_w": vec(conv_ln_w), "cln_b": vec(conv_ln_b), "ln1_w": vec(ln1_w), "ln1_b": vec(ln1_b),
            "ln2_w": vec(ln2_w), "ln2_b": vec(ln2_b),
            "wq": peer_w_query[l].astype(BF16),
            "skx": jnp.einsum("hpkd,ab->hpkabd", peer_subkeys[l].astype(BF16), jnp.eye(_TOK_BLOCKS, dtype=BF16))
            .reshape(PEER_HEADS, 2, PEER_NKEYS * _TOK_BLOCKS, _TOK_BLOCKS * PEER_NKEYS),
            "peer_u": peer_u[l].astype(BF16),
            "peer_vt": peer_v[l].astype(BF16).T,
        }
        mod = _modulation(cvec, w_mod[l], b_mod[l]).reshape(n_rows + pad_rows, 1, 6 * D_MODEL)

        ctx, (c_fin, n_fin, m_fin) = _block(ctx, None, mod, bsz, seq, _pick_tile(bsz * seq, bsz * seq, 1024),
                                             lambda tm: (lambda i: 0), None, True, lw, alpha)
        new_c.append(c_fin.reshape(bsz, 2, N_HEADS, HEAD_DIM, HEAD_DIM))
        new_n.append(n_fin.reshape(bsz, 2, N_HEADS, HEAD_DIM))
        new_m.append(m_fin[:, :, 0].reshape(bsz, 2, N_HEADS))

        state0 = (state_C[:, l].reshape(dbsz, units, HEAD_DIM, HEAD_DIM),
                  state_n[:, l].reshape(dbsz, units, HEAD_DIM),
                  jnp.broadcast_to(state_m[:, l].reshape(dbsz, units, 1), (dbsz, units, GATE_PAD)))
        lat, _ = _block(lat, pos, mod, dbsz, dseq, _pick_tile(dseq, dbsz * dseq, 1024),
                        lambda tm: (lambda i: 1 + (i * tm) // dseq), state0, False, lw, alpha)

    return (ctx.reshape(bsz, seq, D_MODEL), lat.reshape(dbsz, dseq, D_MODEL),
            jnp.stack(new_c, axis=1), jnp.stack(new_n, axis=1), jnp.stack(new_m, axis=1))
```

```python
import functools
import math

import jax
import jax.numpy as jnp
from jax import lax
from jax.experimental import pallas as pl
from jax.experimental.pallas import tpu as pltpu

F32 = jnp.float32
BF16 = jnp.bfloat16

D_MODEL = 1024
N_HEADS = 4
HEAD_DIM = 256
CHUNK = 256
CONV_WIDTH = 31
CONV_HALO = 16
N_GATES = 16
GATE_PAD = 128
GRID_W = 64
POS_BASE = 10000.0
LN_EPS = 1e-6
PEER_HEADS = 8
PEER_NKEYS = 128
PEER_TOPK = 16
V7X_VMEM_LIMIT = 56 * 1024 * 1024


def _params(n_axes):
    return pltpu.CompilerParams(dimension_semantics=("arbitrary",) * n_axes,
                                vmem_limit_bytes=V7X_VMEM_LIMIT)


def _sigmoid(x):
    return 0.5 + 0.5 * jnp.tanh(0.5 * x)


def _log_sigmoid(x):
    return jnp.minimum(x, 0.0) - jnp.log(1.0 + jnp.exp(-jnp.abs(x)))


def _dot(a, b):
    return jnp.dot(a, b, preferred_element_type=F32)


def _dot_nt(a, b):
    return lax.dot_general(a, b, (((1,), (1,)), ((), ())), preferred_element_type=F32)


def _split2(x):
    hi = x.astype(BF16)
    lo = (x - hi.astype(F32)).astype(BF16)
    return hi, lo


def _split3(x):
    a = x.astype(BF16)
    r = x - a.astype(F32)
    b = r.astype(BF16)
    c = (r - b.astype(F32)).astype(BF16)
    return a, b, c


def _layer_norm(x, w, b):
    mu = jnp.mean(x, axis=-1, keepdims=True)
    xc = x - mu
    var = jnp.mean(xc * xc, axis=-1, keepdims=True)
    return xc * lax.rsqrt(var + LN_EPS) * w + b


def _mod_kernel(c_ref, w_ref, b_ref, o_ref):
    c = c_ref[...]
    s = c * _sigmoid(c)
    sh, sl = _split2(s)
    wh, wl = _split2(w_ref[...])
    o_ref[...] = _dot(sh, wh) + _dot(sl, wh) + _dot(sh, wl) + b_ref[...]


def _modulation(cvec, w_mod, b_mod):
    rows = cvec.shape[0]
    tn = 1536
    n = w_mod.shape[1]
    return pl.pallas_call(
        _mod_kernel,
        grid=(n // tn,),
        in_specs=[pl.BlockSpec((rows, D_MODEL), lambda j: (0, 0)),
                  pl.BlockSpec((D_MODEL, tn), lambda j: (0, j)),
                  pl.BlockSpec((1, tn), lambda j: (0, j))],
        out_specs=pl.BlockSpec((rows, tn), lambda j: (0, j)),
        out_shape=jax.ShapeDtypeStruct((rows, n), F32),
        compiler_params=_params(1),
        name="modulation",
    )(cvec, w_mod, b_mod.reshape(1, n))


_STEP_K = 1
_FEATURE_MAJOR_STEPS = (0, 2, 3)
_N_TOKEN_MAJOR = 5


def _lane_tile(block, n_lanes):
    return jnp.concatenate([block] * (n_lanes // block.shape[1]), axis=1)


def _inproj_kernel(has_pos, *refs):
    if has_pos:
        (x_ref, pos_ref, mod_ref, w_ref, b_ref, bt_ref, wgh_ref, wgl_ref, bg_ref,
         proj_ref, projt_ref, gates_ref, gatest_ref, u_scr) = refs
    else:
        (x_ref, mod_ref, w_ref, b_ref, bt_ref, wgh_ref, wgl_ref, bg_ref,
         proj_ref, projt_ref, gates_ref, gatest_ref, u_scr) = refs
    j = pl.program_id(1)
    tm = u_scr.shape[0]

    @pl.when(j == 0)
    def _():
        x = x_ref[...]
        if has_pos:
            x = x + pos_ref[...]
        mod = mod_ref[...]
        u = x * (1.0 + mod[:, D_MODEL:2 * D_MODEL]) + mod[:, 0:D_MODEL]
        uh, ul = _split2(u)
        u_scr[...] = uh
        wgh = wgh_ref[...]
        gates = _dot(uh, wgh) + _dot(ul, wgh) + _dot(uh, wgl_ref[...]) + bg_ref[...]
        gates_ref[...] = gates
        gatest_ref[...] = gates.T

    is_t = jnp.logical_or(j == 0, jnp.logical_or(j == 2, j == 3))

    @pl.when(is_t)
    def _():
        acc = _dot_nt(w_ref[...], u_scr[...]) + _lane_tile(bt_ref[...], tm)

        @pl.when(j == 3)
        def _():
            projt_ref[...] = _sigmoid(acc.astype(BF16))

        @pl.when(j != 3)
        def _():
            projt_ref[...] = acc.astype(BF16)

    @pl.when(jnp.logical_not(is_t))
    def _():
        acc = _dot(u_scr[...], w_ref[...]) + b_ref[...]

        @pl.when(j == _STEP_K)
        def _():
            proj_ref[...] = (acc * (HEAD_DIM ** -0.5)).astype(BF16)

        @pl.when(j == 4)
        def _():
            proj_ref[...] = acc.astype(BF16)

        @pl.when(j >= 5)
        def _():
            proj_ref[...] = _sigmoid(acc.astype(BF16))


def _inproj(x, pos, mod, row_fn, lw, tm):
    t = x.shape[0]
    n_steps = lw["w_main"].shape[1] // D_MODEL
    has_pos = pos is not None
    in_specs = [pl.BlockSpec((tm, D_MODEL), lambda i, j: (i, 0))]
    args = [x]
    if has_pos:
        pos_blocks = pos.shape[0] // tm
        in_specs.append(pl.BlockSpec((tm, D_MODEL), lambda i, j: (i % pos_blocks, 0)))
        args.append(pos)

    def token_major_block(j):
        return jnp.maximum(j - 3, 0)

    def feature_major_block(j):
        return jnp.clip(j - 1, 0, 2)

    in_specs += [
        pl.BlockSpec((None, 1, mod.shape[2]), lambda i, j: (row_fn(i), 0, 0)),
        pl.BlockSpec((D_MODEL, D_MODEL), lambda i, j: (0, j)),
        pl.BlockSpec((1, D_MODEL), lambda i, j: (0, j)),
        pl.BlockSpec((None, D_MODEL, _LANES), lambda i, j: (feature_major_block(j), 0, 0)),
        pl.BlockSpec((D_MODEL, GATE_PAD), lambda i, j: (0, 0)),
        pl.BlockSpec((D_MODEL, GATE_PAD), lambda i, j: (0, 0)),
        pl.BlockSpec((1, GATE_PAD), lambda i, j: (0, 0)),
    ]
    args += [mod, lw["w_main"], lw["b_main"], lw["b_feat"], lw["wg_hi"], lw["wg_lo"], lw["b_gate"]]
    return pl.pallas_call(
        functools.partial(_inproj_kernel, has_pos),
        grid=(t // tm, n_steps),
        in_specs=in_specs,
        out_specs=[pl.BlockSpec((tm, D_MODEL), lambda i, j: (i, token_major_block(j))),
                   pl.BlockSpec((D_MODEL, tm), lambda i, j: (feature_major_block(j), i)),
                   pl.BlockSpec((tm, GATE_PAD), lambda i, j: (i, 0)),
                   pl.BlockSpec((GATE_PAD, tm), lambda i, j: (0, i))],
        out_shape=[jax.ShapeDtypeStruct((t, _N_TOKEN_MAJOR * D_MODEL), BF16),
                   jax.ShapeDtypeStruct((len(_FEATURE_MAJOR_STEPS) * D_MODEL, t), BF16),
                   jax.ShapeDtypeStruct((t, GATE_PAD), F32),
                   jax.ShapeDtypeStruct((GATE_PAD, t), F32)],
        scratch_shapes=[pltpu.VMEM((tm, D_MODEL), BF16)],
        compiler_params=_params(2),
        name="inproj",
    )(*args)


_SEQ_INPUTS = 10


def _mlstm_kernel(has_state, want_state, nc, n_par, *refs):
    refs = list(refs)
    seq_in = [refs[a * _SEQ_INPUTS:(a + 1) * _SEQ_INPUTS] for a in range(n_par)]
    pos = n_par * _SEQ_INPUTS
    if has_state:
        c0_ref, n0_ref, m0_ref = refs[pos:pos + 3]
        pos += 3
    hf_ref, hb_ref = refs[pos:pos + 2]
    pos += 2
    if want_state:
        co_ref, no_ref, mo_ref = refs[pos:pos + 3]
        pos += 3
    c_all, n_all, m_all = refs[pos:pos + 3]
    step = pl.program_id(1)

    @pl.when(step == 0)
    def _():
        if has_state:
            c_all[...] = c0_ref[...]
            n_all[...] = n0_ref[...]
            m_all[...] = m0_ref[...]
        else:
            c_all[...] = jnp.zeros_like(c_all)
            n_all[...] = jnp.zeros_like(n_all)
            m_all[...] = jnp.zeros_like(m_all)

    row = lax.broadcasted_iota(jnp.int32, (CHUNK, CHUNK), 0)
    col = lax.broadcasted_iota(jnp.int32, (CHUNK, CHUNK), 1)

    streams = []
    for a in range(n_par):
        qf, kf, vf, qb, kb, vb, gf, gb, gtf, gtb = seq_in[a]
        streams.append((a, 0, qf, kf, vf, gf, gtf, hf_ref))
        streams.append((a, 1, qb, kb, vb, gb, gtb, hb_ref))
    for a, d, qt_ref, k_ref, vt_ref, g_ref, gt_ref, h_ref in streams:
        c_scr, n_scr, m_scr = c_all.at[a], n_all.at[a], m_all.at[a]
        visible = (row <= col) if d == 0 else (row >= col)
        tri_t = jnp.where(visible, 1.0, 0.0).astype(BF16)
        g_t = gt_ref[...]
        t1, t2, t3 = _split3(_log_sigmoid(g_t))
        b_row_all = _dot(t1, tri_t) + _dot(t2, tri_t) + _dot(t3, tri_t)
        src_all = (g_t - pltpu.roll(b_row_all, GATE_PAD - N_HEADS, axis=0)).T
        last = CHUNK - 1 if d == 0 else 0
        for h in range(N_HEADS):
            u = d * N_HEADS + h
            ci = d * 2 * N_HEADS + h
            cf = ci + N_HEADS
            b_row = b_row_all[cf:cf + 1, :]
            i_row = g_t[ci:ci + 1, :]
            src_col = src_all[:, ci:ci + 1]
            m = m_scr[u:u + 1, 0:1]
            hs = slice(h * HEAD_DIM, (h + 1) * HEAD_DIM)
            q_t = qt_ref[hs, :]
            k = k_ref[:, hs]
            v_t = vt_ref[hs, :]
            c_state = c_scr[u]
            n_state = n_scr[u:u + 1, :]

            dmat_t = jnp.where(visible, src_col + b_row, -jnp.inf)
            inter = b_row + m
            m_t = jnp.maximum(inter, jnp.max(dmat_t, axis=0, keepdims=True))
            w_st = jnp.exp(dmat_t - m_t)
            a_t = jnp.exp(inter - m_t)
            s_t = _dot(k, q_t) * w_st
            num_t = _dot(v_t, s_t.astype(BF16)) + a_t * _dot(c_state.astype(BF16), q_t)
            n_rows = jnp.broadcast_to(n_state, (_F32_ROWS, HEAD_DIM)).astype(BF16)
            den = jnp.sum(s_t, axis=0, keepdims=True) + a_t * _dot(n_rows, q_t)[0:1, :]
            h_ref[hs, a * CHUNK:(a + 1) * CHUNK] = num_t / jnp.maximum(jnp.abs(den), jnp.exp(-m_t))

            b_last = b_row[:, last:last + 1]
            dec_row = b_last - b_row + i_row
            m_new = jnp.maximum(b_last + m, jnp.max(dec_row, axis=-1, keepdims=True))
            w_row = jnp.exp(dec_row - m_new)
            a_c = jnp.exp(b_last + m - m_new)
            vw_t = (v_t.astype(F32) * w_row).astype(BF16)
            c_scr[u] = a_c * c_state + _dot(vw_t, k)
            w_rows = jnp.broadcast_to(w_row, (_F32_ROWS, CHUNK)).astype(BF16)
            n_scr[u:u + 1, :] = a_c * n_state + _dot(w_rows, k)[0:1, :]
            m_scr[u:u + 1, :] = jnp.broadcast_to(m_new, (1, GATE_PAD))

    if want_state:
        @pl.when(step == nc - 1)
        def _():
            co_ref[...] = c_all[...]
            no_ref[...] = n_all[...]
            mo_ref[...] = m_all[...]


def _mlstm(proj, projt, gates, gatest, n_seq, seq_len, state0, want_state):
    t = proj.shape[0]
    nc = seq_len // CHUNK
    units = 2 * N_HEADS
    has_state = state0 is not None

    n_par = _scan_group(n_seq)
    tile = (CHUNK, D_MODEL)
    tile_t = (D_MODEL, CHUNK)
    in_specs, args = [], []
    for a in range(n_par):
        def fwd(c, a=a):
            return lambda p, k: ((p * n_par + a) * nc + k, c)

        def bwd(c, a=a):
            return lambda p, k: ((p * n_par + a) * nc + nc - 1 - k, c)

        def fwd_t(r, a=a):
            return lambda p, k: (r, (p * n_par + a) * nc + k)

        def bwd_t(r, a=a):
            return lambda p, k: (r, (p * n_par + a) * nc + nc - 1 - k)

        in_specs += [pl.BlockSpec(tile_t, fwd_t(0)), pl.BlockSpec(tile, fwd(0)), pl.BlockSpec(tile_t, fwd_t(1)),
                     pl.BlockSpec(tile_t, bwd_t(0)), pl.BlockSpec(tile, bwd(0)), pl.BlockSpec(tile_t, bwd_t(1)),
                     pl.BlockSpec((CHUNK, GATE_PAD), fwd(0)), pl.BlockSpec((CHUNK, GATE_PAD), bwd(0)),
                     pl.BlockSpec((GATE_PAD, CHUNK), fwd_t(0)), pl.BlockSpec((GATE_PAD, CHUNK), bwd_t(0))]
        args += [projt, proj, projt, projt, proj, projt, gates, gates, gatest, gatest]
    state_specs = [pl.BlockSpec((n_par, units, HEAD_DIM, HEAD_DIM), lambda p, k: (p, 0, 0, 0)),
                   pl.BlockSpec((n_par, units, HEAD_DIM), lambda p, k: (p, 0, 0)),
                   pl.BlockSpec((n_par, units, GATE_PAD), lambda p, k: (p, 0, 0))]
    state_shapes = [jax.ShapeDtypeStruct((n_seq, units, HEAD_DIM, HEAD_DIM), F32),
                    jax.ShapeDtypeStruct((n_seq, units, HEAD_DIM), F32),
                    jax.ShapeDtypeStruct((n_seq, units, GATE_PAD), F32)]
    if has_state:
        in_specs += state_specs
        args += list(state0)
    group_tile = (D_MODEL, n_par * CHUNK)
    out_specs = [pl.BlockSpec(group_tile, lambda p, k: (0, p * nc + k)),
                 pl.BlockSpec(group_tile, lambda p, k: (0, p * nc + nc - 1 - k))]
    out_shape = [jax.ShapeDtypeStruct((D_MODEL, t), F32), jax.ShapeDtypeStruct((D_MODEL, t), F32)]
    if want_state:
        out_specs += state_specs
        out_shape += state_shapes
    return pl.pallas_call(
        functools.partial(_mlstm_kernel, has_state, want_state, nc, n_par),
        grid=(n_seq // n_par, nc),
        in_specs=in_specs,
        out_specs=out_specs,
        out_shape=out_shape,
        scratch_shapes=[pltpu.VMEM((n_par, units, HEAD_DIM, HEAD_DIM), F32),
                        pltpu.VMEM((n_par, units, HEAD_DIM), F32),
                        pltpu.VMEM((n_par, units, GATE_PAD), F32)],
        compiler_params=_params(2),
        name="mlstm",
    )(*args)


def _scan_group(n_seq):
    return 2 if n_seq % 2 == 0 else 1


def _scan_column(seq, chunk, nc, n_par):
    return ((seq // n_par) * nc + chunk) * n_par + seq % n_par


_CONV_ROWS = 64
_LANES = 128


def _tail_kernel(has_pos, tm, tiles_per_seq, alpha, *refs):
    refs = list(refs)
    n_ct = tm // CHUNK
    h_refs = refs[:2 * n_ct]
    so_ref, val_ref, sg_ref, sga_ref, sgb_ref, vp_ref, gp_ref, vn_ref, gn_ref, x_ref = refs[2 * n_ct:2 * n_ct + 10]
    pos = 2 * n_ct + 10
    if has_pos:
        pos_ref = refs[pos]
        pos += 1
    (mod_ref, normw_ref, wa_ref, wc_ref, wo_ref, dww_ref, dwb_ref, clnw_ref, clnb_ref, ln1w_ref, ln1b_ref, wq_ref,
     x1_ref, u2_ref, q_ref, xpad, conv_scr, shifted) = refs[pos:]
    i = pl.program_id(0)

    hsum = jnp.concatenate([h_refs[j][...] + h_refs[n_ct + j][...] for j in range(n_ct)], axis=1)
    parts = []
    for h in range(N_HEADS):
        hh = hsum[h * HEAD_DIM:(h + 1) * HEAD_DIM, :]
        mu = jnp.mean(hh, axis=0, keepdims=True)
        hc = hh - mu
        var = jnp.mean(hc * hc, axis=0, keepdims=True)
        parts.append(hc * lax.rsqrt(var + LN_EPS))
    hn = jnp.concatenate(parts, axis=0) * _lane_tile(normw_ref[...], tm)
    hg = (so_ref[...].astype(F32) * hn).T.astype(BF16)
    branch_a = _dot(hg, wa_ref[...])

    first = (i % tiles_per_seq) == 0
    last = (i % tiles_per_seq) == tiles_per_seq - 1
    keep_prev = jnp.where(first, 0.0, 1.0)
    keep_next = jnp.where(last, 0.0, 1.0)
    xpad[0:CONV_HALO, :] = vp_ref[...].astype(F32) * gp_ref[...].astype(F32) * keep_prev
    xpad[CONV_HALO:CONV_HALO + tm, :] = val_ref[...].astype(F32) * sg_ref[...].astype(F32)
    xpad[CONV_HALO + tm:2 * CONV_HALO + tm, :] = vn_ref[...].astype(F32) * gn_ref[...].astype(F32) * keep_next
    tap0 = CONV_HALO - CONV_WIDTH // 2

    n_shift_rows = tm + 2 * CONV_HALO - _F32_ROWS

    def col_body(c, carry):
        cs = pl.ds(pl.multiple_of(c * _LANES, _LANES), _LANES)
        for r in range(1, _F32_ROWS):
            shifted[r - 1, 0:n_shift_rows, :] = xpad[pl.ds(r, n_shift_rows), cs]
        for rb in range(tm // _CONV_ROWS):
            acc = jnp.broadcast_to(dwb_ref[:, cs], (_CONV_ROWS, _LANES))
            for k in range(CONV_WIDTH):
                tiles, r = divmod(tap0 + k, _F32_ROWS)
                start = tiles * _F32_ROWS + rb * _CONV_ROWS
                if r == 0:
                    src = xpad[pl.ds(start, _CONV_ROWS), cs]
                else:
                    src = shifted[r - 1, pl.ds(start, _CONV_ROWS), :]
                acc = acc + src * dww_ref[k:k + 1, cs]
            conv_scr[pl.ds(rb * _CONV_ROWS, _CONV_ROWS), cs] = acc
        return carry

    lax.fori_loop(0, D_MODEL // _LANES, col_body, 0)
    xc = _layer_norm(conv_scr[...], clnw_ref[...], clnb_ref[...])
    xc = (xc * _sigmoid(xc)).astype(BF16)
    branch_b = _dot(xc, wc_ref[...])

    merged = sga_ref[...].astype(F32) * branch_a + sgb_ref[...].astype(F32) * branch_b
    mix = _dot(merged.astype(BF16), wo_ref[...])

    mod = mod_ref[...]
    gate1 = mod[:, 2 * D_MODEL:3 * D_MODEL]
    shift2 = mod[:, 3 * D_MODEL:4 * D_MODEL]
    scale2 = mod[:, 4 * D_MODEL:5 * D_MODEL]
    x = x_ref[...]
    if has_pos:
        x = x + pos_ref[...]
    x1 = _layer_norm(alpha * x + gate1 * mix, ln1w_ref[...], ln1b_ref[...])
    x1_ref[...] = x1
    u2 = (x1 * (1.0 + scale2) + shift2).astype(BF16)
    u2_ref[...] = u2
    q_ref[...] = _dot(u2, wq_ref[...]).astype(BF16)


def _tail(hf, hb, proj, projt, x, pos, mod, row_fn, seq_len, tm, alpha, lw):
    t = x.shape[0]
    has_pos = pos is not None
    tiles_per_seq = seq_len // tm
    hb_per_tile = tm // CONV_HALO
    n_halo = t // CONV_HALO
    big = (tm, D_MODEL)
    halo = (CONV_HALO, D_MODEL)

    def colspec(c):
        return pl.BlockSpec(big, lambda i: (i, c))

    def prev(c):
        return pl.BlockSpec(halo, lambda i: (jnp.maximum(i * hb_per_tile - 1, 0), c))

    def nxt(c):
        return pl.BlockSpec(halo, lambda i: (jnp.minimum((i + 1) * hb_per_tile, n_halo - 1), c))

    def const(shape):
        return pl.BlockSpec(shape, lambda i: (0,) * len(shape), pipeline_mode=pl.Buffered(1))

    def rowspec(r):
        return pl.BlockSpec((D_MODEL, tm), lambda i: (r, i))

    nc = seq_len // CHUNK
    n_par = _scan_group(t // seq_len)
    n_ct = tm // CHUNK

    def scan_chunk(j):
        return pl.BlockSpec((D_MODEL, CHUNK), lambda i: (
            0, _scan_column(i // tiles_per_seq, (i % tiles_per_seq) * n_ct + j, nc, n_par)))

    in_specs = [scan_chunk(j) for j in range(n_ct)] * 2
    in_specs += [rowspec(2), colspec(1), colspec(2), colspec(3), colspec(4),
                 prev(1), prev(2), nxt(1), nxt(2), colspec(0)]
    args = [hf] * n_ct + [hb] * n_ct + [projt, proj, proj, proj, proj, proj, proj, proj, proj, x]
    if has_pos:
        in_specs.append(pl.BlockSpec(big, lambda i: (i % tiles_per_seq, 0)))
        args.append(pos)
    in_specs += [pl.BlockSpec((None, 1, mod.shape[2]), lambda i: (row_fn(i), 0, 0)),
                 const((D_MODEL, _LANES)), const((D_MODEL, D_MODEL)), const((D_MODEL, D_MODEL)),
                 const((D_MODEL, D_MODEL)),
                 const((32, D_MODEL)), const((1, D_MODEL)), const((1, D_MODEL)), const((1, D_MODEL)),
                 const((1, D_MODEL)), const((1, D_MODEL)), const(lw["wq"].shape)]
    args += [mod, lw["norm_w"], lw["w_a"], lw["w_cout"], lw["w_out"], lw["dw_w"], lw["dw_b"], lw["cln_w"],
             lw["cln_b"], lw["ln1_w"], lw["ln1_b"], lw["wq"]]
    n_q = lw["wq"].shape[1]
    return pl.pallas_call(
        functools.partial(_tail_kernel, has_pos, tm, tiles_per_seq, alpha),
        grid=(t // tm,),
        in_specs=in_specs,
        out_specs=[pl.BlockSpec(big, lambda i: (i, 0)), pl.BlockSpec(big, lambda i: (i, 0)),
                   pl.BlockSpec((tm, n_q), lambda i: (i, 0))],
        out_shape=[jax.ShapeDtypeStruct((t, D_MODEL), F32), jax.ShapeDtypeStruct((t, D_MODEL), BF16),
                   jax.ShapeDtypeStruct((t, n_q), BF16)],
        scratch_shapes=[pltpu.VMEM((tm + 2 * CONV_HALO, D_MODEL), F32), pltpu.VMEM((tm, D_MODEL), F32),
                        pltpu.VMEM((_F32_ROWS - 1, tm + 2 * CONV_HALO, _LANES), F32)],
        compiler_params=_params(1),
        name="mixer_tail",
    )(*args)


_TOK_BLOCKS = 8
_LANE = 128
_ROUTE_TM = _TOK_BLOCKS * _LANE


def _merge_exchange_pairs(n):
    pairs = []
    t = (n - 1).bit_length()
    p = 1 << (t - 1)
    while p > 0:
        q, r, d = 1 << (t - 1), 0, p
        while d > 0:
            pairs.extend((i, i + d) for i in range(n - d) if (i & p) == r)
            d, q, r = q - p, q >> 1, p
        p >>= 1
    return tuple(pairs)


_SORT16 = _merge_exchange_pairs(PEER_TOPK)


def _sort_desc(vals):
    vals = list(vals)
    for i, j in _SORT16:
        vals[i], vals[j] = jnp.maximum(vals[i], vals[j]), jnp.minimum(vals[i], vals[j])
    return vals


def _bitonic_desc(vals):
    vals = list(vals)
    d = len(vals) // 2
    while d > 0:
        for i in range(len(vals)):
            if (i & d) == 0:
                vals[i], vals[i + d] = jnp.maximum(vals[i], vals[i + d]), jnp.minimum(vals[i], vals[i + d])
        d //= 2
    return vals


def _top_merge(a, b):
    n = len(a)
    return _bitonic_desc([jnp.maximum(a[i], b[n - 1 - i]) for i in range(n)])


def _top16(keys):
    if len(keys) == PEER_TOPK:
        return _sort_desc(keys)
    half = len(keys) // 2
    return _top_merge(_top16(keys[:half]), _top16(keys[half:]))


def _prefix_count(test, v):
    t8 = test(v[7])
    t4 = test(jnp.where(t8, v[11], v[3]))
    t2 = test(jnp.where(t8, jnp.where(t4, v[13], v[9]), jnp.where(t4, v[5], v[1])))
    lo = jnp.where(t4, jnp.where(t2, v[6], v[4]), jnp.where(t2, v[2], v[0]))
    hi = jnp.where(t4, jnp.where(t2, v[14], v[12]), jnp.where(t2, v[10], v[8]))
    t1 = test(jnp.where(t8, hi, lo))
    cnt = (jnp.where(t8, 8.0, 0.0) + jnp.where(t4, 4.0, 0.0)) + (jnp.where(t2, 2.0, 0.0) + jnp.where(t1, 1.0, 0.0))
    return jnp.where(test(v[15]), 16.0, cnt)


def _joint_top16(a, b):
    q0 = [a[p] + b[0] for p in range(16)]
    q1 = [a[p] + b[1] for p in range(8)]
    p0 = [a[0] + b[q] for q in range(8, 16)]
    m1 = _bitonic_desc(q1 + p0[::-1])
    m2 = _sort_desc([a[p] + b[q] for q, n in ((2, 5), (3, 4), (4, 3), (5, 2), (6, 2)) for p in range(n)])
    q7 = [a[0] + b[7], a[1] + b[7]]
    t2 = _bitonic_desc(m2[:14] + [jnp.maximum(m2[14], q7[1]), jnp.maximum(m2[15], q7[0])])
    return _top_merge(_top_merge(q0, m1), t2)


def _route_kernel(q_ref, skx_ref, rank2_ref, e2_ref, r_ref, a1_ref, km_rank2, km_e2, km_r, km_a1):
    nb = _TOK_BLOCKS
    keys = []
    for p in range(2):
        qp = jnp.concatenate([q_ref[a * _LANE:(a + 1) * _LANE, p * PEER_NKEYS:(p + 1) * PEER_NKEYS]
                              for a in range(nb)], axis=1)
        sp = _dot_nt(skx_ref[p], qp)
        keys.append([sp[k * nb:(k + 1) * nb, :] for k in range(PEER_NKEYS)])
    s1, s2 = keys
    a = _top16(s1)
    b = _top16(s2)
    top = _joint_top16(a, b)
    tau = top[PEER_TOPK - 1]
    zsum = jnp.ones_like(tau)
    for c in top[1:]:
        zsum = zsum + jnp.exp(c - top[0])
    inv_z = 1.0 / zsum
    b_asc = b[::-1]
    for k in range(PEER_NKEYS):
        rows = slice(k * nb, (k + 1) * nb)
        km_r[rows, :] = _prefix_count(lambda bq, x=s1[k]: x + bq >= tau, b)
        km_a1[rows, :] = jnp.exp(s1[k] - a[0]) * inv_z
        km_rank2[rows, :] = float(PEER_TOPK) - _prefix_count(lambda bq, x=s2[k]: x >= bq, b_asc)
        km_e2[rows, :] = jnp.exp(s2[k] - b[0])
    for blk in range(nb):
        cols = slice(blk * _LANE, (blk + 1) * _LANE)
        rows = pl.ds(blk, PEER_NKEYS, stride=nb)
        rank2_ref[:, cols] = km_rank2[rows, :].astype(BF16)
        e2_ref[:, cols] = km_e2[rows, :].astype(BF16)
        r_ref[:, cols] = km_r[rows, :]
        a1_ref[:, cols] = km_a1[rows, :]


def _route(q, skx):
    t = q.shape[0]
    tm = _ROUTE_TM
    assert t % tm == 0, (t, tm)
    out_blk = pl.BlockSpec((None, PEER_NKEYS, tm), lambda h, i: (h, 0, i))
    shape = (PEER_HEADS, PEER_NKEYS, t)
    km = pltpu.VMEM((PEER_NKEYS * _TOK_BLOCKS, _LANE), F32)
    return pl.pallas_call(
        _route_kernel,
        grid=(PEER_HEADS, t // tm),
        in_specs=[pl.BlockSpec((tm, 2 * PEER_NKEYS), lambda h, i: (i, h)),
                  pl.BlockSpec((None, 2, PEER_NKEYS * _TOK_BLOCKS, _TOK_BLOCKS * PEER_NKEYS),
                               lambda h, i: (h, 0, 0, 0))],
        out_specs=[out_blk, out_blk, out_blk, out_blk],
        out_shape=[jax.ShapeDtypeStruct(shape, BF16), jax.ShapeDtypeStruct(shape, BF16),
                   jax.ShapeDtypeStruct(shape, F32), jax.ShapeDtypeStruct(shape, F32)],
        scratch_shapes=[km, km, km, km],
        compiler_params=_params(2),
        name="peer_route",
    )(q, skx)


_GELU_C = math.sqrt(2.0 / math.pi)
_BF16_ROWS = 16
_F32_ROWS = 8
_MXU_N = 256


def _gelu_tanh(x):
    scale = -2.0 * _GELU_C * math.log2(math.e)
    return x / (1.0 + jnp.exp2(x * (x * x * (scale * 0.044715) + scale)))


def _experts_kernel(ib, n_eblk, alpha, u_ref, ut_ref, vt_ref, rank2_ref, e2_ref, r_ref, a1_ref, x1_ref, mod_ref,
                    lnw_ref, lnb_ref, y_ref, acc_scr, at_scr, p_scr, uT_scr):
    s = pl.program_id(1)

    @pl.when(s == 0)
    def _():
        acc_scr[...] = jnp.zeros_like(acc_scr)
        uT_scr[...] = u_ref[...].astype(F32).T.astype(BF16)

    tm = u_ref.shape[0]
    key_tiles = PEER_NKEYS // _BF16_ROWS
    rows_per_chunk = _MXU_N // PEER_NKEYS
    n_chunks = ib // rows_per_chunk

    def tile_rows(ref, h, ii, cols):
        group = ref[h, pl.ds(pl.multiple_of(s * ib + (ii // _F32_ROWS) * _F32_ROWS, _F32_ROWS), _F32_ROWS), cols]
        row = group[ii % _F32_ROWS:ii % _F32_ROWS + 1, :]
        tile = jnp.broadcast_to(row, (_BF16_ROWS, _MXU_N)).astype(BF16)
        return jnp.concatenate([tile] * key_tiles, axis=0)

    def weights(ii):
        rows = slice(ii * PEER_NKEYS, (ii + 1) * PEER_NKEYS)
        for c in range(tm // _MXU_N):
            cols = slice(c * _MXU_N, (c + 1) * _MXU_N)
            act = _gelu_tanh(at_scr[rows, cols].astype(BF16))
            w = None
            for h in range(PEER_HEADS):
                r_b = tile_rows(r_ref, h, ii, cols)
                a_b = tile_rows(a1_ref, h, ii, cols)
                term = jnp.where(rank2_ref[h, :, cols] < r_b, e2_ref[h, :, cols] * a_b, jnp.zeros((), BF16))
                w = term if w is None else w + term
            p_scr[rows, cols] = w * act

    for ch in range(n_chunks):
        erows = slice(ch * _MXU_N, (ch + 1) * _MXU_N)
        at_scr[erows, :] = _dot(ut_ref[erows, :], uT_scr[...])
    for ch in range(n_chunks):
        erows = slice(ch * _MXU_N, (ch + 1) * _MXU_N)
        for ii in range(ch * rows_per_chunk, (ch + 1) * rows_per_chunk):
            weights(ii)
        acc_scr[...] += _dot(vt_ref[:, erows], p_scr[erows, :])

    @pl.when(s == n_eblk - 1)
    def _():
        mod = mod_ref[...]
        gate2 = mod[:, 5 * D_MODEL:6 * D_MODEL]
        y = alpha * x1_ref[...] + gate2 * acc_scr[...].T
        y_ref[...] = _layer_norm(y, lnw_ref[...], lnb_ref[...])


def _experts(u2, u_tab, vt_tab, rank2, e2, r, a1, x1, mod, row_fn, lnw, lnb, tm, eb, alpha):
    t = u2.shape[0]
    n_exp = u_tab.shape[0]
    ib = eb // PEER_NKEYS
    n_eblk = n_exp // eb
    full = pl.BlockSpec((PEER_HEADS, PEER_NKEYS, tm), lambda i, s: (0, 0, i))
    tok = pl.BlockSpec((tm, D_MODEL), lambda i, s: (i, 0))
    vec = pl.BlockSpec((1, D_MODEL), lambda i, s: (0, 0))
    return pl.pallas_call(
        functools.partial(_experts_kernel, ib, n_eblk, alpha),
        grid=(t // tm, n_eblk),
        in_specs=[tok,
                  pl.BlockSpec((eb, D_MODEL), lambda i, s: (s, 0)),
                  pl.BlockSpec((D_MODEL, eb), lambda i, s: (0, s)),
                  full, full, full, full, tok,
                  pl.BlockSpec((None, 1, mod.shape[2]), lambda i, s: (row_fn(i), 0, 0)),
                  vec, vec],
        out_specs=tok,
        out_shape=jax.ShapeDtypeStruct((t, D_MODEL), F32),
        scratch_shapes=[pltpu.VMEM((D_MODEL, tm), F32), pltpu.VMEM((eb, tm), F32),
                        pltpu.VMEM((eb, tm), BF16), pltpu.VMEM((D_MODEL, tm), BF16)],
        compiler_params=_params(2),
        name="peer_experts",
    )(u2, u_tab, vt_tab, rank2, e2, r, a1, x1, mod, lnw, lnb)


def _grid_pos_embed(n_tokens):
    rows = n_tokens // GRID_W
    quarter = D_MODEL // 4
    freqs = jnp.exp(-math.log(POS_BASE) * jnp.arange(quarter, dtype=F32) / quarter)
    r = jnp.arange(rows, dtype=F32)[:, None] * freqs
    cl = jnp.arange(GRID_W, dtype=F32)[:, None] * freqs
    er = jnp.concatenate([jnp.sin(r), jnp.cos(r)], -1)
    ec = jnp.concatenate([jnp.sin(cl), jnp.cos(cl)], -1)
    emb = jnp.concatenate([jnp.broadcast_to(er[:, None, :], (rows, GRID_W, D_MODEL // 2)),
                           jnp.broadcast_to(ec[None, :, :], (rows, GRID_W, D_MODEL // 2))], -1)
    return emb.reshape(rows * GRID_W, D_MODEL)


def _pick_tile(seq_len, n_tokens, target):
    tm = min(target, seq_len)
    while seq_len % tm or n_tokens % tm:
        tm //= 2
    return tm


def _block(x, pos, mod, n_seq, seq_len, tm_in, row_of_token_tile, state0, want_state, lw, alpha):
    t = x.shape[0]
    proj, projt, gates, gatest = _inproj(x, pos, mod, row_of_token_tile(tm_in), lw, tm_in)
    res = _mlstm(proj, projt, gates, gatest, n_seq, seq_len, state0, want_state)
    hf, hb = res[0], res[1]
    tm_tail = _pick_tile(seq_len, t, 256)
    x1, u2, q = _tail(hf, hb, proj, projt, x, pos, mod, row_of_token_tile(tm_tail), seq_len, tm_tail, alpha, lw)
    tm_p = _pick_tile(seq_len, t, 512)
    rank2, e2, r, a1 = _route(q, lw["skx"])
    y = _experts(u2, lw["peer_u"], lw["peer_vt"], rank2, e2, r, a1, x1, mod, row_of_token_tile(tm_p),
                 lw["ln2_w"], lw["ln2_b"], tm_p, 2048, alpha)
    return y, res[2:]


def kernel(x_prompt, x_sample, state_C, state_n, state_m, c, c_ctx, w_in, b_in, mlstm_norm_w, w_a, conv_dw_w,
           conv_dw_b, conv_ln_w, conv_ln_b, w_conv_out, w_out, w_mod, b_mod, ln1_w, ln1_b, ln2_w, ln2_b,
           peer_w_query, peer_subkeys, peer_u, peer_v):
    depth = w_in.shape[0]
    alpha = (2.0 * depth) ** 0.25
    bsz, seq, _ = x_prompt.shape
    dbsz, dseq, _ = x_sample.shape
    units = 2 * N_HEADS
    gate_off = 4 * D_MODEL

    ctx = x_prompt.reshape(bsz * seq, D_MODEL)
    lat = x_sample.reshape(dbsz * dseq, D_MODEL)
    pos = _grid_pos_embed(dseq)
    n_rows = 1 + dbsz
    pad_rows = (-n_rows) % 8
    cvec = jnp.concatenate([c_ctx[None, :], c, jnp.zeros((pad_rows, D_MODEL), F32)], axis=0)

    new_c, new_n, new_m = [], [], []
    for l in range(depth):
        vec = lambda a: a[l].reshape(1, -1)
        w_l = w_in[l]
        b_l = b_in[l]
        wg = jnp.pad(w_l[:, gate_off:gate_off + N_GATES], ((0, 0), (0, GATE_PAD - N_GATES)))
        wg_hi = wg.astype(BF16)
        wg_lo = (wg - wg_hi.astype(F32)).astype(BF16)
        b_gate = jnp.pad(b_l[gate_off:gate_off + N_GATES], (0, GATE_PAD - N_GATES))
        w_cols = jnp.concatenate([w_l[:, :gate_off], w_l[:, gate_off + N_GATES:]], axis=1).astype(BF16)
        b_cols = jnp.concatenate([b_l[:gate_off], b_l[gate_off + N_GATES:]])
        n_steps = w_cols.shape[1] // D_MODEL
        blocks = [w_cols[:, s * D_MODEL:(s + 1) * D_MODEL] for s in range(n_steps)]
        lanes = lambda col: jnp.broadcast_to(col[:, None], (col.shape[0], _LANES))
        lw = {
            "w_main": jnp.concatenate([blk.T if s in _FEATURE_MAJOR_STEPS else blk for s, blk in enumerate(blocks)],
                                      axis=1),
            "b_main": b_cols.reshape(1, -1),
            "b_feat": jnp.stack([lanes(b_cols[s * D_MODEL:(s + 1) * D_MODEL]) for s in _FEATURE_MAJOR_STEPS]),
            "wg_hi": wg_hi, "wg_lo": wg_lo, "b_gate": b_gate.reshape(1, -1),
            "norm_w": lanes(mlstm_norm_w[l]), "w_a": w_a[l].astype(BF16), "w_cout": w_conv_out[l].astype(BF16),
            "w_out": w_out[l].astype(BF16),
            "dw_w": jnp.pad(conv_dw_w[l], ((0, 32 - CONV_WIDTH), (0, 0))), "dw_b": vec(conv_dw_b),
            "cln_w": vec(conv_ln_w), "cln_b": vec(conv_ln_b), "ln1_w": vec(ln1_w), "ln1_b": vec(ln1_b),
            "ln2_w": vec(ln2_w), "ln2_b": vec(ln2_b),
            "wq": peer_w_query[l].astype(BF16),
            "skx": jnp.einsum("hpkd,ab->hpkabd", peer_subkeys[l].astype(BF16), jnp.eye(_TOK_BLOCKS, dtype=BF16))
            .reshape(PEER_HEADS, 2, PEER_NKEYS * _TOK_BLOCKS, _TOK_BLOCKS * PEER_NKEYS),
            "peer_u": peer_u[l].astype(BF16),
            "peer_vt": peer_v[l].astype(BF16).T,
        }
        mod = _modulation(cvec, w_mod[l], b_mod[l]).reshape(n_rows + pad_rows, 1, 6 * D_MODEL)

        ctx, (c_fin, n_fin, m_fin) = _block(ctx, None, mod, bsz, seq, _pick_tile(bsz * seq, bsz * seq, 1024),
                                             lambda tm: (lambda i: 0), None, True, lw, alpha)
        new_c.append(c_fin.reshape(bsz, 2, N_HEADS, HEAD_DIM, HEAD_DIM))
        new_n.append(n_fin.reshape(bsz, 2, N_HEADS, HEAD_DIM))
        new_m.append(m_fin[:, :, 0].reshape(bsz, 2, N_HEADS))

        state0 = (state_C[:, l].reshape(dbsz, units, HEAD_DIM, HEAD_DIM),
                  state_n[:, l].reshape(dbsz, units, HEAD_DIM),
                  jnp.broadcast_to(state_m[:, l].reshape(dbsz, units, 1), (dbsz, units, GATE_PAD)))
        lat, _ = _block(lat, pos, mod, dbsz, dseq, _pick_tile(dseq, dbsz * dseq, 1024),
                        lambda tm: (lambda i: 1 + (i * tm) // dseq), state0, False, lw, alpha)

    return (ctx.reshape(bsz, seq, D_MODEL), lat.reshape(dbsz, dseq, D_MODEL),
            jnp.stack(new_c, axis=1), jnp.stack(new_n, axis=1), jnp.stack(new_m, axis=1))
```

```python
import functools
import math

import jax
import jax.numpy as jnp
from jax import lax
from jax.experimental import pallas as pl
from jax.experimental.pallas import tpu as pltpu

F32 = jnp.float32
BF16 = jnp.bfloat16

D_MODEL = 1024
N_HEADS = 4
HEAD_DIM = 256
CHUNK = 256
CONV_WIDTH = 31
CONV_HALO = 16
N_GATES = 16
GATE_PAD = 128
GRID_W = 64
POS_BASE = 10000.0
LN_EPS = 1e-6
PEER_HEADS = 8
PEER_NKEYS = 128
PEER_TOPK = 16
V7X_VMEM_LIMIT = 56 * 1024 * 1024


def _params(n_axes):
    return pltpu.CompilerParams(dimension_semantics=("arbitrary",) * n_axes,
                                vmem_limit_bytes=V7X_VMEM_LIMIT)


def _sigmoid(x):
    return 0.5 + 0.5 * jnp.tanh(0.5 * x)


def _log_sigmoid(x):
    return jnp.minimum(x, 0.0) - jnp.log(1.0 + jnp.exp(-jnp.abs(x)))


def _dot(a, b):
    return jnp.dot(a, b, preferred_element_type=F32)


def _dot_nt(a, b):
    return lax.dot_general(a, b, (((1,), (1,)), ((), ())), preferred_element_type=F32)


def _split2(x):
    hi = x.astype(BF16)
    lo = (x - hi.astype(F32)).astype(BF16)
    return hi, lo


def _split3(x):
    a = x.astype(BF16)
    r = x - a.astype(F32)
    b = r.astype(BF16)
    c = (r - b.astype(F32)).astype(BF16)
    return a, b, c


def _layer_norm(x, w, b):
    mu = jnp.mean(x, axis=-1, keepdims=True)
    xc = x - mu
    var = jnp.mean(xc * xc, axis=-1, keepdims=True)
    return xc * lax.rsqrt(var + LN_EPS) * w + b


def _mod_kernel(c_ref, w_ref, b_ref, o_ref):
    c = c_ref[...]
    s = c * _sigmoid(c)
    sh, sl = _split2(s)
    wh, wl = _split2(w_ref[...])
    o_ref[...] = _dot(sh, wh) + _dot(sl, wh) + _dot(sh, wl) + b_ref[...]


def _modulation(cvec, w_mod, b_mod):
    rows = cvec.shape[0]
    tn = 1536
    n = w_mod.shape[1]
    return pl.pallas_call(
        _mod_kernel,
        grid=(n // tn,),
        in_specs=[pl.BlockSpec((rows, D_MODEL), lambda j: (0, 0)),
                  pl.BlockSpec((D_MODEL, tn), lambda j: (0, j)),
                  pl.BlockSpec((1, tn), lambda j: (0, j))],
        out_specs=pl.BlockSpec((rows, tn), lambda j: (0, j)),
        out_shape=jax.ShapeDtypeStruct((rows, n), F32),
        compiler_params=_params(1),
        name="modulation",
    )(cvec, w_mod, b_mod.reshape(1, n))


_STEP_K = 1
_FEATURE_MAJOR_STEPS = (0, 2, 3)
_N_TOKEN_MAJOR = 5


def _lane_tile(block, n_lanes):
    return jnp.concatenate([block] * (n_lanes // block.shape[1]), axis=1)


def _inproj_kernel(has_pos, *refs):
    if has_pos:
        (x_ref, pos_ref, mod_ref, w_ref, b_ref, bt_ref, wgh_ref, wgl_ref, bg_ref,
         proj_ref, projt_ref, gatest_ref, u_scr) = refs
    else:
        (x_ref, mod_ref, w_ref, b_ref, bt_ref, wgh_ref, wgl_ref, bg_ref,
         proj_ref, projt_ref, gatest_ref, u_scr) = refs
    j = pl.program_id(1)
    tm = u_scr.shape[0]

    @pl.when(j == 0)
    def _():
        x = x_ref[...]
        if has_pos:
            x = x + pos_ref[...]
        mod = mod_ref[...]
        u = x * (1.0 + mod[:, D_MODEL:2 * D_MODEL]) + mod[:, 0:D_MODEL]
        uh, ul = _split2(u)
        u_scr[...] = uh
        wgh = wgh_ref[...]
        gates = _dot(uh, wgh) + _dot(ul, wgh) + _dot(uh, wgl_ref[...]) + bg_ref[...]
        gatest_ref[...] = gates.T

    is_t = jnp.logical_or(j == 0, jnp.logical_or(j == 2, j == 3))

    @pl.when(is_t)
    def _():
        acc = _dot_nt(w_ref[...], u_scr[...]) + _lane_tile(bt_ref[...], tm)

        @pl.when(j == 3)
        def _():
            projt_ref[...] = _sigmoid(acc.astype(BF16))

        @pl.when(j != 3)
        def _():
            projt_ref[...] = acc.astype(BF16)

    @pl.when(jnp.logical_not(is_t))
    def _():
        acc = _dot(u_scr[...], w_ref[...]) + b_ref[...]

        @pl.when(j == _STEP_K)
        def _():
            proj_ref[...] = (acc * (HEAD_DIM ** -0.5)).astype(BF16)

        @pl.when(j == 4)
        def _():
            proj_ref[...] = acc.astype(BF16)

        @pl.when(j >= 5)
        def _():
            proj_ref[...] = _sigmoid(acc.astype(BF16))


def _inproj(x, pos, mod, row_fn, lw, tm):
    t = x.shape[0]
    n_steps = lw["w_main"].shape[1] // D_MODEL
    has_pos = pos is not None
    in_specs = [pl.BlockSpec((tm, D_MODEL), lambda i, j: (i, 0))]
    args = [x]
    if has_pos:
        pos_blocks = pos.shape[0] // tm
        in_specs.append(pl.BlockSpec((tm, D_MODEL), lambda i, j: (i % pos_blocks, 0)))
        args.append(pos)

    def token_major_block(j):
        return jnp.maximum(j - 3, 0)

    def feature_major_block(j):
        return jnp.clip(j - 1, 0, 2)

    in_specs += [
        pl.BlockSpec((None, 1, mod.shape[2]), lambda i, j: (row_fn(i), 0, 0)),
        pl.BlockSpec((D_MODEL, D_MODEL), lambda i, j: (0, j)),
        pl.BlockSpec((1, D_MODEL), lambda i, j: (0, j)),
        pl.BlockSpec((None, D_MODEL, _LANES), lambda i, j: (feature_major_block(j), 0, 0)),
        pl.BlockSpec((D_MODEL, GATE_PAD), lambda i, j: (0, 0)),
        pl.BlockSpec((D_MODEL, GATE_PAD), lambda i, j: (0, 0)),
        pl.BlockSpec((1, GATE_PAD), lambda i, j: (0, 0)),
    ]
    args += [mod, lw["w_main"], lw["b_main"], lw["b_feat"], lw["wg_hi"], lw["wg_lo"], lw["b_gate"]]
    return pl.pallas_call(
        functools.partial(_inproj_kernel, has_pos),
        grid=(t // tm, n_steps),
        in_specs=in_specs,
        out_specs=[pl.BlockSpec((tm, D_MODEL), lambda i, j: (i, token_major_block(j))),
                   pl.BlockSpec((D_MODEL, tm), lambda i, j: (feature_major_block(j), i)),
                   pl.BlockSpec((GATE_PAD, tm), lambda i, j: (0, i))],
        out_shape=[jax.ShapeDtypeStruct((t, _N_TOKEN_MAJOR * D_MODEL), BF16),
                   jax.ShapeDtypeStruct((len(_FEATURE_MAJOR_STEPS) * D_MODEL, t), BF16),
                   jax.ShapeDtypeStruct((GATE_PAD, t), F32)],
        scratch_shapes=[pltpu.VMEM((tm, D_MODEL), BF16)],
        compiler_params=_params(2),
        name="inproj",
    )(*args)


_SEQ_INPUTS = 8


def _mlstm_kernel(has_state, want_state, nc, n_par, *refs):
    refs = list(refs)
    seq_in = [refs[a * _SEQ_INPUTS:(a + 1) * _SEQ_INPUTS] for a in range(n_par)]
    pos = n_par * _SEQ_INPUTS
    if has_state:
        c0_ref, n0_ref, m0_ref = refs[pos:pos + 3]
        pos += 3
    hf_ref, hb_ref = refs[pos:pos + 2]
    pos += 2
    if want_state:
        co_ref, no_ref, mo_ref = refs[pos:pos + 3]
        pos += 3
    c_all, n_all, m_all = refs[pos:pos + 3]
    step = pl.program_id(1)

    @pl.when(step == 0)
    def _():
        if has_state:
            c_all[...] = c0_ref[...]
            n_all[...] = n0_ref[...]
            m_all[...] = m0_ref[...]
        else:
            c_all[...] = jnp.zeros_like(c_all)
            n_all[...] = jnp.zeros_like(n_all)
            m_all[...] = jnp.zeros_like(m_all)

    row = lax.broadcasted_iota(jnp.int32, (CHUNK, CHUNK), 0)
    col = lax.broadcasted_iota(jnp.int32, (CHUNK, CHUNK), 1)

    streams = []
    for a in range(n_par):
        qf, kf, vf, qb, kb, vb, gtf, gtb = seq_in[a]
        streams.append((a, 0, qf, kf, vf, gtf, hf_ref))
        streams.append((a, 1, qb, kb, vb, gtb, hb_ref))
    for a, d, qt_ref, k_ref, vt_ref, gt_ref, h_ref in streams:
        c_scr, n_scr, m_scr = c_all.at[a], n_all.at[a], m_all.at[a]
        visible = (row <= col) if d == 0 else (row >= col)
        tri_t = jnp.where(visible, 1.0, 0.0).astype(BF16)
        g_t = gt_ref[...]
        t1, t2, t3 = _split3(_log_sigmoid(g_t))
        b_row_all = _dot(t1, tri_t) + _dot(t2, tri_t) + _dot(t3, tri_t)
        src_all = (g_t - pltpu.roll(b_row_all, GATE_PAD - N_HEADS, axis=0)).T
        last = CHUNK - 1 if d == 0 else 0
        for h in range(N_HEADS):
            u = d * N_HEADS + h
            ci = d * 2 * N_HEADS + h
            cf = ci + N_HEADS
            b_row = b_row_all[cf:cf + 1, :]
            i_row = g_t[ci:ci + 1, :]
            src_col = src_all[:, ci:ci + 1]
            m = m_scr[u:u + 1, 0:1]
            hs = slice(h * HEAD_DIM, (h + 1) * HEAD_DIM)
            q_t = qt_ref[hs, :]
            k = k_ref[:, hs]
            v_t = vt_ref[hs, :]
            c_state = c_scr[u]
            n_state = n_scr[u:u + 1, :]

            dmat_t = jnp.where(visible, src_col + b_row, -jnp.inf)
            inter = b_row + m
            m_t = jnp.maximum(inter, jnp.max(dmat_t, axis=0, keepdims=True))
            w_st = jnp.exp(dmat_t - m_t)
            a_t = jnp.exp(inter - m_t)
            s_t = _dot(k, q_t) * w_st
            num_t = _dot(v_t, s_t.astype(BF16)) + a_t * _dot(c_state.astype(BF16), q_t)
            n_rows = jnp.broadcast_to(n_state, (_F32_ROWS, HEAD_DIM)).astype(BF16)
            den = jnp.sum(s_t, axis=0, keepdims=True) + a_t * _dot(n_rows, q_t)[0:1, :]
            h_ref[hs, a * CHUNK:(a + 1) * CHUNK] = num_t / jnp.maximum(jnp.abs(den), jnp.exp(-m_t))

            b_last = b_row[:, last:last + 1]
            dec_row = b_last - b_row + i_row
            m_new = jnp.maximum(b_last + m, jnp.max(dec_row, axis=-1, keepdims=True))
            w_row = jnp.exp(dec_row - m_new)
            a_c = jnp.exp(b_last + m - m_new)
            vw_t = (v_t.astype(F32) * w_row).astype(BF16)
            c_scr[u] = a_c * c_state + _dot(vw_t, k)
            w_rows = jnp.broadcast_to(w_row, (_F32_ROWS, CHUNK)).astype(BF16)
            n_scr[u:u + 1, :] = a_c * n_state + _dot(w_rows, k)[0:1, :]
            m_scr[u:u + 1, :] = jnp.broadcast_to(m_new, (1, GATE_PAD))

    if want_state:
        @pl.when(step == nc - 1)
        def _():
            co_ref[...] = c_all[...]
            no_ref[...] = n_all[...]
            mo_ref[...] = m_all[...]


def _mlstm(proj, projt, gatest, n_seq, seq_len, state0, want_state):
    t = proj.shape[0]
    nc = seq_len // CHUNK
    units = 2 * N_HEADS
    has_state = state0 is not None

    n_par = _scan_group(n_seq)
    tile = (CHUNK, D_MODEL)
    tile_t = (D_MODEL, CHUNK)
    in_specs, args = [], []
    for a in range(n_par):
        def fwd(c, a=a):
            return lambda p, k: ((p * n_par + a) * nc + k, c)

        def bwd(c, a=a):
            return lambda p, k: ((p * n_par + a) * nc + nc - 1 - k, c)

        def fwd_t(r, a=a):
            return lambda p, k: (r, (p * n_par + a) * nc + k)

        def bwd_t(r, a=a):
            return lambda p, k: (r, (p * n_par + a) * nc + nc - 1 - k)

        in_specs += [pl.BlockSpec(tile_t, fwd_t(0)), pl.BlockSpec(tile, fwd(0)), pl.BlockSpec(tile_t, fwd_t(1)),
                     pl.BlockSpec(tile_t, bwd_t(0)), pl.BlockSpec(tile, bwd(0)), pl.BlockSpec(tile_t, bwd_t(1)),
                     pl.BlockSpec((GATE_PAD, CHUNK), fwd_t(0)), pl.BlockSpec((GATE_PAD, CHUNK), bwd_t(0))]
        args += [projt, proj, projt, projt, proj, projt, gatest, gatest]
    state_specs = [pl.BlockSpec((n_par, units, HEAD_DIM, HEAD_DIM), lambda p, k: (p, 0, 0, 0)),
                   pl.BlockSpec((n_par, units, HEAD_DIM), lambda p, k: (p, 0, 0)),
                   pl.BlockSpec((n_par, units, GATE_PAD), lambda p, k: (p, 0, 0))]
    state_shapes = [jax.ShapeDtypeStruct((n_seq, units, HEAD_DIM, HEAD_DIM), F32),
                    jax.ShapeDtypeStruct((n_seq, units, HEAD_DIM), F32),
                    jax.ShapeDtypeStruct((n_seq, units, GATE_PAD), F32)]
    if has_state:
        in_specs += state_specs
        args += list(state0)
    group_tile = (D_MODEL, n_par * CHUNK)
    out_specs = [pl.BlockSpec(group_tile, lambda p, k: (0, p * nc + k)),
                 pl.BlockSpec(group_tile, lambda p, k: (0, p * nc + nc - 1 - k))]
    out_shape = [jax.ShapeDtypeStruct((D_MODEL, t), F32), jax.ShapeDtypeStruct((D_MODEL, t), F32)]
    if want_state:
        out_specs += state_specs
        out_shape += state_shapes
    return pl.pallas_call(
        functools.partial(_mlstm_kernel, has_state, want_state, nc, n_par),
        grid=(n_seq // n_par, nc),
        in_specs=in_specs,
        out_specs=out_specs,
        out_shape=out_shape,
        scratch_shapes=[pltpu.VMEM((n_par, units, HEAD_DIM, HEAD_DIM), F32),
                        pltpu.VMEM((n_par, units, HEAD_DIM), F32),
                        pltpu.VMEM((n_par, units, GATE_PAD), F32)],
        compiler_params=_params(2),
        name="mlstm",
    )(*args)


def _scan_group(n_seq):
    return 2 if n_seq % 2 == 0 else 1


def _scan_column(seq, chunk, nc, n_par):
    return ((seq // n_par) * nc + chunk) * n_par + seq % n_par


_CONV_ROWS = 64
_LANES = 128


def _tail_kernel(has_pos, tm, tiles_per_seq, alpha, *refs):
    refs = list(refs)
    n_ct = tm // CHUNK
    h_refs = refs[:2 * n_ct]
    so_ref, val_ref, sg_ref, sga_ref, sgb_ref, vp_ref, gp_ref, vn_ref, gn_ref, x_ref = refs[2 * n_ct:2 * n_ct + 10]
    pos = 2 * n_ct + 10
    if has_pos:
        pos_ref = refs[pos]
        pos += 1
    (mod_ref, normw_ref, wa_ref, wc_ref, wo_ref, dww_ref, dwb_ref, clnw_ref, clnb_ref, ln1w_ref, ln1b_ref, wq_ref,
     x1_ref, u2_ref, q_ref, xpad, conv_scr, shifted) = refs[pos:]
    i = pl.program_id(0)

    hsum = jnp.concatenate([h_refs[j][...] + h_refs[n_ct + j][...] for j in range(n_ct)], axis=1)
    parts = []
    for h in range(N_HEADS):
        hh = hsum[h * HEAD_DIM:(h + 1) * HEAD_DIM, :]
        mu = jnp.mean(hh, axis=0, keepdims=True)
        hc = hh - mu
        var = jnp.mean(hc * hc, axis=0, keepdims=True)
        parts.append(hc * lax.rsqrt(var + LN_EPS))
    hn = jnp.concatenate(parts, axis=0) * _lane_tile(normw_ref[...], tm)
    hg = (so_ref[...].astype(F32) * hn).T.astype(BF16)
    branch_a = _dot(hg, wa_ref[...])

    first = (i % tiles_per_seq) == 0
    last = (i % tiles_per_seq) == tiles_per_seq - 1
    keep_prev = jnp.where(first, 0.0, 1.0)
    keep_next = jnp.where(last, 0.0, 1.0)
    xpad[0:CONV_HALO, :] = vp_ref[...].astype(F32) * gp_ref[...].astype(F32) * keep_prev
    xpad[CONV_HALO:CONV_HALO + tm, :] = val_ref[...].astype(F32) * sg_ref[...].astype(F32)
    xpad[CONV_HALO + tm:2 * CONV_HALO + tm, :] = vn_ref[...].astype(F32) * gn_ref[...].astype(F32) * keep_next
    tap0 = CONV_HALO - CONV_WIDTH // 2

    n_shift_rows = tm + 2 * CONV_HALO - _F32_ROWS

    def col_body(c, carry):
        cs = pl.ds(pl.multiple_of(c * _LANES, _LANES), _LANES)
        for r in range(1, _F32_ROWS):
            shifted[r - 1, 0:n_shift_rows, :] = xpad[pl.ds(r, n_shift_rows), cs]
        for rb in range(tm // _CONV_ROWS):
            acc = jnp.broadcast_to(dwb_ref[:, cs], (_CONV_ROWS, _LANES))
            for k in range(CONV_WIDTH):
                tiles, r = divmod(tap0 + k, _F32_ROWS)
                start = tiles * _F32_ROWS + rb * _CONV_ROWS
                if r == 0:
                    src = xpad[pl.ds(start, _CONV_ROWS), cs]
                else:
                    src = shifted[r - 1, pl.ds(start, _CONV_ROWS), :]
                acc = acc + src * dww_ref[k:k + 1, cs]
            conv_scr[pl.ds(rb * _CONV_ROWS, _CONV_ROWS), cs] = acc
        return carry

    lax.fori_loop(0, D_MODEL // _LANES, col_body, 0)
    xc = _layer_norm(conv_scr[...], clnw_ref[...], clnb_ref[...])
    xc = (xc * _sigmoid(xc)).astype(BF16)
    branch_b = _dot(xc, wc_ref[...])

    merged = sga_ref[...].astype(F32) * branch_a + sgb_ref[...].astype(F32) * branch_b
    mix = _dot(merged.astype(BF16), wo_ref[...])

    mod = mod_ref[...]
    gate1 = mod[:, 2 * D_MODEL:3 * D_MODEL]
    shift2 = mod[:, 3 * D_MODEL:4 * D_MODEL]
    scale2 = mod[:, 4 * D_MODEL:5 * D_MODEL]
    x = x_ref[...]
    if has_pos:
        x = x + pos_ref[...]
    x1 = _layer_norm(alpha * x + gate1 * mix, ln1w_ref[...], ln1b_ref[...])
    x1_ref[...] = x1
    u2 = (x1 * (1.0 + scale2) + shift2).astype(BF16)
    u2_ref[...] = u2
    q_ref[...] = _dot(u2, wq_ref[...]).astype(BF16)


def _tail(hf, hb, proj, projt, x, pos, mod, row_fn, seq_len, tm, alpha, lw):
    t = x.shape[0]
    has_pos = pos is not None
    tiles_per_seq = seq_len // tm
    hb_per_tile = tm // CONV_HALO
    n_halo = t // CONV_HALO
    big = (tm, D_MODEL)
    halo = (CONV_HALO, D_MODEL)

    def colspec(c):
        return pl.BlockSpec(big, lambda i: (i, c))

    def prev(c):
        return pl.BlockSpec(halo, lambda i: (jnp.maximum(i * hb_per_tile - 1, 0), c))

    def nxt(c):
        return pl.BlockSpec(halo, lambda i: (jnp.minimum((i + 1) * hb_per_tile, n_halo - 1), c))

    def const(shape):
        return pl.BlockSpec(shape, lambda i: (0,) * len(shape), pipeline_mode=pl.Buffered(1))

    def rowspec(r):
        return pl.BlockSpec((D_MODEL, tm), lambda i: (r, i))

    nc = seq_len // CHUNK
    n_par = _scan_group(t // seq_len)
    n_ct = tm // CHUNK

    def scan_chunk(j):
        return pl.BlockSpec((D_MODEL, CHUNK), lambda i: (
            0, _scan_column(i // tiles_per_seq, (i % tiles_per_seq) * n_ct + j, nc, n_par)))

    in_specs = [scan_chunk(j) for j in range(n_ct)] * 2
    in_specs += [rowspec(2), colspec(1), colspec(2), colspec(3), colspec(4),
                 prev(1), prev(2), nxt(1), nxt(2), colspec(0)]
    args = [hf] * n_ct + [hb] * n_ct + [projt, proj, proj, proj, proj, proj, proj, proj, proj, x]
    if has_pos:
        in_specs.append(pl.BlockSpec(big, lambda i: (i % tiles_per_seq, 0)))
        args.append(pos)
    in_specs += [pl.BlockSpec((None, 1, mod.shape[2]), lambda i: (row_fn(i), 0, 0)),
                 const((D_MODEL, _LANES)), const((D_MODEL, D_MODEL)), const((D_MODEL, D_MODEL)),
                 const((D_MODEL, D_MODEL)),
                 const((32, D_MODEL)), const((1, D_MODEL)), const((1, D_MODEL)), const((1, D_MODEL)),
                 const((1, D_MODEL)), const((1, D_MODEL)), const(lw["wq"].shape)]
    args += [mod, lw["norm_w"], lw["w_a"], lw["w_cout"], lw["w_out"], lw["dw_w"], lw["dw_b"], lw["cln_w"],
             lw["cln_b"], lw["ln1_w"], lw["ln1_b"], lw["wq"]]
    n_q = lw["wq"].shape[1]
    return pl.pallas_call(
        functools.partial(_tail_kernel, has_pos, tm, tiles_per_seq, alpha),
        grid=(t // tm,),
        in_specs=in_specs,
        out_specs=[pl.BlockSpec(big, lambda i: (i, 0)), pl.BlockSpec(big, lambda i: (i, 0)),
                   pl.BlockSpec((tm, n_q), lambda i: (i, 0))],
        out_shape=[jax.ShapeDtypeStruct((t, D_MODEL), F32), jax.ShapeDtypeStruct((t, D_MODEL), BF16),
                   jax.ShapeDtypeStruct((t, n_q), BF16)],
        scratch_shapes=[pltpu.VMEM((tm + 2 * CONV_HALO, D_MODEL), F32), pltpu.VMEM((tm, D_MODEL), F32),
                        pltpu.VMEM((_F32_ROWS - 1, tm + 2 * CONV_HALO, _LANES), F32)],
        compiler_params=_params(1),
        name="mixer_tail",
    )(*args)


_TOK_BLOCKS = 8
_LANE = 128
_ROUTE_TM = _TOK_BLOCKS * _LANE


def _merge_exchange_pairs(n):
    pairs = []
    t = (n - 1).bit_length()
    p = 1 << (t - 1)
    while p > 0:
        q, r, d = 1 << (t - 1), 0, p
        while d > 0:
            pairs.extend((i, i + d) for i in range(n - d) if (i & p) == r)
            d, q, r = q - p, q >> 1, p
        p >>= 1
    return tuple(pairs)


_SORT16 = _merge_exchange_pairs(PEER_TOPK)


def _sort_desc(vals):
    vals = list(vals)
    for i, j in _SORT16:
        vals[i], vals[j] = jnp.maximum(vals[i], vals[j]), jnp.minimum(vals[i], vals[j])
    return vals


def _bitonic_desc(vals):
    vals = list(vals)
    d = len(vals) // 2
    while d > 0:
        for i in range(len(vals)):
            if (i & d) == 0:
                vals[i], vals[i + d] = jnp.maximum(vals[i], vals[i + d]), jnp.minimum(vals[i], vals[i + d])
        d //= 2
    return vals


def _top_merge(a, b):
    n = len(a)
    return _bitonic_desc([jnp.maximum(a[i], b[n - 1 - i]) for i in range(n)])


def _top16(keys):
    if len(keys) == PEER_TOPK:
        return _sort_desc(keys)
    half = len(keys) // 2
    return _top_merge(_top16(keys[:half]), _top16(keys[half:]))


def _prefix_count(test, v):
    t8 = test(v[7])
    t4 = test(jnp.where(t8, v[11], v[3]))
    t2 = test(jnp.where(t8, jnp.where(t4, v[13], v[9]), jnp.where(t4, v[5], v[1])))
    lo = jnp.where(t4, jnp.where(t2, v[6], v[4]), jnp.where(t2, v[2], v[0]))
    hi = jnp.where(t4, jnp.where(t2, v[14], v[12]), jnp.where(t2, v[10], v[8]))
    t1 = test(jnp.where(t8, hi, lo))
    cnt = (jnp.where(t8, 8.0, 0.0) + jnp.where(t4, 4.0, 0.0)) + (jnp.where(t2, 2.0, 0.0) + jnp.where(t1, 1.0, 0.0))
    return jnp.where(test(v[15]), 16.0, cnt)


def _joint_top16(a, b):
    q0 = [a[p] + b[0] for p in range(16)]
    q1 = [a[p] + b[1] for p in range(8)]
    p0 = [a[0] + b[q] for q in range(8, 16)]
    m1 = _bitonic_desc(q1 + p0[::-1])
    m2 = _sort_desc([a[p] + b[q] for q, n in ((2, 5), (3, 4), (4, 3), (5, 2), (6, 2)) for p in range(n)])
    q7 = [a[0] + b[7], a[1] + b[7]]
    t2 = _bitonic_desc(m2[:14] + [jnp.maximum(m2[14], q7[1]), jnp.maximum(m2[15], q7[0])])
    return _top_merge(_top_merge(q0, m1), t2)


def _route_kernel(q_ref, skx_ref, rank2_ref, e2_ref, r_ref, a1_ref, km_rank2, km_e2, km_r, km_a1):
    nb = _TOK_BLOCKS
    keys = []
    for p in range(2):
        qp = jnp.concatenate([q_ref[a * _LANE:(a + 1) * _LANE, p * PEER_NKEYS:(p + 1) * PEER_NKEYS]
                              for a in range(nb)], axis=1)
        sp = _dot_nt(skx_ref[p], qp)
        keys.append([sp[k * nb:(k + 1) * nb, :] for k in range(PEER_NKEYS)])
    s1, s2 = keys
    a = _top16(s1)
    b = _top16(s2)
    top = _joint_top16(a, b)
    tau = top[PEER_TOPK - 1]
    zsum = jnp.ones_like(tau)
    for c in top[1:]:
        zsum = zsum + jnp.exp(c - top[0])
    inv_z = 1.0 / zsum
    b_asc = b[::-1]
    for k in range(PEER_NKEYS):
        rows = slice(k * nb, (k + 1) * nb)
        km_r[rows, :] = _prefix_count(lambda bq, x=s1[k]: x + bq >= tau, b)
        km_a1[rows, :] = jnp.exp(s1[k] - a[0]) * inv_z
        km_rank2[rows, :] = float(PEER_TOPK) - _prefix_count(lambda bq, x=s2[k]: x >= bq, b_asc)
        km_e2[rows, :] = jnp.exp(s2[k] - b[0])
    for blk in range(nb):
        cols = slice(blk * _LANE, (blk + 1) * _LANE)
        rows = pl.ds(blk, PEER_NKEYS, stride=nb)
        rank2_ref[:, cols] = km_rank2[rows, :].astype(BF16)
        e2_ref[:, cols] = km_e2[rows, :].astype(BF16)
        r_ref[:, cols] = km_r[rows, :]
        a1_ref[:, cols] = km_a1[rows, :]


def _route(q, skx):
    t = q.shape[0]
    tm = _ROUTE_TM
    assert t % tm == 0, (t, tm)
    out_blk = pl.BlockSpec((None, PEER_NKEYS, tm), lambda h, i: (h, 0, i))
    shape = (PEER_HEADS, PEER_NKEYS, t)
    km = pltpu.VMEM((PEER_NKEYS * _TOK_BLOCKS, _LANE), F32)
    return pl.pallas_call(
        _route_kernel,
        grid=(PEER_HEADS, t // tm),
        in_specs=[pl.BlockSpec((tm, 2 * PEER_NKEYS), lambda h, i: (i, h)),
                  pl.BlockSpec((None, 2, PEER_NKEYS * _TOK_BLOCKS, _TOK_BLOCKS * PEER_NKEYS),
                               lambda h, i: (h, 0, 0, 0))],
        out_specs=[out_blk, out_blk, out_blk, out_blk],
        out_shape=[jax.ShapeDtypeStruct(shape, BF16), jax.ShapeDtypeStruct(shape, BF16),
                   jax.ShapeDtypeStruct(shape, F32), jax.ShapeDtypeStruct(shape, F32)],
        scratch_shapes=[km, km, km, km],
        compiler_params=_params(2),
        name="peer_route",
    )(q, skx)


_GELU_C = math.sqrt(2.0 / math.pi)
_BF16_ROWS = 16
_F32_ROWS = 8
_MXU_N = 256


def _gelu_tanh(x):
    scale = -2.0 * _GELU_C * math.log2(math.e)
    return x / (1.0 + jnp.exp2(x * (x * x * (scale * 0.044715) + scale)))


def _experts_kernel(ib, n_eblk, alpha, u_ref, ut_ref, vt_ref, rank2_ref, e2_ref, r_ref, a1_ref, x1_ref, mod_ref,
                    lnw_ref, lnb_ref, y_ref, acc_scr, at_scr, p_scr, uT_scr):
    s = pl.program_id(1)

    @pl.when(s == 0)
    def _():
        acc_scr[...] = jnp.zeros_like(acc_scr)
        uT_scr[...] = u_ref[...].astype(F32).T.astype(BF16)

    tm = u_ref.shape[0]
    key_tiles = PEER_NKEYS // _BF16_ROWS
    rows_per_chunk = _MXU_N // PEER_NKEYS
    n_chunks = ib // rows_per_chunk

    def tile_rows(ref, h, ii, cols):
        group = ref[h, pl.ds(pl.multiple_of(s * ib + (ii // _F32_ROWS) * _F32_ROWS, _F32_ROWS), _F32_ROWS), cols]
        row = group[ii % _F32_ROWS:ii % _F32_ROWS + 1, :]
        tile = jnp.broadcast_to(row, (_BF16_ROWS, _MXU_N)).astype(BF16)
        return jnp.concatenate([tile] * key_tiles, axis=0)

    def weights(ii):
        rows = slice(ii * PEER_NKEYS, (ii + 1) * PEER_NKEYS)
        for c in range(tm // _MXU_N):
            cols = slice(c * _MXU_N, (c + 1) * _MXU_N)
            act = _gelu_tanh(at_scr[rows, cols].astype(BF16))
            w = None
            for h in range(PEER_HEADS):
                r_b = tile_rows(r_ref, h, ii, cols)
                a_b = tile_rows(a1_ref, h, ii, cols)
                term = jnp.where(rank2_ref[h, :, cols] < r_b, e2_ref[h, :, cols] * a_b, jnp.zeros((), BF16))
                w = term if w is None else w + term
            p_scr[rows, cols] = w * act

    for ch in range(n_chunks):
        erows = slice(ch * _MXU_N, (ch + 1) * _MXU_N)
        at_scr[erows, :] = _dot(ut_ref[erows, :], uT_scr[...])
    for ch in range(n_chunks):
        erows = slice(ch * _MXU_N, (ch + 1) * _MXU_N)
        for ii in range(ch * rows_per_chunk, (ch + 1) * rows_per_chunk):
            weights(ii)
        acc_scr[...] += _dot(vt_ref[:, erows], p_scr[erows, :])

    @pl.when(s == n_eblk - 1)
    def _():
        mod = mod_ref[...]
        gate2 = mod[:, 5 * D_MODEL:6 * D_MODEL]
        y = alpha * x1_ref[...] + gate2 * acc_scr[...].T
        y_ref[...] = _layer_norm(y, lnw_ref[...], lnb_ref[...])


def _experts(u2, u_tab, vt_tab, rank2, e2, r, a1, x1, mod, row_fn, lnw, lnb, tm, eb, alpha):
    t = u2.shape[0]
    n_exp = u_tab.shape[0]
    ib = eb // PEER_NKEYS
    n_eblk = n_exp // eb
    full = pl.BlockSpec((PEER_HEADS, PEER_NKEYS, tm), lambda i, s: (0, 0, i))
    tok = pl.BlockSpec((tm, D_MODEL), lambda i, s: (i, 0))
    vec = pl.BlockSpec((1, D_MODEL), lambda i, s: (0, 0))
    return pl.pallas_call(
        functools.partial(_experts_kernel, ib, n_eblk, alpha),
        grid=(t // tm, n_eblk),
        in_specs=[tok,
                  pl.BlockSpec((eb, D_MODEL), lambda i, s: (s, 0)),
                  pl.BlockSpec((D_MODEL, eb), lambda i, s: (0, s)),
                  full, full, full, full, tok,
                  pl.BlockSpec((None, 1, mod.shape[2]), lambda i, s: (row_fn(i), 0, 0)),
                  vec, vec],
        out_specs=tok,
        out_shape=jax.ShapeDtypeStruct((t, D_MODEL), F32),
        scratch_shapes=[pltpu.VMEM((D_MODEL, tm), F32), pltpu.VMEM((eb, tm), F32),
                        pltpu.VMEM((eb, tm), BF16), pltpu.VMEM((D_MODEL, tm), BF16)],
        compiler_params=_params(2),
        name="peer_experts",
    )(u2, u_tab, vt_tab, rank2, e2, r, a1, x1, mod, lnw, lnb)


def _grid_pos_embed(n_tokens):
    rows = n_tokens // GRID_W
    quarter = D_MODEL // 4
    freqs = jnp.exp(-math.log(POS_BASE) * jnp.arange(quarter, dtype=F32) / quarter)
    r = jnp.arange(rows, dtype=F32)[:, None] * freqs
    cl = jnp.arange(GRID_W, dtype=F32)[:, None] * freqs
    er = jnp.concatenate([jnp.sin(r), jnp.cos(r)], -1)
    ec = jnp.concatenate([jnp.sin(cl), jnp.cos(cl)], -1)
    emb = jnp.concatenate([jnp.broadcast_to(er[:, None, :], (rows, GRID_W, D_MODEL // 2)),
                           jnp.broadcast_to(ec[None, :, :], (rows, GRID_W, D_MODEL // 2))], -1)
    return emb.reshape(rows * GRID_W, D_MODEL)


def _pick_tile(seq_len, n_tokens, target):
    tm = min(target, seq_len)
    while seq_len % tm or n_tokens % tm:
        tm //= 2
    return tm


def _block(x, pos, mod, n_seq, seq_len, tm_in, row_of_token_tile, state0, want_state, lw, alpha):
    t = x.shape[0]
    proj, projt, gatest = _inproj(x, pos, mod, row_of_token_tile(tm_in), lw, tm_in)
    res = _mlstm(proj, projt, gatest, n_seq, seq_len, state0, want_state)
    hf, hb = res[0], res[1]
    tm_tail = _pick_tile(seq_len, t, 256)
    x1, u2, q = _tail(hf, hb, proj, projt, x, pos, mod, row_of_token_tile(tm_tail), seq_len, tm_tail, alpha, lw)
    tm_p = _pick_tile(seq_len, t, 512)
    rank2, e2, r, a1 = _route(q, lw["skx"])
    y = _experts(u2, lw["peer_u"], lw["peer_vt"], rank2, e2, r, a1, x1, mod, row_of_token_tile(tm_p),
                 lw["ln2_w"], lw["ln2_b"], tm_p, 2048, alpha)
    return y, res[2:]


def kernel(x_prompt, x_sample, state_C, state_n, state_m, c, c_ctx, w_in, b_in, mlstm_norm_w, w_a, conv_dw_w,
           conv_dw_b, conv_ln_w, conv_ln_b, w_conv_out, w_out, w_mod, b_mod, ln1_w, ln1_b, ln2_w, ln2_b,
           peer_w_query, peer_subkeys, peer_u, peer_v):
    depth = w_in.shape[0]
    alpha = (2.0 * depth) ** 0.25
    bsz, seq, _ = x_prompt.shape
    dbsz, dseq, _ = x_sample.shape
    units = 2 * N_HEADS
    gate_off = 4 * D_MODEL

    ctx = x_prompt.reshape(bsz * seq, D_MODEL)
    lat = x_sample.reshape(dbsz * dseq, D_MODEL)
    pos = _grid_pos_embed(dseq)
    n_rows = 1 + dbsz
    pad_rows = (-n_rows) % 8
    cvec = jnp.concatenate([c_ctx[None, :], c, jnp.zeros((pad_rows, D_MODEL), F32)], axis=0)

    new_c, new_n, new_m = [], [], []
    for l in range(depth):
        vec = lambda a: a[l].reshape(1, -1)
        w_l = w_in[l]
        b_l = b_in[l]
        wg = jnp.pad(w_l[:, gate_off:gate_off + N_GATES], ((0, 0), (0, GATE_PAD - N_GATES)))
        wg_hi = wg.astype(BF16)
        wg_lo = (wg - wg_hi.astype(F32)).astype(BF16)
        b_gate = jnp.pad(b_l[gate_off:gate_off + N_GATES], (0, GATE_PAD - N_GATES))
        w_cols = jnp.concatenate([w_l[:, :gate_off], w_l[:, gate_off + N_GATES:]], axis=1).astype(BF16)
        b_cols = jnp.concatenate([b_l[:gate_off], b_l[gate_off + N_GATES:]])
        n_steps = w_cols.shape[1] // D_MODEL
        blocks = [w_cols[:, s * D_MODEL:(s + 1) * D_MODEL] for s in range(n_steps)]
        lanes = lambda col: jnp.broadcast_to(col[:, None], (col.shape[0], _LANES))
        lw = {
            "w_main": jnp.concatenate([blk.T if s in _FEATURE_MAJOR_STEPS else blk for s, blk in enumerate(blocks)],
                                      axis=1),
            "b_main": b_cols.reshape(1, -1),
            "b_feat": jnp.stack([lanes(b_cols[s * D_MODEL:(s + 1) * D_MODEL]) for s in _FEATURE_MAJOR_STEPS]),
            "wg_hi": wg_hi, "wg_lo": wg_lo, "b_gate": b_gate.reshape(1, -1),
            "norm_w": lanes(mlstm_norm_w[l]), "w_a": w_a[l].astype(BF16), "w_cout": w_conv_out[l].astype(BF16),
            "w_out": w_out[l].astype(BF16),
            "dw_w": jnp.pad(conv_dw_w[l], ((0, 32 - CONV_WIDTH), (0, 0))), "dw_b": vec(conv_dw_b),
            "cln_w": vec(conv_ln_w), "cln_b": vec(conv_ln_b), "ln1_w": vec(ln1_w), "ln1_b": vec(ln1_b),
            "ln2_w": vec(ln2_w), "ln2_b": vec(ln2_b),
            "wq": peer_w_query[l].astype(BF16),
            "skx": jnp.einsum("hpkd,ab->hpkabd", peer_subkeys[l].astype(BF16), jnp.eye(_TOK_BLOCKS, dtype=BF16))
            .reshape(PEER_HEADS, 2, PEER_NKEYS * _TOK_BLOCKS, _TOK_BLOCKS * PEER_NKEYS),
            "peer_u": peer_u[l].astype(BF16),
            "peer_vt": peer_v[l].astype(BF16).T,
        }
        mod = _modulation(cvec, w_mod[l], b_mod[l]).reshape(n_rows + pad_rows, 1, 6 * D_MODEL)

        ctx, (c_fin, n_fin, m_fin) = _block(ctx, None, mod, bsz, seq, _pick_tile(bsz * seq, bsz * seq, 1024),
                                             lambda tm: (lambda i: 0), None, True, lw, alpha)
        new_c.append(c_fin.reshape(bsz, 2, N_HEADS, HEAD_DIM, HEAD_DIM))
        new_n.append(n_fin.reshape(bsz, 2, N_HEADS, HEAD_DIM))
        new_m.append(m_fin[:, :, 0].reshape(bsz, 2, N_HEADS))

        state0 = (state_C[:, l].reshape(dbsz, units, HEAD_DIM, HEAD_DIM),
                  state_n[:, l].reshape(dbsz, units, HEAD_DIM),
                  jnp.broadcast_to(state_m[:, l].reshape(dbsz, units, 1), (dbsz, units, GATE_PAD)))
        lat, _ = _block(lat, pos, mod, dbsz, dseq, _pick_tile(dseq, dbsz * dseq, 1024),
                        lambda tm: (lambda i: 1 + (i * tm) // dseq), state0, False, lw, alpha)

    return (ctx.reshape(bsz, seq, D_MODEL), lat.reshape(dbsz, dseq, D_MODEL),
            jnp.stack(new_c, axis=1), jnp.stack(new_n, axis=1), jnp.stack(new_m, axis=1))
```
